```python
import jax, jax.numpy as jnp
from jax import lax
import numpy as np

D_MODEL = 1024
BATCH = 2
SEQ = 8192
DEPTH = 2

GRID_W = 64
CTX_LEN = 256
EPS = 1e-6
LRU_WIDTH = 256
LRU_BLOCKS = 4
LRU_BLOCK = LRU_WIDTH // LRU_BLOCKS
LRU_C = 8.0
CONV_W = 4
CONV_LEFT = 2
MLA_HEADS = 8
MLA_NOPE = 64
MLA_ROPE = 32
MLA_V = 64
MLA_Q_RANK = 256
MLA_KV_RANK = 128
ROPE_AXIS_FREQ = MLA_ROPE // 4
ROPE_BASE = 10000.0
Q_BLOCK = 128
ML_HEADS = 4
ML_HEAD_DIM = 64
ML_WIDTH = ML_HEADS * ML_HEAD_DIM
ML_CHUNK = 128
D_MIX = LRU_WIDTH + MLA_HEADS * MLA_V + ML_WIDTH
IN_SPLITS = (LRU_WIDTH, LRU_WIDTH, MLA_Q_RANK, MLA_KV_RANK, MLA_ROPE, ML_WIDTH, ML_WIDTH)
D_IN = sum(IN_SPLITS)
N_EXPERTS = 32
TOP_K = 4
D_EXPERT = 1024
SWIGLU_LIMIT = 7.0
SWIGLU_ALPHA = 1.702
MOE_BLOCK = 256

kernel_name = 'hybrid_lru_mla_mlstm_moe_dit'

F32 = jnp.float32


def rmsnorm(x, g):
    xf = x.astype(F32)
    y = xf * lax.rsqrt(jnp.mean(xf * xf, axis=-1, keepdims=True) + EPS)
    return (y * g.astype(F32)).astype(x.dtype)


def modulation(cvec, w_mod, b_mod):
    m = jax.nn.silu(cvec) @ w_mod + b_mod
    return [t[:, None, :] for t in jnp.split(m, 6, axis=-1)]


def short_conv(x, w, b):
    S = x.shape[1]
    xp = jnp.pad(x, ((0, 0), (CONV_LEFT, CONV_W - 1 - CONV_LEFT), (0, 0)))
    return sum(xp[:, k:k + S] * w[k] for k in range(CONV_W)) + b


def block_diag(x, w):
    G, I, J = w.shape
    y = jnp.einsum('...gi,gij->...gj', x.reshape(x.shape[:-1] + (G, I)), w)
    return y.reshape(x.shape[:-1] + (G * J,))


def flip_seq(t, rev, axis):
    return jnp.flip(t, axis=axis) if rev else t


def _lin_combine(l, r):
    a1, b1 = l
    a2, b2 = r
    return a1 * a2, a2 * b1 + b2


def linear_scan(a, b, h0):
    A, Bc = lax.associative_scan(_lin_combine, (a, b), axis=1)
    return A * h0[:, None] + Bc


def rglru_coeffs(xc, wa, ba, wx, bx, lam):
    r = jax.nn.sigmoid((block_diag(xc, wa) + ba).astype(F32))
    i = jax.nn.sigmoid((block_diag(xc, wx) + bx).astype(F32))
    log_a = LRU_C * r * jax.nn.log_sigmoid(lam.astype(F32))
    a = jnp.exp(log_a)
    b = jnp.sqrt(-jnp.expm1(2.0 * log_a)) * i * xc.astype(F32)
    return a, b


def rglru_group(u_x, u_g, uc_x, uc_g, conv_w, conv_b, wa, ba, wx, bx, lam, need_ctx):
    xl = short_conv(u_x, conv_w, conv_b)
    xc = short_conv(uc_x, conv_w, conv_b)
    B = u_x.shape[0]
    hs_l, hs_c = [], []
    for d in range(2):
        rev = d == 1
        a_c, b_c = rglru_coeffs(flip_seq(xc, rev, 1), wa[d], ba[d], wx[d], bx[d], lam[d])
        h_c = linear_scan(a_c, b_c, jnp.zeros((B, LRU_WIDTH), F32))
        a_l, b_l = rglru_coeffs(flip_seq(xl, rev, 1), wa[d], ba[d], wx[d], bx[d], lam[d])
        h_l = linear_scan(a_l, b_l, h_c[:, -1])
        hs_l.append(flip_seq(h_l, rev, 1))
        hs_c.append(flip_seq(h_c, rev, 1))
    y_l = (hs_l[0] + hs_l[1]).astype(u_x.dtype) * jax.nn.gelu(u_g)
    y_c = (hs_c[0] + hs_c[1]).astype(uc_x.dtype) * jax.nn.gelu(uc_g) if need_ctx else None
    return y_l, y_c


def rope2d(x, cos, sin):
    xs = x.reshape(x.shape[:-1] + (2, 2, ROPE_AXIS_FREQ)).astype(F32)
    x1, x2 = xs[..., 0, :], xs[..., 1, :]
    out = jnp.stack([x1 * cos - x2 * sin, x2 * cos + x1 * sin], axis=-2)
    return out.reshape(x.shape).astype(x.dtype)


def mla_project(uq, ukv, ukr, q_norm_g, w_qb, kv_norm_g, w_kvb, cos, sin):
    B, S = uq.shape[:2]
    q = (rmsnorm(uq, q_norm_g) @ w_qb).reshape(B, S, MLA_HEADS, MLA_NOPE + MLA_ROPE)
    kv = (rmsnorm(ukv, kv_norm_g) @ w_kvb).reshape(B, S, MLA_HEADS, MLA_NOPE + MLA_V)
    q_nope, q_rope = q[..., :MLA_NOPE], q[..., MLA_NOPE:]
    k_nope, v = kv[..., :MLA_NOPE], kv[..., MLA_NOPE:]
    k_rope = ukr
    if cos is not None:
        q_rope = rope2d(q_rope, cos[:, None], sin[:, None])
        k_rope = rope2d(k_rope, cos, sin)
    q = jnp.concatenate([q_nope, q_rope], axis=-1)
    k = jnp.concatenate([k_nope, jnp.broadcast_to(k_rope[:, :, None, :], (B, S, MLA_HEADS, MLA_ROPE))], axis=-1)
    return q, k, v


def attend(q, k, v):
    s = jnp.einsum('bqhd,bkhd->bhqk', q, k).astype(F32) * (MLA_NOPE + MLA_ROPE) ** -0.5
    p = jax.nn.softmax(s, axis=-1).astype(v.dtype)
    return jnp.einsum('bhqk,bkhd->bqhd', p, v)


def mla_group(uq, ukv, ukr, uq_c, ukv_c, ukr_c, q_norm_g, w_qb, kv_norm_g, w_kvb, cos, sin, need_ctx):
    B, S = uq.shape[:2]
    ql, kl, vl = mla_project(uq, ukv, ukr, q_norm_g, w_qb, kv_norm_g, w_kvb, cos, sin)
    qc, kc, vc = mla_project(uq_c, ukv_c, ukr_c, q_norm_g, w_qb, kv_norm_g, w_kvb, None, None)
    k_all = jnp.concatenate([kc, kl], axis=1)
    v_all = jnp.concatenate([vc, vl], axis=1)
    qb = ql.reshape(B, S // Q_BLOCK, Q_BLOCK, MLA_HEADS, MLA_NOPE + MLA_ROPE).transpose(1, 0, 2, 3, 4)
    ob = lax.map(lambda qi: attend(qi, k_all, v_all), qb)
    y_l = ob.transpose(1, 0, 2, 3, 4).reshape(B, S, MLA_HEADS * MLA_V)
    y_c = attend(qc, kc, vc).reshape(B, uq_c.shape[1], MLA_HEADS * MLA_V) if need_ctx else None
    return y_l, y_c


def mlstm_scan(q, k, v, ig, lf, state):
    B, H, S, d = q.shape
    nc = S // ML_CHUNK

    def to_chunks(t):
        return jnp.moveaxis(t.reshape(t.shape[:2] + (nc, ML_CHUNK) + t.shape[3:]), 2, 0)

    pos = jnp.arange(ML_CHUNK)
    lower = pos[:, None] >= pos[None, :]

    def step(carry, xs):
        C, n, m = carry
        qc, kc, vc, ic, fc = xs
        b = jnp.cumsum(fc, axis=-1)
        D = jnp.where(lower, b[..., :, None] - b[..., None, :] + ic[..., None, :], -jnp.inf)
        inter = b + m[..., None]
        m_t = jnp.maximum(inter, jnp.max(D, axis=-1))
        P = jnp.einsum('bhtk,bhsk->bhts', qc, kc) * jnp.exp(D - m_t[..., None])
        w_inter = jnp.exp(inter - m_t)
        num = jnp.einsum('bhts,bhsv->bhtv', P, vc) + w_inter[..., None] * jnp.einsum('bhvk,bhtk->bhtv', C, qc)
        den = jnp.sum(P, axis=-1) + w_inter * jnp.einsum('bhk,bhtk->bht', n, qc)
        h = num / jnp.maximum(jnp.abs(den), jnp.exp(-m_t))[..., None]
        bL = b[..., -1]
        w_s = bL[..., None] - b + ic
        m_new = jnp.maximum(bL + m, jnp.max(w_s, axis=-1))
        decay = jnp.exp(bL + m - m_new)
        w_s = jnp.exp(w_s - m_new[..., None])
        C_new = decay[..., None, None] * C + jnp.einsum('bhs,bhsv,bhsk->bhvk', w_s, vc, kc)
        n_new = decay[..., None] * n + jnp.einsum('bhs,bhsk->bhk', w_s, kc)
        return (C_new, n_new, m_new), h

    state, hs = lax.scan(step, state, (to_chunks(q), to_chunks(k), to_chunks(v), to_chunks(ig), to_chunks(lf)))
    return jnp.moveaxis(hs, 0, 2).reshape(B, H, S, d), state


def mlstm_gates(qkv, w_gate, b_gate):
    g = (qkv @ w_gate + b_gate).astype(F32)
    ig = jnp.swapaxes(g[..., :ML_HEADS], 1, 2)
    lf = jnp.swapaxes(jax.nn.log_sigmoid(g[..., ML_HEADS:]), 1, 2)
    return ig, lf


def mlstm_group(u_x, u_z, uc_x, uc_z, conv_w, conv_b, wq, wk, wv, w_gate, b_gate, norm_g, skip, need_ctx):
    def qkv(ux):
        xc = jax.nn.silu(short_conv(ux, conv_w, conv_b))
        q = block_diag(xc, wq)
        k = block_diag(xc, wk) * ML_HEAD_DIM ** -0.5
        v = block_diag(ux, wv)
        return xc, q, k, v

    def heads(t):
        B, S, _ = t.shape
        return t.reshape(B, S, ML_HEADS, ML_HEAD_DIM).transpose(0, 2, 1, 3).astype(F32)

    def finish(h, xconv, z):
        B, H, S, d = h.shape
        h = h.transpose(0, 2, 1, 3)
        mu = jnp.mean(h, axis=-1, keepdims=True)
        var = jnp.mean(jnp.square(h - mu), axis=-1, keepdims=True)
        h = ((h - mu) * lax.rsqrt(var + EPS)).reshape(B, S, H * d) * norm_g.astype(F32)
        return ((h + skip.astype(F32) * xconv.astype(F32)) * jax.nn.silu(z.astype(F32))).astype(z.dtype)

    xl, ql, kl, vl = qkv(u_x)
    xc, qc, kc, vc = qkv(uc_x)
    qkv_l = jnp.concatenate([ql, kl, vl], axis=-1)
    qkv_c = jnp.concatenate([qc, kc, vc], axis=-1)
    ql, kl, vl, qc, kc, vc = [heads(t) for t in (ql, kl, vl, qc, kc, vc)]
    B = u_x.shape[0]
    hs_l, hs_c = [], []
    for d in range(2):
        rev = d == 1
        ig_c, lf_c = mlstm_gates(qkv_c, w_gate[d], b_gate[d])
        ig_l, lf_l = mlstm_gates(qkv_l, w_gate[d], b_gate[d])
        state0 = (jnp.zeros((B, ML_HEADS, ML_HEAD_DIM, ML_HEAD_DIM), F32),
                  jnp.zeros((B, ML_HEADS, ML_HEAD_DIM), F32),
                  jnp.zeros((B, ML_HEADS), F32))
        h_c, st = mlstm_scan(*[flip_seq(t, rev, 2) for t in (qc, kc, vc, ig_c, lf_c)], state0)
        h_l, _ = mlstm_scan(*[flip_seq(t, rev, 2) for t in (ql, kl, vl, ig_l, lf_l)], st)
        hs_l.append(flip_seq(h_l, rev, 2))
        hs_c.append(flip_seq(h_c, rev, 2))
    y_l = finish(hs_l[0] + hs_l[1], xl, u_z)
    y_c = finish(hs_c[0] + hs_c[1], xc, uc_z) if need_ctx else None
    return y_l, y_c


def moe(t, w_router, b_router, w_gu, b_gu, w_down, b_down):
    T, D = t.shape
    TK = T * TOP_K
    logits = (t @ w_router + b_router).astype(F32)
    top_v, top_e = lax.top_k(logits, TOP_K)
    gate = jax.nn.softmax(top_v, axis=-1)
    flat_e = top_e.reshape(-1)
    flat_tok = jnp.arange(TK, dtype=jnp.int32) // TOP_K
    order = jnp.argsort(flat_e)
    sorted_e = flat_e[order]
    counts = jnp.zeros((N_EXPERTS,), jnp.int32).at[flat_e].add(1)
    start = jnp.cumsum(counts) - counts
    padded = (counts + MOE_BLOCK - 1) // MOE_BLOCK * MOE_BLOCK
    padded_end = jnp.cumsum(padded)
    padded_start = padded_end - padded
    dest = padded_start[sorted_e] + jnp.arange(TK, dtype=jnp.int32) - start[sorted_e]
    n_blocks = -(-TK // MOE_BLOCK) + N_EXPERTS
    P = n_blocks * MOE_BLOCK
    rows_tok = jnp.full((P,), T, jnp.int32).at[dest].set(flat_tok[order])
    rows_gate = jnp.zeros((P,), F32).at[dest].set(gate.reshape(-1)[order])
    block_e = jnp.minimum(jnp.searchsorted(padded_end, jnp.arange(n_blocks, dtype=jnp.int32) * MOE_BLOCK, side='right'), N_EXPERTS - 1)
    t_pad = jnp.concatenate([t, jnp.zeros((1, D), t.dtype)], axis=0)
    xb = t_pad[rows_tok].reshape(n_blocks, MOE_BLOCK, D)

    def expert_block(args):
        xe, e = args
        gu = xe @ w_gu[e] + b_gu[e]
        glu = jnp.minimum(gu[:, :D_EXPERT], SWIGLU_LIMIT)
        lin = jnp.clip(gu[:, D_EXPERT:], -SWIGLU_LIMIT, SWIGLU_LIMIT)
        act = glu * jax.nn.sigmoid(SWIGLU_ALPHA * glu) * (lin + 1)
        return act @ w_down[e] + b_down[e]

    yb = lax.map(expert_block, (xb, block_e))
    out = jnp.zeros((T + 1, D), t.dtype).at[rows_tok].add(yb.reshape(P, D) * rows_gate[:, None].astype(t.dtype))
    return out[:T]


def setup_inputs(seed: int = 0) -> dict:
    state = [jax.random.key(seed)]

    def nk():
        state[0], sub = jax.random.split(state[0])
        return sub

    def nrm(shape, scale):
        return jax.random.normal(nk(), shape, F32) * scale

    def gain(shape):
        return 1.0 + nrm(shape, 0.02)

    L, D = DEPTH, D_MODEL
    u = jax.random.uniform(nk(), (L, 2, LRU_WIDTH), F32, 0.9, 0.999)
    sig = u ** (1.0 / LRU_C)
    lru_lambda = jnp.log(sig) - jnp.log1p(-sig)
    ml_b_gate = jnp.concatenate([nrm((L, 2, ML_HEADS), 0.1),
                                 3.0 + 3.0 * jax.random.uniform(nk(), (L, 2, ML_HEADS), F32)], axis=-1)
    return {
        'x': nrm((BATCH, SEQ, D), 1.0),
        'c': nrm((BATCH, D), 1.0),
        'ctx': nrm((BATCH, CTX_LEN, D), 1.0),
        'c_ctx': nrm((D,), 1.0),
        'norm1_g': gain((L, D)),
        'norm2_g': gain((L, D)),
        'w_mod': nrm((L, D, 6 * D), 0.5 * D ** -0.5),
        'b_mod': nrm((L, 6 * D), 0.02),
        'w_in': nrm((L, D, D_IN), D ** -0.5),
        'w_out': nrm((L, D_MIX, D), D_MIX ** -0.5),
        'lru_conv_w': nrm((L, CONV_W, LRU_WIDTH), CONV_W ** -0.5),
        'lru_conv_b': nrm((L, LRU_WIDTH), 0.02),
        'lru_wa': nrm((L, 2, LRU_BLOCKS, LRU_BLOCK, LRU_BLOCK), LRU_BLOCK ** -0.5),
        'lru_ba': nrm((L, 2, LRU_WIDTH), 0.1),
        'lru_wx': nrm((L, 2, LRU_BLOCKS, LRU_BLOCK, LRU_BLOCK), LRU_BLOCK ** -0.5),
        'lru_bx': nrm((L, 2, LRU_WIDTH), 0.1),
        'lru_lambda': lru_lambda,
        'mla_q_norm_g': gain((L, MLA_Q_RANK)),
        'mla_w_qb': nrm((L, MLA_Q_RANK, MLA_HEADS * (MLA_NOPE + MLA_ROPE)), MLA_Q_RANK ** -0.5),
        'mla_kv_norm_g': gain((L, MLA_KV_RANK)),
        'mla_w_kvb': nrm((L, MLA_KV_RANK, MLA_HEADS * (MLA_NOPE + MLA_V)), MLA_KV_RANK ** -0.5),
        'ml_conv_w': nrm((L, CONV_W, ML_WIDTH), CONV_W ** -0.5),
        'ml_conv_b': nrm((L, ML_WIDTH), 0.02),
        'ml_wq': nrm((L, ML_HEADS, ML_HEAD_DIM, ML_HEAD_DIM), ML_HEAD_DIM ** -0.5),
        'ml_wk': nrm((L, ML_HEADS, ML_HEAD_DIM, ML_HEAD_DIM), ML_HEAD_DIM ** -0.5),
        'ml_wv': nrm((L, ML_HEADS, ML_HEAD_DIM, ML_HEAD_DIM), ML_HEAD_DIM ** -0.5),
        'ml_w_gate': nrm((L, 2, 3 * ML_WIDTH, 2 * ML_HEADS), (3 * ML_WIDTH) ** -0.5),
        'ml_b_gate': ml_b_gate,
        'ml_norm_g': gain((L, ML_WIDTH)),
        'ml_skip': gain((L, ML_WIDTH)),
        'w_router': nrm((L, D, N_EXPERTS), D ** -0.5),
        'b_router': nrm((L, N_EXPERTS), 0.01),
        'w_gu': nrm((L, N_EXPERTS, D, 2 * D_EXPERT), D ** -0.5),
        'b_gu': nrm((L, N_EXPERTS, 2 * D_EXPERT), 0.01),
        'w_down': nrm((L, N_EXPERTS, D_EXPERT, D), D_EXPERT ** -0.5),
        'b_down': nrm((L, N_EXPERTS, D), 0.01),
        'final_g': gain((D,)),
    }


def reference(x, c, ctx, c_ctx, norm1_g, norm2_g, w_mod, b_mod, w_in, w_out,
              lru_conv_w, lru_conv_b, lru_wa, lru_ba, lru_wx, lru_bx, lru_lambda,
              mla_q_norm_g, mla_w_qb, mla_kv_norm_g, mla_w_kvb,
              ml_conv_w, ml_conv_b, ml_wq, ml_wk, ml_wv, ml_w_gate, ml_b_gate, ml_norm_g, ml_skip,
              w_router, b_router, w_gu, b_gu, w_down, b_down, final_g):
    B, S, D = x.shape
    ROWS = S // GRID_W
    row = jnp.repeat(jnp.arange(ROWS, dtype=jnp.int32), GRID_W)
    col = jnp.tile(jnp.arange(GRID_W, dtype=jnp.int32), ROWS)
    freqs = ROPE_BASE ** (-jnp.arange(ROPE_AXIS_FREQ, dtype=F32) / ROPE_AXIS_FREQ)
    ang = jnp.stack([row[:, None] * freqs, col[:, None] * freqs], axis=1)
    cos, sin = jnp.cos(ang), jnp.sin(ang)
    split_at = list(np.cumsum(IN_SPLITS)[:-1])

    x_l, x_c = x, ctx
    for l in range(DEPTH):
        need_ctx = l < DEPTH - 1
        sh1, sc1, g1, sh2, sc2, g2 = modulation(c, w_mod[l], b_mod[l])
        csh1, csc1, cg1, csh2, csc2, cg2 = modulation(c_ctx[None], w_mod[l], b_mod[l])
        h = rmsnorm(x_l, norm1_g[l]) * (1 + sc1) + sh1
        hc = rmsnorm(x_c, norm1_g[l]) * (1 + csc1) + csh1
        a_x, a_g, b_q, b_kv, b_kr, m_x, m_z = jnp.split(h @ w_in[l], split_at, axis=-1)
        ac_x, ac_g, bc_q, bc_kv, bc_kr, mc_x, mc_z = jnp.split(hc @ w_in[l], split_at, axis=-1)
        ya_l, ya_c = rglru_group(a_x, a_g, ac_x, ac_g, lru_conv_w[l], lru_conv_b[l], lru_wa[l], lru_ba[l],
                                 lru_wx[l], lru_bx[l], lru_lambda[l], need_ctx)
        yb_l, yb_c = mla_group(b_q, b_kv, b_kr, bc_q, bc_kv, bc_kr, mla_q_norm_g[l], mla_w_qb[l],
                               mla_kv_norm_g[l], mla_w_kvb[l], cos, sin, need_ctx)
        yc_l, yc_c = mlstm_group(m_x, m_z, mc_x, mc_z, ml_conv_w[l], ml_conv_b[l], ml_wq[l], ml_wk[l], ml_wv[l],
                                 ml_w_gate[l], ml_b_gate[l], ml_norm_g[l], ml_skip[l], need_ctx)
        x_l = x_l + g1 * (jnp.concatenate([ya_l, yb_l, yc_l], axis=-1) @ w_out[l])
        h2 = rmsnorm(x_l, norm2_g[l]) * (1 + sc2) + sh2
        moe_args = (w_router[l], b_router[l], w_gu[l], b_gu[l], w_down[l], b_down[l])
        if need_ctx:
            x_c = x_c + cg1 * (jnp.concatenate([ya_c, yb_c, yc_c], axis=-1) @ w_out[l])
            hc2 = rmsnorm(x_c, norm2_g[l]) * (1 + csc2) + csh2
            f = moe(jnp.concatenate([h2.reshape(-1, D), hc2.reshape(-1, D)], axis=0), *moe_args)
            f_l, f_c = f[:B * S], f[B * S:]
            x_c = x_c + cg2 * f_c.reshape(x_c.shape)
        else:
            f_l = moe(h2.reshape(-1, D), *moe_args)
        x_l = x_l + g2 * f_l.reshape(x_l.shape)
    return rmsnorm(x_l, final_g)
```

```python
import functools
import math

import jax
import jax.numpy as jnp
import numpy as np
from jax import lax
from jax.experimental import pallas as pl
from jax.experimental.pallas import tpu as pltpu

F32 = jnp.float32
BF16 = jnp.bfloat16
I32 = jnp.int32
HIGHEST = lax.Precision.HIGHEST

LANES = 128
SUBLANES = 8
VMEM_LIMIT_BYTES = 56 * 1024 * 1024

GRID_W = 64
EPS = 1e-6
LRU_WIDTH = 256
LRU_C = 8.0
CONV_W = 4
MLA_HEADS = 8
MLA_NOPE = 64
MLA_ROPE = 32
MLA_V = 64
MLA_Q_RANK = 256
MLA_KV_RANK = 128
ROPE_AXIS_FREQ = MLA_ROPE // 4
ROPE_BASE = 10000.0
ML_HEADS = 4
ML_HEAD_DIM = 64
ML_WIDTH = ML_HEADS * ML_HEAD_DIM
ML_PAD = ML_HEADS * LANES
N_EXPERTS = 32
TOP_K = 4
D_EXPERT = 1024
SWIGLU_LIMIT = 7.0
SWIGLU_ALPHA = 1.702
MOE_BLOCK = 256

TILE = 256
HALO = SUBLANES
UB_W = MLA_Q_RANK + MLA_KV_RANK + 2 * LANES
COMB_TILE = 128

NT_DIMS = (((1,), (1,)), ((), ()))


def _params(*sem):
    return pltpu.CompilerParams(dimension_semantics=sem, vmem_limit_bytes=VMEM_LIMIT_BYTES)


def _sigmoid(x):
    return 1.0 / (1.0 + jnp.exp(-x))


def _log_sigmoid(x):
    return jnp.minimum(x, 0.0) - jnp.log1p(jnp.exp(-jnp.abs(x)))


def _rms(x, g):
    return x * lax.rsqrt(jnp.mean(x * x, axis=-1, keepdims=True) + EPS) * g


def _mod_kernel(cv_ref, w_ref, b_ref, o_ref):
    cv = cv_ref[...]
    a = cv * _sigmoid(cv)
    o_ref[0, 0] = jnp.dot(a, w_ref[0], precision=HIGHEST, preferred_element_type=F32) + b_ref[0, 0]


def _modulation(cv, w_mod, b_mod):
    depth, d, _ = w_mod.shape
    rows = cv.shape[0]
    return pl.pallas_call(
        _mod_kernel,
        grid=(depth, 6),
        in_specs=[
            pl.BlockSpec((rows, d), lambda l, j: (0, 0)),
            pl.BlockSpec((1, d, d), lambda l, j: (l, 0, j)),
            pl.BlockSpec((1, 1, 1, d), lambda l, j: (l, j, 0, 0)),
        ],
        out_specs=pl.BlockSpec((1, 1, rows, d), lambda l, j: (l, j, 0, 0)),
        out_shape=jax.ShapeDtypeStruct((depth, 6, rows, d), F32),
        compiler_params=_params("arbitrary", "arbitrary"),
        name="modulation",
    )(cv, w_mod, b_mod.reshape(depth, 6, 1, d))


def _mod_row(i, n_lat, lat_per_batch, ctx_row):
    return jnp.where(i < n_lat, i // lat_per_batch, ctx_row)


def _in_kernel(x_ref, mod_ref, g_ref, w_ref, ua_ref, ub_ref, um_ref, *, n_lat, lat_per_batch, ctx_row):
    r = _mod_row(pl.program_id(0), n_lat, lat_per_batch, ctx_row)
    sh = mod_ref[0, pl.ds(r, 1), :]
    sc = mod_ref[1, pl.ds(r, 1), :]
    h = _rms(x_ref[...], g_ref[...]) * (1.0 + sc) + sh
    u = jnp.dot(h.astype(BF16), w_ref[...], preferred_element_type=F32)
    wa = ua_ref.shape[1]
    wb = ub_ref.shape[1]
    ua_ref[...] = u[:, :wa]
    ub_ref[...] = u[:, wa:wa + wb]
    um_ref[...] = u[:, wa + wb:]


def _in_proj(x_all, mod_l, g, w_in_p, geo):
    t_all, d = x_all.shape
    n_tiles = t_all // TILE
    wa, wb, wm = 2 * LRU_WIDTH, UB_W, 2 * ML_PAD
    kern = functools.partial(_in_kernel, n_lat=geo["n_lat"], lat_per_batch=geo["lpb"], ctx_row=geo["ctx_row"])
    return pl.pallas_call(
        kern,
        grid=(n_tiles,),
        in_specs=[
            pl.BlockSpec((TILE, d), lambda i: (i, 0)),
            pl.BlockSpec(mod_l.shape, lambda i: (0, 0, 0)),
            pl.BlockSpec((1, d), lambda i: (0, 0)),
            pl.BlockSpec(w_in_p.shape, lambda i: (0, 0)),
        ],
        out_specs=[
            pl.BlockSpec((TILE, wa), lambda i: (i, 0)),
            pl.BlockSpec((TILE, wb), lambda i: (i, 0)),
            pl.BlockSpec((TILE, wm), lambda i: (i, 0)),
        ],
        out_shape=[
            jax.ShapeDtypeStruct((t_all, wa), F32),
            jax.ShapeDtypeStruct((t_all, wb), F32),
            jax.ShapeDtypeStruct((t_all, wm), F32),
        ],
        compiler_params=_params("arbitrary"),
        name="in_proj",
    )(x_all, mod_l, g.reshape(1, d), w_in_p)


def _chunk_block(b, j, geo, rev):
    lat = (geo["lpb"] - j) if rev else (j - 1)
    return jnp.where(j == 0, geo["n_lat"] + b, b * geo["lpb"] + lat)


def _chunk_specs(width, col, geo, rev):
    per = TILE // HALO
    last = geo["t_all"] // HALO - 1

    def cur(b, j):
        return (_chunk_block(b, j, geo, rev), col)

    def prev(b, j):
        return (jnp.maximum(_chunk_block(b, j, geo, rev) * per - 1, 0), col)

    def nxt(b, j):
        return (jnp.minimum((_chunk_block(b, j, geo, rev) + 1) * per, last), col)

    return [pl.BlockSpec((TILE, width), cur), pl.BlockSpec((HALO, width), prev), pl.BlockSpec((HALO, width), nxt)]


def _short_conv(x, xp_ref, xn_ref, w_ref, b_ref, j, lpb, rev):
    n = x.shape[0]
    lat = (lpb - j) if rev else (j - 1)
    is_lat = j > 0
    prev_ok = jnp.logical_and(is_lat, lat > 0)
    next_ok = jnp.logical_and(is_lat, lat < lpb - 1)
    xp = xp_ref[...] * prev_ok.astype(F32)
    xn = xn_ref[...] * next_ok.astype(F32)
    row = lax.broadcasted_iota(I32, x.shape, 0)
    x_m1 = jnp.where(row == 0, xp[HALO - 1:HALO, :], pltpu.roll(x, 1, 0))
    x_m2 = jnp.where(row == 0, xp[HALO - 2:HALO - 1, :], jnp.where(row == 1, xp[HALO - 1:HALO, :], pltpu.roll(x, 2, 0)))
    x_p1 = jnp.where(row == n - 1, xn[0:1, :], pltpu.roll(x, n - 1, 0))
    return x_m2 * w_ref[0:1, :] + x_m1 * w_ref[1:2, :] + x * w_ref[2:3, :] + x_p1 * w_ref[3:4, :] + b_ref[...]


def _lin_scan(a, b, rev):
    n = a.shape[0]
    row = lax.broadcasted_iota(I32, a.shape, 0)
    d = 1
    while d < n:
        if rev:
            a_s, b_s, valid = pltpu.roll(a, n - d, 0), pltpu.roll(b, n - d, 0), row < n - d
        else:
            a_s, b_s, valid = pltpu.roll(a, d, 0), pltpu.roll(b, d, 0), row >= d
        a_s = jnp.where(valid, a_s, 1.0)
        b_s = jnp.where(valid, b_s, 0.0)
        b = a * b_s + b
        a = a * a_s
        d *= 2
    return a, b


def _lru_kernel(*refs, rev, lpb):
    if rev:
        x_ref, xp_ref, xn_ref, g_ref, hf_ref, cw_ref, cb_ref, wg_ref, bg_ref, lam_ref, o_ref, h_scr = refs
    else:
        x_ref, xp_ref, xn_ref, cw_ref, cb_ref, wg_ref, bg_ref, lam_ref, o_ref, h_scr = refs
    j = pl.program_id(1)

    @pl.when(j == 0)
    def _():
        h_scr[...] = jnp.zeros_like(h_scr)

    x = x_ref[...]
    n, w = x.shape
    xc = _short_conv(x, xp_ref, xn_ref, cw_ref, cb_ref, j, lpb, rev)
    gates = jnp.dot(xc.astype(BF16), wg_ref[...], preferred_element_type=F32) + bg_ref[...]
    r = _sigmoid(gates[:, :w])
    ig = _sigmoid(gates[:, w:])
    log_a = LRU_C * r * _log_sigmoid(lam_ref[...])
    a = jnp.exp(log_a)
    bb = jnp.sqrt(-jnp.tanh(log_a) * (1.0 + a * a)) * ig * xc
    a_cum, h_loc = _lin_scan(a, bb, rev)
    h = a_cum * h_scr[...] + h_loc
    h_scr[...] = h[0:1, :] if rev else h[n - 1:n, :]
    if rev:
        o_ref[...] = ((hf_ref[...] + h) * jax.nn.gelu(g_ref[...], approximate=True)).astype(o_ref.dtype)
    else:
        o_ref[...] = h


def _lru_dir(u_a, hf, cw, cb, wg, bg, lam, geo, rev):
    w = LRU_WIDTH
    bsz, lpb = geo["batch"], geo["lpb"]
    specs = _chunk_specs(w, 0, geo, rev)
    args = [u_a, u_a, u_a]
    cur = specs[0].index_map
    if rev:
        specs += [pl.BlockSpec((TILE, w), lambda b, j: (_chunk_block(b, j, geo, rev), 1)), pl.BlockSpec((TILE, w), cur)]
        args += [u_a, hf]
    const = lambda b, j: (0, 0)
    specs += [pl.BlockSpec(cw.shape, const), pl.BlockSpec((1, w), const), pl.BlockSpec(wg.shape, const),
              pl.BlockSpec((1, 2 * w), const), pl.BlockSpec((1, w), const)]
    args += [cw, cb.reshape(1, w), wg, bg.reshape(1, 2 * w), lam.reshape(1, w)]
    return pl.pallas_call(
        functools.partial(_lru_kernel, rev=rev, lpb=lpb),
        grid=(bsz, lpb + 1),
        in_specs=specs,
        out_specs=pl.BlockSpec((TILE, w), cur),
        out_shape=jax.ShapeDtypeStruct((geo["t_all"], w), BF16 if rev else F32),
        scratch_shapes=[pltpu.VMEM((1, w), F32)],
        compiler_params=_params("arbitrary", "arbitrary"),
        name="rglru_bwd" if rev else "rglru_fwd",
    )(*args)


def _mlstm_kernel(*refs, rev, lpb):
    if rev:
        (x_ref, xp_ref, xn_ref, z_ref, hf_ref, cw_ref, cb_ref, wq_ref, wk_ref, wkt_ref, wv_ref, wg_ref, wgt_ref,
         bg_ref, bgt_ref, ng_ref, sk_ref, o_ref, c_scr, m_scr) = refs
    else:
        (x_ref, xp_ref, xn_ref, cw_ref, cb_ref, wq_ref, wk_ref, wkt_ref, wv_ref, wg_ref, wgt_ref,
         bg_ref, bgt_ref, o_ref, c_scr, m_scr) = refs
    j = pl.program_id(1)

    @pl.when(j == 0)
    def _():
        c_scr[...] = jnp.zeros_like(c_scr)
        m_scr[...] = jnp.zeros_like(m_scr)

    x = x_ref[...]
    n = x.shape[0]
    xc = _short_conv(x, xp_ref, xn_ref, cw_ref, cb_ref, j, lpb, rev)
    xc = xc * _sigmoid(xc)
    xcb = xc.astype(BF16)
    q = jnp.dot(xcb, wq_ref[...], preferred_element_type=F32)
    k = jnp.dot(xcb, wk_ref[...], preferred_element_type=F32)
    kt = lax.dot_general(wkt_ref[...], xcb, NT_DIMS, preferred_element_type=F32)
    lane_w = lax.broadcasted_iota(I32, (1, ML_PAD), 1)
    ones_lane = (lane_w % LANES == ML_HEAD_DIM).astype(F32)
    v = jnp.dot(x.astype(BF16), wv_ref[...], preferred_element_type=F32) + ones_lane
    qkv = jnp.concatenate([q, k, v], axis=1).astype(BF16)
    g_col = jnp.dot(qkv, wg_ref[...], preferred_element_type=F32) + bg_ref[...]
    g_row = lax.dot_general(wgt_ref[...], qkv, NT_DIMS, preferred_element_type=F32) + bgt_ref[...]
    ti = lax.broadcasted_iota(I32, (n, n), 0)
    si = lax.broadcasted_iota(I32, (n, n), 1)
    mask = (si >= ti) if rev else (si <= ti)
    tri = mask.astype(F32)
    tri_t = ((ti >= si) if rev else (ti <= si)).astype(F32)
    b_col = jnp.dot(tri, _log_sigmoid(g_col), precision=HIGHEST, preferred_element_type=F32)
    b_row = jnp.dot(_log_sigmoid(g_row), tri_t, precision=HIGHEST, preferred_element_type=F32)
    lane = lax.broadcasted_iota(I32, (1, LANES), 1)
    num_mask = (lane < ML_HEAD_DIM).astype(F32)
    den_mask = (lane == ML_HEAD_DIM).astype(F32)
    last = 0 if rev else n - 1
    outs = []
    for h in range(ML_HEADS):
        hs = slice(h * LANES, (h + 1) * LANES)
        bc = b_col[:, ML_HEADS + h:ML_HEADS + h + 1]
        br = b_row[ML_HEADS + h:ML_HEADS + h + 1, :]
        ic = g_col[:, h:h + 1]
        ir = g_row[h:h + 1, :]
        m_prev = m_scr[h:h + 1, 0:1]
        dmat = jnp.where(mask, bc - br + ir, -jnp.inf)
        inter = bc + m_prev
        m_t = jnp.maximum(inter, jnp.max(dmat, axis=1, keepdims=True))
        s = jnp.dot(q[:, hs].astype(BF16), kt[hs, :].astype(BF16), preferred_element_type=F32)
        p = s * jnp.exp(dmat - m_t)
        w_inter = jnp.exp(inter - m_t)
        vh = v[:, hs]
        c_old = c_scr[h]
        numden = (jnp.dot(p.astype(BF16), vh.astype(BF16), preferred_element_type=F32)
                  + w_inter * jnp.dot(q[:, hs].astype(BF16), c_old.astype(BF16), preferred_element_type=F32))
        den = jnp.sum(numden * den_mask, axis=1, keepdims=True)
        hh = numden * num_mask / jnp.maximum(jnp.abs(den), jnp.exp(-m_t))
        b_last = bc[last:last + 1, :]
        ws_col = b_last - bc + ic
        m_new = jnp.maximum(b_last + m_prev, jnp.max(ws_col, axis=0, keepdims=True))
        decay = jnp.exp(b_last + m_prev - m_new)
        wv = (jnp.exp(ws_col - m_new) * vh).astype(BF16)
        c_scr[h] = decay * c_old + jnp.dot(kt[hs, :].astype(BF16), wv, preferred_element_type=F32)
        m_scr[h:h + 1, :] = jnp.broadcast_to(m_new, (1, LANES))
        if rev:
            hsum = hf_ref[:, hs] + hh
            mu = jnp.sum(hsum, axis=1, keepdims=True) * (1.0 / ML_HEAD_DIM)
            cen = (hsum - mu) * num_mask
            var = jnp.sum(cen * cen, axis=1, keepdims=True) * (1.0 / ML_HEAD_DIM)
            hh = cen * lax.rsqrt(var + EPS)
        outs.append(hh)
    hcat = jnp.concatenate(outs, axis=1)
    if rev:
        z = z_ref[...]
        o_ref[...] = ((hcat * ng_ref[...] + sk_ref[...] * xc) * (z * _sigmoid(z))).astype(o_ref.dtype)
    else:
        o_ref[...] = hcat


def _mlstm_dir(u_m, hf, wts, geo, rev):
    w = ML_PAD
    bsz, lpb = geo["batch"], geo["lpb"]
    specs = _chunk_specs(w, 0, geo, rev)
    args = [u_m, u_m, u_m]
    cur = specs[0].index_map
    if rev:
        specs += [pl.BlockSpec((TILE, w), lambda b, j: (_chunk_block(b, j, geo, rev), 1)), pl.BlockSpec((TILE, w), cur)]
        args += [u_m, hf]
    const = lambda b, j: (0, 0)
    names = ["cw", "cb", "wq", "wk", "wkt", "wv", "wg", "wgt", "bg", "bgt"] + (["ng", "sk"] if rev else [])
    for nm in names:
        specs.append(pl.BlockSpec(wts[nm].shape, const))
        args.append(wts[nm])
    return pl.pallas_call(
        functools.partial(_mlstm_kernel, rev=rev, lpb=lpb),
        grid=(bsz, lpb + 1),
        in_specs=specs,
        out_specs=pl.BlockSpec((TILE, w), cur),
        out_shape=jax.ShapeDtypeStruct((geo["t_all"], w), BF16 if rev else F32),
        scratch_shapes=[pltpu.VMEM((ML_HEADS, LANES, LANES), F32), pltpu.VMEM((SUBLANES, LANES), F32)],
        compiler_params=_params("arbitrary", "arbitrary"),
        name="mlstm_bwd" if rev else "mlstm_fwd",
    )(*args)


def _mla_proj_kernel(ub_ref, cos_ref, sin_ref, gq_ref, w1_ref, w2_ref, gkv_ref, wk_ref, wv_ref, q_ref, k_ref, v_ref):
    ub = ub_ref[...]
    cos = cos_ref[...]
    sin = sin_ref[...]
    qn = _rms(ub[:, :MLA_Q_RANK], gq_ref[...]).astype(BF16)
    qa = jnp.dot(qn, w1_ref[...], preferred_element_type=F32)
    qb = jnp.dot(qn, w2_ref[...], preferred_element_type=F32)
    kvn = _rms(ub[:, MLA_Q_RANK:MLA_Q_RANK + MLA_KV_RANK], gkv_ref[...]).astype(BF16)
    kn = jnp.dot(kvn, wk_ref[...], preferred_element_type=F32)
    lane = lax.broadcasted_iota(I32, (1, LANES), 1)
    ones_lane = (lane == MLA_V).astype(F32)
    vn = jnp.dot(kvn, wv_ref[...], preferred_element_type=F32)
    off = MLA_Q_RANK + MLA_KV_RANK
    kr = ub[:, off:off + LANES] * cos + ub[:, off + LANES:off + 2 * LANES] * sin
    for h in range(MLA_HEADS):
        hs = slice(h * LANES, (h + 1) * LANES)
        q_ref[0, h] = (qa[:, hs] * cos + qb[:, hs] * sin).astype(BF16)
        k_ref[0, h] = (kn[:, hs] + kr).astype(BF16)
        v_ref[0, h] = (vn[:, hs] + ones_lane).astype(BF16)


def _mla_proj(u_b, cos_t, sin_t, wts, geo):
    n_tiles = geo["t_all"] // TILE
    n_lat, lpb, bsz = geo["n_lat"], geo["lpb"], geo["batch"]
    hw = MLA_HEADS * LANES

    def batch_of(i):
        return jnp.where(i < n_lat, i // lpb, i - n_lat)

    def blk_of(i):
        return jnp.where(i < n_lat, i % lpb, lpb)

    const = lambda i: (0, 0)
    head_spec = pl.BlockSpec((1, MLA_HEADS, TILE, LANES), lambda i: (batch_of(i), 0, blk_of(i), 0))
    head_shape = jax.ShapeDtypeStruct((bsz, MLA_HEADS, (lpb + 1) * TILE, LANES), BF16)
    return pl.pallas_call(
        _mla_proj_kernel,
        grid=(n_tiles,),
        in_specs=[
            pl.BlockSpec((TILE, UB_W), lambda i: (i, 0)),
            pl.BlockSpec((TILE, LANES), lambda i: (blk_of(i), 0)),
            pl.BlockSpec((TILE, LANES), lambda i: (blk_of(i), 0)),
            pl.BlockSpec((1, MLA_Q_RANK), const),
            pl.BlockSpec((MLA_Q_RANK, hw), const),
            pl.BlockSpec((MLA_Q_RANK, hw), const),
            pl.BlockSpec((1, MLA_KV_RANK), const),
            pl.BlockSpec((MLA_KV_RANK, hw), const),
            pl.BlockSpec((MLA_KV_RANK, hw), const),
        ],
        out_specs=[head_spec, head_spec, head_spec],
        out_shape=[head_shape, head_shape, head_shape],
        compiler_params=_params("arbitrary"),
        name="mla_proj",
    )(u_b, cos_t, sin_t, wts["gq"], wts["w1"], wts["w2"], wts["gkv"], wts["wk"], wts["wv"])


def _flash_kernel(q_ref, k_ref, v_ref, o_ref, m_scr, acc_scr, *, k_start, nk, tk):
    lane = lax.broadcasted_iota(I32, (1, LANES), 1)
    den_mask = (lane == MLA_V).astype(F32)
    outs = []
    for j in range(2):
        q = q_ref[0, j]
        m_scr[...] = jnp.full_like(m_scr, -jnp.inf)
        acc_scr[...] = jnp.zeros_like(acc_scr)

        def body(i, carry, j=j, q=q):
            start = pl.multiple_of(k_start + i * tk, LANES)
            kk = k_ref[0, j, pl.ds(start, tk), :]
            vv = v_ref[0, j, pl.ds(start, tk), :]
            s = lax.dot_general(q, kk, NT_DIMS, preferred_element_type=F32)
            m_old = m_scr[...]
            m_new = jnp.maximum(m_old, jnp.max(s, axis=1, keepdims=True))
            p = jnp.exp(s - m_new)
            acc_scr[...] = jnp.exp(m_old - m_new) * acc_scr[...] + jnp.dot(p.astype(BF16), vv, preferred_element_type=F32)
            m_scr[...] = m_new
            return carry

        lax.fori_loop(0, nk, body, 0)
        acc = acc_scr[...]
        den = jnp.sum(acc * den_mask, axis=1, keepdims=True)
        outs.append(acc / den)
    o_ref[...] = jnp.where(lane < MLA_V, outs[0], pltpu.roll(outs[1], MLA_V, 1)).astype(o_ref.dtype)


def _key_tile(n):
    for cand in range(min(n, 1024) // LANES * LANES, 0, -LANES):
        if n % cand == 0:
            return cand
    raise ValueError(n)


def _flash(q, k, v, tq, q_blk0, nq, k_start, k_len, out_blk0, t_all):
    bsz, heads, rows, _ = q.shape
    tk = _key_tile(k_len)
    kern = functools.partial(_flash_kernel, k_start=k_start, nk=k_len // tk, tk=tk)
    return pl.pallas_call(
        kern,
        grid=(bsz, heads // 2, nq),
        in_specs=[
            pl.BlockSpec((1, 2, tq, LANES), lambda b, h, i: (b, h, q_blk0 + i, 0)),
            pl.BlockSpec((1, 2, rows, LANES), lambda b, h, i: (b, h, 0, 0)),
            pl.BlockSpec((1, 2, rows, LANES), lambda b, h, i: (b, h, 0, 0)),
        ],
        out_specs=pl.BlockSpec((tq, LANES), lambda b, h, i: (out_blk0 + b * nq + i, h)),
        out_shape=jax.ShapeDtypeStruct((t_all, heads * MLA_V), BF16),
        scratch_shapes=[pltpu.VMEM((tq, 1), F32), pltpu.VMEM((tq, LANES), F32)],
        compiler_params=_params("arbitrary", "arbitrary", "arbitrary"),
        name="mla_attention",
    )(q, k, v)


def _out_kernel(ya_ref, yb_ref, yc_ref, x_ref, mod_ref, w_ref, g_ref, wr_ref, br_ref,
                xo_ref, h2_ref, te_ref, tg_ref, *, n_lat, lat_per_batch, ctx_row):
    r = _mod_row(pl.program_id(0), n_lat, lat_per_batch, ctx_row)
    g1 = mod_ref[2, pl.ds(r, 1), :]
    sh2 = mod_ref[3, pl.ds(r, 1), :]
    sc2 = mod_ref[4, pl.ds(r, 1), :]
    y = jnp.concatenate([ya_ref[...], yb_ref[...], yc_ref[...]], axis=1)
    x1 = x_ref[...] + g1 * jnp.dot(y, w_ref[...], preferred_element_type=F32)
    xo_ref[...] = x1
    h2 = _rms(x1, g_ref[...]) * (1.0 + sc2) + sh2
    h2_ref[...] = h2
    logits = jnp.dot(h2, wr_ref[...], precision=HIGHEST, preferred_element_type=F32) + br_ref[...]
    n, ne = logits.shape
    lane_e = lax.broadcasted_iota(I32, (n, ne), 1).astype(F32)
    lane_o = lax.broadcasted_iota(I32, (n, LANES), 1)
    vals, idxs = [], []
    for _ in range(TOP_K):
        m = jnp.max(logits, axis=1, keepdims=True)
        idx = jnp.min(jnp.where(logits == m, lane_e, float(ne)), axis=1, keepdims=True)
        logits = jnp.where(lane_e == idx, -jnp.inf, logits)
        vals.append(m)
        idxs.append(idx)
    exps = [jnp.exp(vv - vals[0]) for vv in vals]
    tot = exps[0] + exps[1] + exps[2] + exps[3]
    te = jnp.zeros((n, LANES), F32)
    tg = jnp.zeros((n, LANES), F32)
    for kk in range(TOP_K):
        te = jnp.where(lane_o == kk, idxs[kk], te)
        tg = jnp.where(lane_o == kk, exps[kk] / tot, tg)
    te_ref[...] = te.astype(I32)
    tg_ref[...] = tg


def _out_proj(ya, yb, yc, x_all, mod_l, w_out_p, g, w_router, b_router, geo, n_tiles):
    d = x_all.shape[1]
    rows = n_tiles * TILE
    kern = functools.partial(_out_kernel, n_lat=geo["n_lat"], lat_per_batch=geo["lpb"], ctx_row=geo["ctx_row"])
    row_blk = lambda i: (i, 0)
    const = lambda i: (0, 0)
    w_router = jnp.pad(w_router, ((0, 0), (0, LANES - N_EXPERTS)))
    b_router = jnp.pad(b_router, (0, LANES - N_EXPERTS), constant_values=-jnp.inf)
    return pl.pallas_call(
        kern,
        grid=(n_tiles,),
        in_specs=[
            pl.BlockSpec((TILE, ya.shape[1]), row_blk),
            pl.BlockSpec((TILE, yb.shape[1]), row_blk),
            pl.BlockSpec((TILE, yc.shape[1]), row_blk),
            pl.BlockSpec((TILE, d), row_blk),
            pl.BlockSpec(mod_l.shape, lambda i: (0, 0, 0)),
            pl.BlockSpec(w_out_p.shape, const),
            pl.BlockSpec((1, d), const),
            pl.BlockSpec(w_router.shape, const),
            pl.BlockSpec((1, LANES), const),
        ],
        out_specs=[pl.BlockSpec((TILE, d), row_blk), pl.BlockSpec((TILE, d), row_blk),
                   pl.BlockSpec((TILE, LANES), row_blk), pl.BlockSpec((TILE, LANES), row_blk)],
        out_shape=[jax.ShapeDtypeStruct((rows, d), F32), jax.ShapeDtypeStruct((rows, d), F32),
                   jax.ShapeDtypeStruct((rows, LANES), I32), jax.ShapeDtypeStruct((rows, LANES), F32)],
        compiler_params=_params("arbitrary"),
        name="out_proj_router",
    )(ya, yb, yc, x_all, mod_l, w_out_p, g.reshape(1, d), w_router, b_router.reshape(1, LANES))


def _row_copy(src_hbm, row, buf, slot, r, sem):
    return pltpu.make_async_copy(src_hbm.at[pl.ds(row, 1), :], buf.at[slot, pl.ds(r, 1), :], sem.at[slot])


def _moe_kernel(be_ref, nu_ref, tok_ref, tok_next_ref, h_hbm, gate_ref, wgu_ref, bgu_ref, wd_ref, bd_ref,
                y_ref, xbuf, sem, wgu_bf, wd_bf):
    i = pl.program_id(0)
    n_used = nu_ref[0]
    slot = i % 2
    rows = xbuf.shape[1]

    def issue(idx_ref, dst_slot):
        def body(r, carry):
            _row_copy(h_hbm, idx_ref[0, 0, r], xbuf, dst_slot, r, sem).start()
            return carry
        lax.fori_loop(0, rows, body, 0)

    @pl.when(i == 0)
    def _():
        issue(tok_ref, 0)

    @pl.when(i + 1 < n_used)
    def _():
        issue(tok_next_ref, 1 - slot)

    @pl.when(i < n_used)
    def _():
        def wait_body(r, carry):
            _row_copy(h_hbm, 0, xbuf, slot, r, sem).wait()
            return carry
        lax.fori_loop(0, rows, wait_body, 0)

        prev_e = be_ref[jnp.maximum(i - 1, 0)]

        @pl.when(jnp.logical_or(i == 0, be_ref[i] != prev_e))
        def _():
            wgu_bf[...] = wgu_ref[0].astype(BF16)
            wd_bf[...] = wd_ref[0].astype(BF16)

        x = xbuf[slot].astype(BF16)
        gu = jnp.dot(x, wgu_bf[...], preferred_element_type=F32) + bgu_ref[0]
        glu = jnp.minimum(gu[:, :D_EXPERT], SWIGLU_LIMIT)
        lin = jnp.clip(gu[:, D_EXPERT:], -SWIGLU_LIMIT, SWIGLU_LIMIT)
        act = glu * _sigmoid(SWIGLU_ALPHA * glu) * (lin + 1.0)
        y = jnp.dot(act.astype(BF16), wd_bf[...], preferred_element_type=F32) + bd_ref[0]
        y_ref[...] = y * gate_ref[...]

    @pl.when(i >= n_used)
    def _():
        y_ref[...] = jnp.zeros_like(y_ref)


def _moe_experts(h2, rows_tok, rows_gate, block_e, n_used, w_gu, b_gu, w_down, b_down):
    n_blocks = rows_tok.shape[0] // MOE_BLOCK
    d = h2.shape[1]
    tok3 = rows_tok.reshape(n_blocks, 1, MOE_BLOCK)
    grid_spec = pltpu.PrefetchScalarGridSpec(
        num_scalar_prefetch=2,
        grid=(n_blocks,),
        in_specs=[
            pl.BlockSpec((1, 1, MOE_BLOCK), lambda i, be, nu: (i, 0, 0), memory_space=pltpu.SMEM),
            pl.BlockSpec((1, 1, MOE_BLOCK), lambda i, be, nu: (jnp.minimum(i + 1, n_blocks - 1), 0, 0),
                         memory_space=pltpu.SMEM),
            pl.BlockSpec(memory_space=pl.ANY),
            pl.BlockSpec((MOE_BLOCK, 1), lambda i, be, nu: (i, 0)),
            pl.BlockSpec((1, d, 2 * D_EXPERT), lambda i, be, nu: (be[i], 0, 0)),
            pl.BlockSpec((1, 1, 2 * D_EXPERT), lambda i, be, nu: (be[i], 0, 0)),
            pl.BlockSpec((1, D_EXPERT, d), lambda i, be, nu: (be[i], 0, 0)),
            pl.BlockSpec((1, 1, d), lambda i, be, nu: (be[i], 0, 0)),
        ],
        out_specs=pl.BlockSpec((MOE_BLOCK, d), lambda i, be, nu: (i, 0)),
        scratch_shapes=[
            pltpu.VMEM((2, MOE_BLOCK, d), F32),
            pltpu.SemaphoreType.DMA((2,)),
            pltpu.VMEM((d, 2 * D_EXPERT), BF16),
            pltpu.VMEM((D_EXPERT, d), BF16),
        ],
    )
    return pl.pallas_call(
        _moe_kernel,
        grid_spec=grid_spec,
        out_shape=jax.ShapeDtypeStruct((n_blocks * MOE_BLOCK, d), F32),
        compiler_params=_params("arbitrary"),
        name="moe_experts",
    )(block_e, n_used, tok3, tok3, h2, rows_gate.reshape(-1, 1), w_gu, b_gu.reshape(N_EXPERTS, 1, -1),
      w_down, b_down.reshape(N_EXPERTS, 1, -1))


def _comb_kernel(inv_ref, inv_next_ref, y_hbm, x_ref, mod_ref, fg_ref, o_ref, cbuf, sem,
                 *, n_lat, lat_per_batch, ctx_row, final):
    i = pl.program_id(0)
    n_steps = pl.num_programs(0)
    slot = i % 2
    rows = cbuf.shape[1]
    per_tile = TILE // COMB_TILE

    def issue(idx_ref, dst_slot):
        def body(r, carry):
            _row_copy(y_hbm, idx_ref[0, 0, r], cbuf, dst_slot, r, sem).start()
            return carry
        lax.fori_loop(0, rows, body, 0)

    @pl.when(i == 0)
    def _():
        issue(inv_ref, 0)

    @pl.when(i + 1 < n_steps)
    def _():
        issue(inv_next_ref, 1 - slot)

    def wait_body(r, carry):
        _row_copy(y_hbm, 0, cbuf, slot, r, sem).wait()
        return carry
    lax.fori_loop(0, rows, wait_body, 0)

    r = _mod_row(i // per_tile, n_lat, lat_per_batch, ctx_row)
    g2 = mod_ref[5, pl.ds(r, 1), :]
    f = cbuf[slot, 0:COMB_TILE, :]
    for kk in range(1, TOP_K):
        f = f + cbuf[slot, kk * COMB_TILE:(kk + 1) * COMB_TILE, :]
    x2 = x_ref[...] + g2 * f
    o_ref[...] = _rms(x2, fg_ref[...]) if final else x2


def _moe_combine(y_sorted, inv, x_all, mod_l, final_g, geo, final):
    t_moe = inv.shape[0]
    d = x_all.shape[1]
    n_steps = t_moe // COMB_TILE
    inv3 = inv.reshape(n_steps, COMB_TILE, TOP_K).transpose(0, 2, 1).reshape(n_steps, 1, TOP_K * COMB_TILE)
    kern = functools.partial(_comb_kernel, n_lat=geo["n_lat"], lat_per_batch=geo["lpb"], ctx_row=geo["ctx_row"],
                             final=final)
    return pl.pallas_call(
        kern,
        grid=(n_steps,),
        in_specs=[
            pl.BlockSpec((1, 1, TOP_K * COMB_TILE), lambda i: (i, 0, 0), memory_space=pltpu.SMEM),
            pl.BlockSpec((1, 1, TOP_K * COMB_TILE), lambda i: (jnp.minimum(i + 1, n_steps - 1), 0, 0),
                         memory_space=pltpu.SMEM),
            pl.BlockSpec(memory_space=pl.ANY),
            pl.BlockSpec((COMB_TILE, d), lambda i: (i, 0)),
            pl.BlockSpec(mod_l.shape, lambda i: (0, 0, 0)),
            pl.BlockSpec((1, d), lambda i: (0, 0)),
        ],
        out_specs=pl.BlockSpec((COMB_TILE, d), lambda i: (i, 0)),
        out_shape=jax.ShapeDtypeStruct((t_moe, d), F32),
        scratch_shapes=[pltpu.VMEM((2, TOP_K * COMB_TILE, d), F32), pltpu.SemaphoreType.DMA((2,))],
        compiler_params=_params("arbitrary"),
        name="moe_combine",
    )(inv3, inv3, y_sorted, x_all, mod_l, final_g.reshape(1, d))


def _route(te, tg, t_moe):
    tk = t_moe * TOP_K
    e_flat = te[:t_moe, :TOP_K].reshape(tk)
    g_flat = tg[:t_moe, :TOP_K].reshape(tk)
    onehot = (e_flat[:, None] == jnp.arange(N_EXPERTS, dtype=I32)[None, :]).astype(I32)
    csum = jnp.cumsum(onehot, axis=0)
    rank = jnp.sum(csum * onehot, axis=1) - 1
    counts = csum[-1]
    padded = (counts + MOE_BLOCK - 1) // MOE_BLOCK * MOE_BLOCK
    padded_end = jnp.cumsum(padded)
    padded_start = padded_end - padded
    dest = jnp.sum(onehot * padded_start[None, :], axis=1) + rank
    n_blocks = -(-tk // MOE_BLOCK) + N_EXPERTS
    p = n_blocks * MOE_BLOCK
    rows_tok = jnp.zeros((p,), I32).at[dest].set(jnp.arange(tk, dtype=I32) // TOP_K)
    rows_gate = jnp.zeros((p,), F32).at[dest].set(g_flat)
    block_e = jnp.minimum(
        jnp.searchsorted(padded_end, jnp.arange(n_blocks, dtype=I32) * MOE_BLOCK, side="right"), N_EXPERTS - 1
    ).astype(I32)
    n_used = (padded_end[-1:] // MOE_BLOCK).astype(I32)
    return rows_tok, rows_gate, block_e, n_used, dest.reshape(t_moe, TOP_K).astype(I32)


def _block_diag_dense(w):
    g, i, j = w.shape
    out = jnp.zeros((g * i, g * j), w.dtype)
    for n in range(g):
        out = out.at[n * i:(n + 1) * i, n * j:(n + 1) * j].set(w[n])
    return out


def _pad_heads(w, heads, axis):
    shape = list(w.shape)
    shape[axis:axis + 1] = [heads, shape[axis] // heads]
    w = w.reshape(shape)
    pad = [(0, 0)] * w.ndim
    pad[axis + 1] = (0, LANES - shape[axis + 1])
    w = jnp.pad(w, pad)
    shape[axis:axis + 2] = [heads * LANES]
    return w.reshape(shape)


_ROPE_SWAP = np.concatenate([np.arange(8, 16), np.arange(0, 8), np.arange(24, 32), np.arange(16, 24)])


def _prep_in_weight(w_in):
    d = w_in.shape[0]
    o = np.cumsum([0, LRU_WIDTH, LRU_WIDTH, MLA_Q_RANK, MLA_KV_RANK, MLA_ROPE, ML_WIDTH, ML_WIDTH])
    a_xg = w_in[:, o[0]:o[2]]
    b_qkv = w_in[:, o[2]:o[4]]
    kr = w_in[:, o[4]:o[5]]
    z_nope = jnp.zeros((d, MLA_NOPE), w_in.dtype)
    z_tail = jnp.zeros((d, LANES - MLA_NOPE - MLA_ROPE), w_in.dtype)
    m_x = _pad_heads(w_in[:, o[5]:o[6]], ML_HEADS, 1)
    m_z = _pad_heads(w_in[:, o[6]:o[7]], ML_HEADS, 1)
    return jnp.concatenate([a_xg, b_qkv, z_nope, kr, z_tail, z_nope, kr[:, _ROPE_SWAP], z_tail, m_x, m_z],
                           axis=1).astype(BF16)


def _prep_out_weight(w_out):
    a = w_out[:LRU_WIDTH]
    b = w_out[LRU_WIDTH:LRU_WIDTH + MLA_HEADS * MLA_V]
    c = _pad_heads(w_out[LRU_WIDTH + MLA_HEADS * MLA_V:], ML_HEADS, 0)
    return jnp.concatenate([a, b, c], axis=0).astype(BF16)


def _prep_mla(q_norm_g, w_qb, kv_norm_g, w_kvb):
    scale = (MLA_NOPE + MLA_ROPE) ** -0.5
    rq = w_qb.shape[0]
    wq = w_qb.reshape(rq, MLA_HEADS, MLA_NOPE + MLA_ROPE) * scale
    nope, rope = wq[..., :MLA_NOPE], wq[..., MLA_NOPE:]
    z32 = jnp.zeros((rq, MLA_HEADS, LANES - MLA_NOPE - MLA_ROPE), w_qb.dtype)
    w1 = jnp.concatenate([nope, rope, z32], axis=-1).reshape(rq, MLA_HEADS * LANES)
    w2 = jnp.concatenate([jnp.zeros_like(nope), rope[..., _ROPE_SWAP], z32], axis=-1).reshape(rq, MLA_HEADS * LANES)
    rk = w_kvb.shape[0]
    wkv = w_kvb.reshape(rk, MLA_HEADS, MLA_NOPE + MLA_V)
    z64 = jnp.zeros((rk, MLA_HEADS, LANES - MLA_NOPE), w_kvb.dtype)
    wk = jnp.concatenate([wkv[..., :MLA_NOPE], z64], axis=-1).reshape(rk, MLA_HEADS * LANES)
    wv = jnp.concatenate([wkv[..., MLA_NOPE:], z64], axis=-1).reshape(rk, MLA_HEADS * LANES)
    return {"gq": q_norm_g.reshape(1, -1), "w1": w1.astype(BF16), "w2": w2.astype(BF16),
            "gkv": kv_norm_g.reshape(1, -1), "wk": wk.astype(BF16), "wv": wv.astype(BF16)}


def _prep_mlstm(conv_w, conv_b, wq, wk, wv, w_gate_d, b_gate_d, norm_g, skip):
    def proj(w):
        return _pad_heads(_pad_heads(_block_diag_dense(w), ML_HEADS, 0), ML_HEADS, 1)

    wk_p = proj(wk) * (ML_HEAD_DIM ** -0.5)
    wg = jnp.concatenate([_pad_heads(w_gate_d[i * ML_WIDTH:(i + 1) * ML_WIDTH], ML_HEADS, 0) for i in range(3)], axis=0)
    ng = w_gate_d.shape[1]
    wg = jnp.pad(wg, ((0, 0), (0, LANES - ng)))
    b_gate_d = jnp.pad(b_gate_d, (0, LANES - ng))
    gate_rows = 2 * SUBLANES
    return {
        "cw": _pad_heads(conv_w, ML_HEADS, 1), "cb": _pad_heads(conv_b.reshape(1, -1), ML_HEADS, 1),
        "wq": proj(wq).astype(BF16), "wk": wk_p.astype(BF16), "wkt": wk_p.T.astype(BF16), "wv": proj(wv).astype(BF16),
        "wg": wg.astype(BF16), "wgt": wg.T[:gate_rows].astype(BF16),
        "bg": b_gate_d.reshape(1, -1), "bgt": b_gate_d[:gate_rows].reshape(-1, 1),
        "ng": _pad_heads(norm_g.reshape(1, -1), ML_HEADS, 1), "sk": _pad_heads(skip.reshape(1, -1), ML_HEADS, 1),
    }


def _rope_tables(seq, ctx_len):
    rows = seq // GRID_W
    row = jnp.repeat(jnp.arange(rows, dtype=I32), GRID_W)
    col = jnp.tile(jnp.arange(GRID_W, dtype=I32), rows)
    freqs = ROPE_BASE ** (-jnp.arange(ROPE_AXIS_FREQ, dtype=F32) / ROPE_AXIS_FREQ)
    ang_r, ang_c = row[:, None] * freqs, col[:, None] * freqs
    cos32 = jnp.concatenate([jnp.cos(ang_r), jnp.cos(ang_r), jnp.cos(ang_c), jnp.cos(ang_c)], axis=1)
    sin32 = jnp.concatenate([-jnp.sin(ang_r), jnp.sin(ang_r), -jnp.sin(ang_c), jnp.sin(ang_c)], axis=1)
    cos32 = jnp.concatenate([cos32, jnp.ones((ctx_len, MLA_ROPE), F32)], axis=0)
    sin32 = jnp.concatenate([sin32, jnp.zeros((ctx_len, MLA_ROPE), F32)], axis=0)
    n = seq + ctx_len
    tail = jnp.zeros((n, LANES - MLA_NOPE - MLA_ROPE), F32)
    cos_t = jnp.concatenate([jnp.ones((n, MLA_NOPE), F32), cos32, tail], axis=1)
    sin_t = jnp.concatenate([jnp.zeros((n, MLA_NOPE), F32), sin32, tail], axis=1)
    return cos_t, sin_t


def kernel(x, c, ctx, c_ctx, norm1_g, norm2_g, w_mod, b_mod, w_in, w_out, lru_conv_w, lru_conv_b, lru_wa, lru_ba,
           lru_wx, lru_bx, lru_lambda, mla_q_norm_g, mla_w_qb, mla_kv_norm_g, mla_w_kvb, ml_conv_w, ml_conv_b,
           ml_wq, ml_wk, ml_wv, ml_w_gate, ml_b_gate, ml_norm_g, ml_skip, w_router, b_router, w_gu, b_gu, w_down,
           b_down, final_g):
    bsz, seq, d = x.shape
    ctx_len = ctx.shape[1]
    depth = w_mod.shape[0]
    assert ctx_len == TILE and seq % (2 * TILE) == 0 and bsz + 1 <= SUBLANES
    t_lat = bsz * seq
    t_all = t_lat + bsz * ctx_len
    geo = {"batch": bsz, "lpb": seq // TILE, "n_lat": t_lat // TILE, "ctx_row": bsz, "t_all": t_all}

    cv = jnp.zeros((SUBLANES, d), F32).at[:bsz].set(c).at[bsz].set(c_ctx)
    mod = _modulation(cv, w_mod, b_mod)
    cos_t, sin_t = _rope_tables(seq, ctx_len)
    x_all = jnp.concatenate([x.reshape(t_lat, d), ctx.reshape(bsz * ctx_len, d)], axis=0)

    out = None
    for l in range(depth):
        last = l == depth - 1
        u_a, u_b, u_m = _in_proj(x_all, mod[l], norm1_g[l], _prep_in_weight(w_in[l]), geo)

        hf = None
        for dd in range(2):
            wg = jnp.concatenate([_block_diag_dense(lru_wa[l, dd]), _block_diag_dense(lru_wx[l, dd])], axis=1)
            bg = jnp.concatenate([lru_ba[l, dd], lru_bx[l, dd]])
            hf = _lru_dir(u_a, hf, lru_conv_w[l], lru_conv_b[l], wg.astype(BF16), bg, lru_lambda[l, dd], geo, dd == 1)
        ya = hf

        q, k, v = _mla_proj(u_b, cos_t, sin_t, _prep_mla(mla_q_norm_g[l], mla_w_qb[l], mla_kv_norm_g[l], mla_w_kvb[l]), geo)
        tq = 2 * TILE
        yb = _flash(q, k, v, tq, 0, seq // tq, 0, seq + ctx_len, 0, t_lat)
        if not last:
            yb_c = _flash(q, k, v, TILE, seq // TILE, 1, seq, ctx_len, 0, bsz * ctx_len)
            yb = jnp.concatenate([yb, yb_c], axis=0)

        hf = None
        for dd in range(2):
            wts = _prep_mlstm(ml_conv_w[l], ml_conv_b[l], ml_wq[l], ml_wk[l], ml_wv[l], ml_w_gate[l, dd],
                              ml_b_gate[l, dd], ml_norm_g[l], ml_skip[l])
            hf = _mlstm_dir(u_m, hf, wts, geo, dd == 1)
        yc = hf

        n_tiles = (t_lat if last else t_all) // TILE
        x_mid, h2, te, tg = _out_proj(ya, yb, yc, x_all, mod[l], _prep_out_weight(w_out[l]), norm2_g[l],
                                      w_router[l], b_router[l], geo, n_tiles)
        t_moe = n_tiles * TILE
        rows_tok, rows_gate, block_e, n_used, inv = _route(te, tg, t_moe)
        y_sorted = _moe_experts(h2, rows_tok, rows_gate, block_e, n_used, w_gu[l], b_gu[l], w_down[l], b_down[l])
        x_all = _moe_combine(y_sorted, inv, x_mid, mod[l], final_g, geo, last)
        if last:
            out = x_all.reshape(bsz, seq, d)
    return out
```

```python
import functools
import math

import jax
import jax.numpy as jnp
import numpy as np
from jax import lax
from jax.experimental import pallas as pl
from jax.experimental.pallas import tpu as pltpu

F32 = jnp.float32
BF16 = jnp.bfloat16
I32 = jnp.int32
HIGHEST = lax.Precision.HIGHEST

LANES = 128
SUBLANES = 8
VMEM_LIMIT_BYTES = 56 * 1024 * 1024

GRID_W = 64
EPS = 1e-6
LRU_WIDTH = 256
LRU_C = 8.0
CONV_W = 4
MLA_HEADS = 8
MLA_NOPE = 64
MLA_ROPE = 32
MLA_V = 64
MLA_Q_RANK = 256
MLA_KV_RANK = 128
ROPE_AXIS_FREQ = MLA_ROPE // 4
ROPE_BASE = 10000.0
ML_HEADS = 4
ML_HEAD_DIM = 64
ML_WIDTH = ML_HEADS * ML_HEAD_DIM
ML_PAD = ML_HEADS * LANES
N_EXPERTS = 32
TOP_K = 4
D_EXPERT = 1024
SWIGLU_LIMIT = 7.0
SWIGLU_ALPHA = 1.702
MOE_BLOCK = 256

TILE = 256
HALO = SUBLANES
UB_W = MLA_Q_RANK + MLA_KV_RANK + 2 * LANES
COMB_TILE = 128

NT_DIMS = (((1,), (1,)), ((), ()))


def _params(*sem):
    return pltpu.CompilerParams(dimension_semantics=sem, vmem_limit_bytes=VMEM_LIMIT_BYTES)


def _sigmoid(x):
    return 1.0 / (1.0 + jnp.exp(-x))


def _log_sigmoid(x):
    return jnp.minimum(x, 0.0) - jnp.log1p(jnp.exp(-jnp.abs(x)))


def _rms(x, g):
    return x * lax.rsqrt(jnp.mean(x * x, axis=-1, keepdims=True) + EPS) * g


def _mod_kernel(cv_ref, w_ref, b_ref, o_ref):
    cv = cv_ref[...]
    a = cv * _sigmoid(cv)
    o_ref[0, 0] = jnp.dot(a, w_ref[0], precision=HIGHEST, preferred_element_type=F32) + b_ref[0, 0]


def _modulation(cv, w_mod, b_mod):
    depth, d, _ = w_mod.shape
    rows = cv.shape[0]
    return pl.pallas_call(
        _mod_kernel,
        grid=(depth, 6),
        in_specs=[
            pl.BlockSpec((rows, d), lambda l, j: (0, 0)),
            pl.BlockSpec((1, d, d), lambda l, j: (l, 0, j)),
            pl.BlockSpec((1, 1, 1, d), lambda l, j: (l, j, 0, 0)),
        ],
        out_specs=pl.BlockSpec((1, 1, rows, d), lambda l, j: (l, j, 0, 0)),
        out_shape=jax.ShapeDtypeStruct((depth, 6, rows, d), F32),
        compiler_params=_params("arbitrary", "arbitrary"),
        name="modulation",
    )(cv, w_mod, b_mod.reshape(depth, 6, 1, d))


def _mod_row(i, n_lat, lat_per_batch, ctx_row):
    return jnp.where(i < n_lat, i // lat_per_batch, ctx_row)


def _in_kernel(x_ref, mod_ref, g_ref, w_ref, ua_ref, ub_ref, um_ref, *, n_lat, lat_per_batch, ctx_row):
    r = _mod_row(pl.program_id(0), n_lat, lat_per_batch, ctx_row)
    sh = mod_ref[0, pl.ds(r, 1), :]
    sc = mod_ref[1, pl.ds(r, 1), :]
    h = _rms(x_ref[...], g_ref[...]) * (1.0 + sc) + sh
    u = jnp.dot(h.astype(BF16), w_ref[...], preferred_element_type=F32)
    wa = ua_ref.shape[1]
    wb = ub_ref.shape[1]
    ua_ref[...] = u[:, :wa]
    ub_ref[...] = u[:, wa:wa + wb]
    um_ref[...] = u[:, wa + wb:]


def _in_proj(x_all, mod_l, g, w_in_p, geo):
    t_all, d = x_all.shape
    n_tiles = t_all // TILE
    wa, wb, wm = 2 * LRU_WIDTH, UB_W, 2 * ML_PAD
    kern = functools.partial(_in_kernel, n_lat=geo["n_lat"], lat_per_batch=geo["lpb"], ctx_row=geo["ctx_row"])
    return pl.pallas_call(
        kern,
        grid=(n_tiles,),
        in_specs=[
            pl.BlockSpec((TILE, d), lambda i: (i, 0)),
            pl.BlockSpec(mod_l.shape, lambda i: (0, 0, 0)),
            pl.BlockSpec((1, d), lambda i: (0, 0)),
            pl.BlockSpec(w_in_p.shape, lambda i: (0, 0)),
        ],
        out_specs=[
            pl.BlockSpec((TILE, wa), lambda i: (i, 0)),
            pl.BlockSpec((TILE, wb), lambda i: (i, 0)),
            pl.BlockSpec((TILE, wm), lambda i: (i, 0)),
        ],
        out_shape=[
            jax.ShapeDtypeStruct((t_all, wa), F32),
            jax.ShapeDtypeStruct((t_all, wb), F32),
            jax.ShapeDtypeStruct((t_all, wm), F32),
        ],
        compiler_params=_params("arbitrary"),
        name="in_proj",
    )(x_all, mod_l, g.reshape(1, d), w_in_p)


def _chunk_block(b, j, geo, rev):
    lat = (geo["lpb"] - j) if rev else (j - 1)
    return jnp.where(j == 0, geo["n_lat"] + b, b * geo["lpb"] + lat)


def _chunk_specs(width, col, geo, rev):
    per = TILE // HALO
    last = geo["t_all"] // HALO - 1

    def cur(b, j):
        return (_chunk_block(b, j, geo, rev), col)

    def prev(b, j):
        return (jnp.maximum(_chunk_block(b, j, geo, rev) * per - 1, 0), col)

    def nxt(b, j):
        return (jnp.minimum((_chunk_block(b, j, geo, rev) + 1) * per, last), col)

    return [pl.BlockSpec((TILE, width), cur), pl.BlockSpec((HALO, width), prev), pl.BlockSpec((HALO, width), nxt)]


def _short_conv(x, xp_ref, xn_ref, w_ref, b_ref, j, lpb, rev):
    n = x.shape[0]
    lat = (lpb - j) if rev else (j - 1)
    is_lat = j > 0
    prev_ok = jnp.logical_and(is_lat, lat > 0)
    next_ok = jnp.logical_and(is_lat, lat < lpb - 1)
    xp = xp_ref[...] * prev_ok.astype(F32)
    xn = xn_ref[...] * next_ok.astype(F32)
    row = lax.broadcasted_iota(I32, x.shape, 0)
    x_m1 = jnp.where(row == 0, xp[HALO - 1:HALO, :], pltpu.roll(x, 1, 0))
    x_m2 = jnp.where(row == 0, xp[HALO - 2:HALO - 1, :], jnp.where(row == 1, xp[HALO - 1:HALO, :], pltpu.roll(x, 2, 0)))
    x_p1 = jnp.where(row == n - 1, xn[0:1, :], pltpu.roll(x, n - 1, 0))
    return x_m2 * w_ref[0:1, :] + x_m1 * w_ref[1:2, :] + x * w_ref[2:3, :] + x_p1 * w_ref[3:4, :] + b_ref[...]


def _lin_scan(a, b, rev):
    n = a.shape[0]
    row = lax.broadcasted_iota(I32, a.shape, 0)
    d = 1
    while d < n:
        if rev:
            a_s, b_s, valid = pltpu.roll(a, n - d, 0), pltpu.roll(b, n - d, 0), row < n - d
        else:
            a_s, b_s, valid = pltpu.roll(a, d, 0), pltpu.roll(b, d, 0), row >= d
        a_s = jnp.where(valid, a_s, 1.0)
        b_s = jnp.where(valid, b_s, 0.0)
        b = a * b_s + b
        a = a * a_s
        d *= 2
    return a, b


def _lru_kernel(*refs, rev, lpb):
    if rev:
        x_ref, xp_ref, xn_ref, g_ref, hf_ref, cw_ref, cb_ref, wg_ref, bg_ref, lam_ref, o_ref, h_scr = refs
    else:
        x_ref, xp_ref, xn_ref, cw_ref, cb_ref, wg_ref, bg_ref, lam_ref, o_ref, h_scr = refs
    j = pl.program_id(1)

    @pl.when(j == 0)
    def _():
        h_scr[...] = jnp.zeros_like(h_scr)

    x = x_ref[...]
    n, w = x.shape
    xc = _short_conv(x, xp_ref, xn_ref, cw_ref, cb_ref, j, lpb, rev)
    gates = jnp.dot(xc.astype(BF16), wg_ref[...], preferred_element_type=F32) + bg_ref[...]
    r = _sigmoid(gates[:, :w])
    ig = _sigmoid(gates[:, w:])
    log_a = LRU_C * r * _log_sigmoid(lam_ref[...])
    a = jnp.exp(log_a)
    bb = jnp.sqrt(-jnp.tanh(log_a) * (1.0 + a * a)) * ig * xc
    a_cum, h_loc = _lin_scan(a, bb, rev)
    h = a_cum * h_scr[...] + h_loc
    h_scr[...] = h[0:1, :] if rev else h[n - 1:n, :]
    if rev:
        o_ref[...] = ((hf_ref[...] + h) * jax.nn.gelu(g_ref[...], approximate=True)).astype(o_ref.dtype)
    else:
        o_ref[...] = h


def _lru_dir(u_a, hf, cw, cb, wg, bg, lam, geo, rev):
    w = LRU_WIDTH
    bsz, lpb = geo["batch"], geo["lpb"]
    specs = _chunk_specs(w, 0, geo, rev)
    args = [u_a, u_a, u_a]
    cur = specs[0].index_map
    if rev:
        specs += [pl.BlockSpec((TILE, w), lambda b, j: (_chunk_block(b, j, geo, rev), 1)), pl.BlockSpec((TILE, w), cur)]
        args += [u_a, hf]
    const = lambda b, j: (0, 0)
    specs += [pl.BlockSpec(cw.shape, const), pl.BlockSpec((1, w), const), pl.BlockSpec(wg.shape, const),
              pl.BlockSpec((1, 2 * w), const), pl.BlockSpec((1, w), const)]
    args += [cw, cb.reshape(1, w), wg, bg.reshape(1, 2 * w), lam.reshape(1, w)]
    return pl.pallas_call(
        functools.partial(_lru_kernel, rev=rev, lpb=lpb),
        grid=(bsz, lpb + 1),
        in_specs=specs,
        out_specs=pl.BlockSpec((TILE, w), cur),
        out_shape=jax.ShapeDtypeStruct((geo["t_all"], w), BF16 if rev else F32),
        scratch_shapes=[pltpu.VMEM((1, w), F32)],
        compiler_params=_params("arbitrary", "arbitrary"),
        name="rglru_bwd" if rev else "rglru_fwd",
    )(*args)


def _mlstm_kernel(*refs, rev, lpb):
    if rev:
        (x_ref, xp_ref, xn_ref, z_ref, hf_ref, cw_ref, cb_ref, wq_ref, wk_ref, wkt_ref, wv_ref, wg_ref, wgt_ref,
         bg_ref, bgt_ref, ng_ref, sk_ref, o_ref, c_scr, m_scr) = refs
    else:
        (x_ref, xp_ref, xn_ref, cw_ref, cb_ref, wq_ref, wk_ref, wkt_ref, wv_ref, wg_ref, wgt_ref,
         bg_ref, bgt_ref, o_ref, c_scr, m_scr) = refs
    j = pl.program_id(1)

    @pl.when(j == 0)
    def _():
        c_scr[...] = jnp.zeros_like(c_scr)
        m_scr[...] = jnp.zeros_like(m_scr)

    x = x_ref[...]
    n = x.shape[0]
    xc = _short_conv(x, xp_ref, xn_ref, cw_ref, cb_ref, j, lpb, rev)
    xc = xc * _sigmoid(xc)
    xcb = xc.astype(BF16)
    q = jnp.dot(xcb, wq_ref[...], preferred_element_type=F32)
    k = jnp.dot(xcb, wk_ref[...], preferred_element_type=F32)
    kt = lax.dot_general(wkt_ref[...], xcb, NT_DIMS, preferred_element_type=F32)
    lane_w = lax.broadcasted_iota(I32, (1, ML_PAD), 1)
    ones_lane = (lane_w % LANES == ML_HEAD_DIM).astype(F32)
    v = jnp.dot(x.astype(BF16), wv_ref[...], preferred_element_type=F32) + ones_lane
    qkv = jnp.concatenate([q, k, v], axis=1).astype(BF16)
    g_col = jnp.dot(qkv, wg_ref[...], preferred_element_type=F32) + bg_ref[...]
    g_row = lax.dot_general(wgt_ref[...], qkv, NT_DIMS, preferred_element_type=F32) + bgt_ref[...]
    ti = lax.broadcasted_iota(I32, (n, n), 0)
    si = lax.broadcasted_iota(I32, (n, n), 1)
    mask = (si >= ti) if rev else (si <= ti)
    tri = mask.astype(F32)
    tri_t = ((ti >= si) if rev else (ti <= si)).astype(F32)
    b_col = jnp.dot(tri, _log_sigmoid(g_col), precision=HIGHEST, preferred_element_type=F32)
    b_row = jnp.dot(_log_sigmoid(g_row), tri_t, precision=HIGHEST, preferred_element_type=F32)
    lane = lax.broadcasted_iota(I32, (1, LANES), 1)
    num_mask = (lane < ML_HEAD_DIM).astype(F32)
    den_mask = (lane == ML_HEAD_DIM).astype(F32)
    last = 0 if rev else n - 1
    outs = []
    for h in range(ML_HEADS):
        hs = slice(h * LANES, (h + 1) * LANES)
        bc = b_col[:, ML_HEADS + h:ML_HEADS + h + 1]
        br = b_row[ML_HEADS + h:ML_HEADS + h + 1, :]
        ic = g_col[:, h:h + 1]
        ir = g_row[h:h + 1, :]
        m_prev = m_scr[h:h + 1, 0:1]
        dmat = jnp.where(mask, bc - br + ir, -jnp.inf)
        inter = bc + m_prev
        m_t = jnp.maximum(inter, jnp.max(dmat, axis=1, keepdims=True))
        s = jnp.dot(q[:, hs].astype(BF16), kt[hs, :].astype(BF16), preferred_element_type=F32)
        p = s * jnp.exp(dmat - m_t)
        w_inter = jnp.exp(inter - m_t)
        vh = v[:, hs]
        c_old = c_scr[h]
        numden = (jnp.dot(p.astype(BF16), vh.astype(BF16), preferred_element_type=F32)
                  + w_inter * jnp.dot(q[:, hs].astype(BF16), c_old.astype(BF16), preferred_element_type=F32))
        den = jnp.sum(numden * den_mask, axis=1, keepdims=True)
        hh = numden * num_mask / jnp.maximum(jnp.abs(den), jnp.exp(-m_t))
        b_last = bc[last:last + 1, :]
        ws_col = b_last - bc + ic
        m_new = jnp.maximum(b_last + m_prev, jnp.max(ws_col, axis=0, keepdims=True))
        decay = jnp.exp(b_last + m_prev - m_new)
        wv = (jnp.exp(ws_col - m_new) * vh).astype(BF16)
        c_scr[h] = decay * c_old + jnp.dot(kt[hs, :].astype(BF16), wv, preferred_element_type=F32)
        m_scr[h:h + 1, :] = jnp.broadcast_to(m_new, (1, LANES))
        if rev:
            hsum = hf_ref[:, hs] + hh
            mu = jnp.sum(hsum, axis=1, keepdims=True) * (1.0 / ML_HEAD_DIM)
            cen = (hsum - mu) * num_mask
            var = jnp.sum(cen * cen, axis=1, keepdims=True) * (1.0 / ML_HEAD_DIM)
            hh = cen * lax.rsqrt(var + EPS)
        outs.append(hh)
    hcat = jnp.concatenate(outs, axis=1)
    if rev:
        z = z_ref[...]
        o_ref[...] = ((hcat * ng_ref[...] + sk_ref[...] * xc) * (z * _sigmoid(z))).astype(o_ref.dtype)
    else:
        o_ref[...] = hcat


def _mlstm_dir(u_m, hf, wts, geo, rev):
    w = ML_PAD
    bsz, lpb = geo["batch"], geo["lpb"]
    specs = _chunk_specs(w, 0, geo, rev)
    args = [u_m, u_m, u_m]
    cur = specs[0].index_map
    if rev:
        specs += [pl.BlockSpec((TILE, w), lambda b, j: (_chunk_block(b, j, geo, rev), 1)), pl.BlockSpec((TILE, w), cur)]
        args += [u_m, hf]
    const = lambda b, j: (0, 0)
    names = ["cw", "cb", "wq", "wk", "wkt", "wv", "wg", "wgt", "bg", "bgt"] + (["ng", "sk"] if rev else [])
    for nm in names:
        specs.append(pl.BlockSpec(wts[nm].shape, const))
        args.append(wts[nm])
    return pl.pallas_call(
        functools.partial(_mlstm_kernel, rev=rev, lpb=lpb),
        grid=(bsz, lpb + 1),
        in_specs=specs,
        out_specs=pl.BlockSpec((TILE, w), cur),
        out_shape=jax.ShapeDtypeStruct((geo["t_all"], w), BF16 if rev else F32),
        scratch_shapes=[pltpu.VMEM((ML_HEADS, LANES, LANES), F32), pltpu.VMEM((SUBLANES, LANES), F32)],
        compiler_params=_params("arbitrary", "arbitrary"),
        name="mlstm_bwd" if rev else "mlstm_fwd",
    )(*args)


def _mla_proj_kernel(ub_ref, cos_ref, sin_ref, gq_ref, w1_ref, w2_ref, gkv_ref, wk_ref, wv_ref, q_ref, k_ref, v_ref):
    ub = ub_ref[...]
    cos = cos_ref[...]
    sin = sin_ref[...]
    qn = _rms(ub[:, :MLA_Q_RANK], gq_ref[...]).astype(BF16)
    qa = jnp.dot(qn, w1_ref[...], preferred_element_type=F32)
    qb = jnp.dot(qn, w2_ref[...], preferred_element_type=F32)
    kvn = _rms(ub[:, MLA_Q_RANK:MLA_Q_RANK + MLA_KV_RANK], gkv_ref[...]).astype(BF16)
    kn = jnp.dot(kvn, wk_ref[...], preferred_element_type=F32)
    lane = lax.broadcasted_iota(I32, (1, LANES), 1)
    ones_lane = (lane == MLA_V).astype(F32)
    vn = jnp.dot(kvn, wv_ref[...], preferred_element_type=F32)
    off = MLA_Q_RANK + MLA_KV_RANK
    kr = ub[:, off:off + LANES] * cos + ub[:, off + LANES:off + 2 * LANES] * sin
    for h in range(MLA_HEADS):
        hs = slice(h * LANES, (h + 1) * LANES)
        q_ref[0, h] = (qa[:, hs] * cos + qb[:, hs] * sin).astype(BF16)
        k_ref[0, h] = (kn[:, hs] + kr).astype(BF16)
        v_ref[0, h] = (vn[:, hs] + ones_lane).astype(BF16)


def _mla_proj(u_b, cos_t, sin_t, wts, geo):
    n_tiles = geo["t_all"] // TILE
    n_lat, lpb, bsz = geo["n_lat"], geo["lpb"], geo["batch"]
    hw = MLA_HEADS * LANES

    def batch_of(i):
        return jnp.where(i < n_lat, i // lpb, i - n_lat)

    def blk_of(i):
        return jnp.where(i < n_lat, i % lpb, lpb)

    const = lambda i: (0, 0)
    head_spec = pl.BlockSpec((1, MLA_HEADS, TILE, LANES), lambda i: (batch_of(i), 0, blk_of(i), 0))
    head_shape = jax.ShapeDtypeStruct((bsz, MLA_HEADS, (lpb + 1) * TILE, LANES), BF16)
    return pl.pallas_call(
        _mla_proj_kernel,
        grid=(n_tiles,),
        in_specs=[
            pl.BlockSpec((TILE, UB_W), lambda i: (i, 0)),
            pl.BlockSpec((TILE, LANES), lambda i: (blk_of(i), 0)),
            pl.BlockSpec((TILE, LANES), lambda i: (blk_of(i), 0)),
            pl.BlockSpec((1, MLA_Q_RANK), const),
            pl.BlockSpec((MLA_Q_RANK, hw), const),
            pl.BlockSpec((MLA_Q_RANK, hw), const),
            pl.BlockSpec((1, MLA_KV_RANK), const),
            pl.BlockSpec((MLA_KV_RANK, hw), const),
            pl.BlockSpec((MLA_KV_RANK, hw), const),
        ],
        out_specs=[head_spec, head_spec, head_spec],
        out_shape=[head_shape, head_shape, head_shape],
        compiler_params=_params("arbitrary"),
        name="mla_proj",
    )(u_b, cos_t, sin_t, wts["gq"], wts["w1"], wts["w2"], wts["gkv"], wts["wk"], wts["wv"])


def _flash_kernel(q_ref, k_ref, v_ref, o_ref, m_scr, acc_scr, *, k_start, nk, tk):
    lane = lax.broadcasted_iota(I32, (1, LANES), 1)
    den_mask = (lane == MLA_V).astype(F32)
    outs = []
    for j in range(2):
        q = q_ref[0, j]
        m_scr[...] = jnp.full_like(m_scr, -jnp.inf)
        acc_scr[...] = jnp.zeros_like(acc_scr)

        def body(i, carry, j=j, q=q):
            start = k_start + i * tk
            kk = k_ref[0, j, start:start + tk, :]
            vv = v_ref[0, j, start:start + tk, :]
            s = lax.dot_general(q, kk, NT_DIMS, preferred_element_type=F32)
            cols = [s[:, c * LANES:(c + 1) * LANES] for c in range(tk // LANES)]
            mp = cols[0]
            for sc in cols[1:]:
                mp = jnp.maximum(mp, sc)
            m_old = m_scr[...]
            m_new = jnp.maximum(m_old, jnp.broadcast_to(jnp.max(mp, axis=1, keepdims=True), mp.shape))
            p = jnp.concatenate([jnp.exp2(sc - m_new).astype(BF16) for sc in cols], axis=1)
            acc_scr[...] = jnp.exp2(m_old - m_new) * acc_scr[...] + jnp.dot(p, vv, preferred_element_type=F32)
            m_scr[...] = m_new
            return carry

        for i in range(nk):
            body(i, 0)
        acc = acc_scr[...]
        den = jnp.sum(acc * den_mask, axis=1, keepdims=True)
        outs.append(acc / den)
    o_ref[...] = jnp.where(lane < MLA_V, outs[0], pltpu.roll(outs[1], MLA_V, 1)).astype(o_ref.dtype)


def _key_tile(n):
    for cand in range(min(n, 1024) // LANES * LANES, 0, -LANES):
        if n % cand == 0:
            return cand
    raise ValueError(n)


def _flash(q, k, v, tq, q_blk0, nq, k_start, k_len, out_blk0, t_all):
    bsz, heads, rows, _ = q.shape
    tk = _key_tile(k_len)
    kern = functools.partial(_flash_kernel, k_start=k_start, nk=k_len // tk, tk=tk)
    return pl.pallas_call(
        kern,
        grid=(bsz, heads // 2, nq),
        in_specs=[
            pl.BlockSpec((1, 2, tq, LANES), lambda b, h, i: (b, h, q_blk0 + i, 0)),
            pl.BlockSpec((1, 2, rows, LANES), lambda b, h, i: (b, h, 0, 0)),
            pl.BlockSpec((1, 2, rows, LANES), lambda b, h, i: (b, h, 0, 0)),
        ],
        out_specs=pl.BlockSpec((tq, LANES), lambda b, h, i: (out_blk0 + b * nq + i, h)),
        out_shape=jax.ShapeDtypeStruct((t_all, heads * MLA_V), BF16),
        scratch_shapes=[pltpu.VMEM((tq, LANES), F32), pltpu.VMEM((tq, LANES), F32)],
        compiler_params=_params("arbitrary", "arbitrary", "arbitrary"),
        name="mla_attention",
    )(q, k, v)


def _out_kernel(ya_ref, yb_ref, yc_ref, x_ref, mod_ref, w_ref, g_ref, wr_ref, br_ref,
                xo_ref, h2_ref, te_ref, tg_ref, rk_ref, cnt_ref, *, n_lat, lat_per_batch, ctx_row):
    i = pl.program_id(0)

    @pl.when(i == 0)
    def _():
        cnt_ref[...] = jnp.zeros_like(cnt_ref)

    r = _mod_row(i, n_lat, lat_per_batch, ctx_row)
    g1 = mod_ref[2, pl.ds(r, 1), :]
    sh2 = mod_ref[3, pl.ds(r, 1), :]
    sc2 = mod_ref[4, pl.ds(r, 1), :]
    y = jnp.concatenate([ya_ref[...], yb_ref[...], yc_ref[...]], axis=1)
    x1 = x_ref[...] + g1 * jnp.dot(y, w_ref[...], preferred_element_type=F32)
    xo_ref[...] = x1
    h2 = _rms(x1, g_ref[...]) * (1.0 + sc2) + sh2
    h2_ref[...] = h2
    logits = jnp.dot(h2, wr_ref[...], precision=HIGHEST, preferred_element_type=F32) + br_ref[...]
    n, ne = logits.shape
    lane_e = lax.broadcasted_iota(I32, (n, ne), 1).astype(F32)
    lane_o = lax.broadcasted_iota(I32, (n, LANES), 1)
    vals, idxs = [], []
    for _ in range(TOP_K):
        m = jnp.max(logits, axis=1, keepdims=True)
        idx = jnp.min(jnp.where(logits == m, lane_e, float(ne)), axis=1, keepdims=True)
        logits = jnp.where(lane_e == idx, -jnp.inf, logits)
        vals.append(m)
        idxs.append(idx)
    exps = [jnp.exp(vv - vals[0]) for vv in vals]
    tot = exps[0] + exps[1] + exps[2] + exps[3]
    hits = [(lane_e == idx).astype(F32) for idx in idxs]
    chosen = hits[0] + hits[1] + hits[2] + hits[3]
    ti = lax.broadcasted_iota(I32, (n, n), 0)
    si = lax.broadcasted_iota(I32, (n, n), 1)
    before = jnp.dot((si < ti).astype(BF16), chosen.astype(BF16), preferred_element_type=F32) + cnt_ref[0:1, :]
    cnt_ref[...] = cnt_ref[...] + jnp.sum(chosen, axis=0, keepdims=True)
    te = jnp.zeros((n, LANES), F32)
    tg = jnp.zeros((n, LANES), F32)
    rk = jnp.zeros((n, LANES), F32)
    for kk in range(TOP_K):
        te = jnp.where(lane_o == kk, idxs[kk], te)
        tg = jnp.where(lane_o == kk, exps[kk] / tot, tg)
        rk = jnp.where(lane_o == kk, jnp.sum(hits[kk] * before, axis=1, keepdims=True), rk)
    te_ref[...] = te.astype(I32)
    tg_ref[...] = tg
    rk_ref[...] = rk.astype(I32)


def _out_proj(ya, yb, yc, x_all, mod_l, w_out_p, g, w_router, b_router, geo, n_tiles):
    d = x_all.shape[1]
    rows = n_tiles * TILE
    kern = functools.partial(_out_kernel, n_lat=geo["n_lat"], lat_per_batch=geo["lpb"], ctx_row=geo["ctx_row"])
    row_blk = lambda i: (i, 0)
    const = lambda i: (0, 0)
    w_router = jnp.pad(w_router, ((0, 0), (0, LANES - N_EXPERTS)))
    b_router = jnp.pad(b_router, (0, LANES - N_EXPERTS), constant_values=-jnp.inf)
    return pl.pallas_call(
        kern,
        grid=(n_tiles,),
        in_specs=[
            pl.BlockSpec((TILE, ya.shape[1]), row_blk),
            pl.BlockSpec((TILE, yb.shape[1]), row_blk),
            pl.BlockSpec((TILE, yc.shape[1]), row_blk),
            pl.BlockSpec((TILE, d), row_blk),
            pl.BlockSpec(mod_l.shape, lambda i: (0, 0, 0)),
            pl.BlockSpec(w_out_p.shape, const),
            pl.BlockSpec((1, d), const),
            pl.BlockSpec(w_router.shape, const),
            pl.BlockSpec((1, LANES), const),
        ],
        out_specs=[pl.BlockSpec((TILE, d), row_blk), pl.BlockSpec((TILE, d), row_blk),
                   pl.BlockSpec((TILE, LANES), row_blk), pl.BlockSpec((TILE, LANES), row_blk),
                   pl.BlockSpec((TILE, LANES), row_blk), pl.BlockSpec((SUBLANES, LANES), const)],
        out_shape=[jax.ShapeDtypeStruct((rows, d), F32), jax.ShapeDtypeStruct((rows, d), F32),
                   jax.ShapeDtypeStruct((rows, LANES), I32), jax.ShapeDtypeStruct((rows, LANES), F32),
                   jax.ShapeDtypeStruct((rows, LANES), I32), jax.ShapeDtypeStruct((SUBLANES, LANES), F32)],
        compiler_params=_params("arbitrary"),
        name="out_proj_router",
    )(ya, yb, yc, x_all, mod_l, w_out_p, g.reshape(1, d), w_router, b_router.reshape(1, LANES))


def _dispatch_kernel(zrow_ref, nu_ref, dest_ref, h_ref, xs_hbm, zbuf, zsem, sem):
    i = pl.program_id(0)
    n_tok = h_ref.shape[0]
    n_rows = TOP_K * n_tok
    n_blocks = xs_hbm.shape[0] // MOE_BLOCK

    def zero_copy(blk_row):
        row = pl.multiple_of(blk_row, MOE_BLOCK)
        return pltpu.make_async_copy(zbuf, xs_hbm.at[pl.ds(row, MOE_BLOCK), :], zsem)

    @pl.when(i == 0)
    def _():
        zbuf[...] = jnp.zeros_like(zbuf)
        for start in (True, False):
            for e in range(N_EXPERTS):
                @pl.when(zrow_ref[e] >= 0)
                def _(e=e, start=start):
                    zero_copy(zrow_ref[e]).start() if start else zero_copy(zrow_ref[e]).wait()

            def tail(b, carry, start=start):
                zero_copy(b * MOE_BLOCK).start() if start else zero_copy(b * MOE_BLOCK).wait()
                return carry
            lax.fori_loop(nu_ref[0], n_blocks, tail, 0)

    def row_copy(r):
        t = r % n_tok
        return pltpu.make_async_copy(h_ref.at[pl.ds(t, 1), :], xs_hbm.at[pl.ds(dest_ref[0, 0, r], 1), :], sem)

    def issue(r, carry):
        row_copy(r).start()
        return carry

    def drain(r, carry):
        row_copy(r).wait()
        return carry

    lax.fori_loop(0, n_rows, issue, 0, unroll=8)
    lax.fori_loop(0, n_rows, drain, 0, unroll=8)


def _moe_dispatch(h2, dest, zrow, n_used, n_slots):
    t_moe, d = dest.shape[0], h2.shape[1]
    n_tiles = t_moe // TILE
    dest3 = dest.reshape(n_tiles, TILE, TOP_K).transpose(0, 2, 1).reshape(n_tiles, 1, TOP_K * TILE)
    grid_spec = pltpu.PrefetchScalarGridSpec(
        num_scalar_prefetch=2,
        grid=(n_tiles,),
        in_specs=[
            pl.BlockSpec((1, 1, TOP_K * TILE), lambda i, zr, nu: (i, 0, 0), memory_space=pltpu.SMEM),
            pl.BlockSpec((TILE, d), lambda i, zr, nu: (i, 0)),
        ],
        out_specs=pl.BlockSpec(memory_space=pl.ANY),
        scratch_shapes=[pltpu.VMEM((MOE_BLOCK, d), F32), pltpu.SemaphoreType.DMA, pltpu.SemaphoreType.DMA],
    )
    return pl.pallas_call(
        _dispatch_kernel,
        grid_spec=grid_spec,
        out_shape=jax.ShapeDtypeStruct((n_slots, d), F32),
        compiler_params=_params("arbitrary"),
        name="moe_dispatch",
    )(zrow, n_used, dest3, h2)


def _moe_kernel(be_ref, nu_ref, x_ref, wgu_ref, bgu_ref, wd_ref, bd_ref, y_ref, wgu_bf, wd_bf):
    i = pl.program_id(0)

    @pl.when(i < nu_ref[0])
    def _():
        prev_e = be_ref[jnp.maximum(i - 1, 0)]

        @pl.when(jnp.logical_or(i == 0, be_ref[i] != prev_e))
        def _():
            wgu_bf[...] = wgu_ref[0].astype(BF16)
            wd_bf[...] = wd_ref[0].astype(BF16)

        x = x_ref[...].astype(BF16)
        gu = jnp.dot(x, wgu_bf[...], preferred_element_type=F32) + bgu_ref[0]
        glu = jnp.minimum(gu[:, :D_EXPERT], SWIGLU_LIMIT)
        lin = jnp.clip(gu[:, D_EXPERT:], -SWIGLU_LIMIT, SWIGLU_LIMIT)
        act = glu * _sigmoid(SWIGLU_ALPHA * glu) * (lin + 1.0)
        y_ref[...] = jnp.dot(act.astype(BF16), wd_bf[...], preferred_element_type=F32) + bd_ref[0]

    @pl.when(i >= nu_ref[0])
    def _():
        y_ref[...] = jnp.zeros_like(y_ref)


def _moe_experts(x_sorted, block_e, n_used, w_gu, b_gu, w_down, b_down):
    n_slots, d = x_sorted.shape
    n_blocks = n_slots // MOE_BLOCK

    def blk(i, be, nu):
        return jnp.minimum(i, nu[0] - 1)

    grid_spec = pltpu.PrefetchScalarGridSpec(
        num_scalar_prefetch=2,
        grid=(n_blocks,),
        in_specs=[
            pl.BlockSpec((MOE_BLOCK, d), lambda i, be, nu: (i, 0)),
            pl.BlockSpec((1, d, 2 * D_EXPERT), lambda i, be, nu: (be[blk(i, be, nu)], 0, 0)),
            pl.BlockSpec((1, 1, 2 * D_EXPERT), lambda i, be, nu: (be[blk(i, be, nu)], 0, 0)),
            pl.BlockSpec((1, D_EXPERT, d), lambda i, be, nu: (be[blk(i, be, nu)], 0, 0)),
            pl.BlockSpec((1, 1, d), lambda i, be, nu: (be[blk(i, be, nu)], 0, 0)),
        ],
        out_specs=pl.BlockSpec((MOE_BLOCK, d), lambda i, be, nu: (i, 0)),
        scratch_shapes=[pltpu.VMEM((d, 2 * D_EXPERT), BF16), pltpu.VMEM((D_EXPERT, d), BF16)],
    )
    return pl.pallas_call(
        _moe_kernel,
        grid_spec=grid_spec,
        out_shape=jax.ShapeDtypeStruct((n_slots, d), F32),
        compiler_params=_params("arbitrary"),
        name="moe_experts",
    )(block_e, n_used, x_sorted, w_gu, b_gu.reshape(N_EXPERTS, 1, -1), w_down, b_down.reshape(N_EXPERTS, 1, -1))


def _comb_kernel(inv_ref, inv_next_ref, y_hbm, tg_ref, x_ref, mod_ref, fg_ref, o_ref, cbuf, sem,
                 *, n_lat, lat_per_batch, ctx_row, final):
    i = pl.program_id(0)
    n_steps = pl.num_programs(0)
    slot = i % 2
    rows = cbuf.shape[1]
    per_tile = TILE // COMB_TILE

    def row_copy(row, dst_slot, r):
        return pltpu.make_async_copy(y_hbm.at[pl.ds(row, 1), :], cbuf.at[dst_slot, pl.ds(r, 1), :], sem.at[dst_slot])

    def issue(idx_ref, dst_slot):
        def body(r, carry):
            row_copy(idx_ref[0, 0, r], dst_slot, r).start()
            return carry
        lax.fori_loop(0, rows, body, 0, unroll=8)

    @pl.when(i == 0)
    def _():
        issue(inv_ref, 0)

    @pl.when(i + 1 < n_steps)
    def _():
        issue(inv_next_ref, 1 - slot)

    def drain(r, carry):
        row_copy(0, slot, r).wait()
        return carry
    lax.fori_loop(0, rows, drain, 0, unroll=8)

    r = _mod_row(i // per_tile, n_lat, lat_per_batch, ctx_row)
    g2 = mod_ref[5, pl.ds(r, 1), :]
    tg = tg_ref[...]
    f = tg[:, 0:1] * cbuf[slot, 0:COMB_TILE, :]
    for kk in range(1, TOP_K):
        f = f + tg[:, kk:kk + 1] * cbuf[slot, kk * COMB_TILE:(kk + 1) * COMB_TILE, :]
    x2 = x_ref[...] + g2 * f
    o_ref[...] = _rms(x2, fg_ref[...]) if final else x2


def _moe_combine(y_sorted, dest, tg, x_all, mod_l, final_g, geo, final):
    t_moe = dest.shape[0]
    d = x_all.shape[1]
    n_steps = t_moe // COMB_TILE
    inv3 = dest.reshape(n_steps, COMB_TILE, TOP_K).transpose(0, 2, 1).reshape(n_steps, 1, TOP_K * COMB_TILE)
    kern = functools.partial(_comb_kernel, n_lat=geo["n_lat"], lat_per_batch=geo["lpb"], ctx_row=geo["ctx_row"],
                             final=final)
    return pl.pallas_call(
        kern,
        grid=(n_steps,),
        in_specs=[
            pl.BlockSpec((1, 1, TOP_K * COMB_TILE), lambda i: (i, 0, 0), memory_space=pltpu.SMEM),
            pl.BlockSpec((1, 1, TOP_K * COMB_TILE), lambda i: (jnp.minimum(i + 1, n_steps - 1), 0, 0),
                         memory_space=pltpu.SMEM),
            pl.BlockSpec(memory_space=pl.ANY),
            pl.BlockSpec((COMB_TILE, LANES), lambda i: (i, 0)),
            pl.BlockSpec((COMB_TILE, d), lambda i: (i, 0)),
            pl.BlockSpec(mod_l.shape, lambda i: (0, 0, 0)),
            pl.BlockSpec((1, d), lambda i: (0, 0)),
        ],
        out_specs=pl.BlockSpec((COMB_TILE, d), lambda i: (i, 0)),
        out_shape=jax.ShapeDtypeStruct((t_moe, d), F32),
        scratch_shapes=[pltpu.VMEM((2, TOP_K * COMB_TILE, d), F32), pltpu.SemaphoreType.DMA((2,))],
        compiler_params=_params("arbitrary"),
        name="moe_combine",
    )(inv3, inv3, y_sorted, tg, x_all, mod_l, final_g.reshape(1, d))


def _route(te, rk, cnt, t_moe):
    counts = cnt[0, :N_EXPERTS].astype(I32)
    padded = (counts + MOE_BLOCK - 1) // MOE_BLOCK * MOE_BLOCK
    padded_end = jnp.cumsum(padded)
    padded_start = padded_end - padded
    experts = jnp.arange(N_EXPERTS, dtype=I32)
    e_tok = te[:t_moe, :TOP_K]
    dest = rk[:t_moe, :TOP_K] + jnp.sum(jnp.where(e_tok[..., None] == experts, padded_start, 0), axis=-1)
    n_blocks = -(-(t_moe * TOP_K) // MOE_BLOCK) + N_EXPERTS
    blk_start = jnp.arange(n_blocks, dtype=I32) * MOE_BLOCK
    block_e = jnp.minimum(jnp.sum((padded_end[None, :] <= blk_start[:, None]).astype(I32), axis=1), N_EXPERTS - 1)
    n_used = (padded_end[-1:] // MOE_BLOCK).astype(I32)
    zrow = jnp.where(counts > 0, padded_end - MOE_BLOCK, -1).astype(I32)
    return dest.astype(I32), block_e, n_used, zrow, n_blocks * MOE_BLOCK


def _block_diag_dense(w):
    g, i, j = w.shape
    out = jnp.zeros((g * i, g * j), w.dtype)
    for n in range(g):
        out = out.at[n * i:(n + 1) * i, n * j:(n + 1) * j].set(w[n])
    return out


def _pad_heads(w, heads, axis):
    shape = list(w.shape)
    shape[axis:axis + 1] = [heads, shape[axis] // heads]
    w = w.reshape(shape)
    pad = [(0, 0)] * w.ndim
    pad[axis + 1] = (0, LANES - shape[axis + 1])
    w = jnp.pad(w, pad)
    shape[axis:axis + 2] = [heads * LANES]
    return w.reshape(shape)


_ROPE_SWAP = np.concatenate([np.arange(8, 16), np.arange(0, 8), np.arange(24, 32), np.arange(16, 24)])


def _prep_in_weight(w_in):
    d = w_in.shape[0]
    o = np.cumsum([0, LRU_WIDTH, LRU_WIDTH, MLA_Q_RANK, MLA_KV_RANK, MLA_ROPE, ML_WIDTH, ML_WIDTH])
    a_xg = w_in[:, o[0]:o[2]]
    b_qkv = w_in[:, o[2]:o[4]]
    kr = w_in[:, o[4]:o[5]]
    z_nope = jnp.zeros((d, MLA_NOPE), w_in.dtype)
    z_tail = jnp.zeros((d, LANES - MLA_NOPE - MLA_ROPE), w_in.dtype)
    m_x = _pad_heads(w_in[:, o[5]:o[6]], ML_HEADS, 1)
    m_z = _pad_heads(w_in[:, o[6]:o[7]], ML_HEADS, 1)
    return jnp.concatenate([a_xg, b_qkv, z_nope, kr, z_tail, z_nope, kr[:, _ROPE_SWAP], z_tail, m_x, m_z],
                           axis=1).astype(BF16)


def _prep_out_weight(w_out):
    a = w_out[:LRU_WIDTH]
    b = w_out[LRU_WIDTH:LRU_WIDTH + MLA_HEADS * MLA_V]
    c = _pad_heads(w_out[LRU_WIDTH + MLA_HEADS * MLA_V:], ML_HEADS, 0)
    return jnp.concatenate([a, b, c], axis=0).astype(BF16)


def _prep_mla(q_norm_g, w_qb, kv_norm_g, w_kvb):
    scale = (MLA_NOPE + MLA_ROPE) ** -0.5 * math.log2(math.e)
    rq = w_qb.shape[0]
    wq = w_qb.reshape(rq, MLA_HEADS, MLA_NOPE + MLA_ROPE) * scale
    nope, rope = wq[..., :MLA_NOPE], wq[..., MLA_NOPE:]
    z32 = jnp.zeros((rq, MLA_HEADS, LANES - MLA_NOPE - MLA_ROPE), w_qb.dtype)
    w1 = jnp.concatenate([nope, rope, z32], axis=-1).reshape(rq, MLA_HEADS * LANES)
    w2 = jnp.concatenate([jnp.zeros_like(nope), rope[..., _ROPE_SWAP], z32], axis=-1).reshape(rq, MLA_HEADS * LANES)
    rk = w_kvb.shape[0]
    wkv = w_kvb.reshape(rk, MLA_HEADS, MLA_NOPE + MLA_V)
    z64 = jnp.zeros((rk, MLA_HEADS, LANES - MLA_NOPE), w_kvb.dtype)
    wk = jnp.concatenate([wkv[..., :MLA_NOPE], z64], axis=-1).reshape(rk, MLA_HEADS * LANES)
    wv = jnp.concatenate([wkv[..., MLA_NOPE:], z64], axis=-1).reshape(rk, MLA_HEADS * LANES)
    return {"gq": q_norm_g.reshape(1, -1), "w1": w1.astype(BF16), "w2": w2.astype(BF16),
            "gkv": kv_norm_g.reshape(1, -1), "wk": wk.astype(BF16), "wv": wv.astype(BF16)}


def _prep_mlstm(conv_w, conv_b, wq, wk, wv, w_gate_d, b_gate_d, norm_g, skip):
    def proj(w):
        return _pad_heads(_pad_heads(_block_diag_dense(w), ML_HEADS, 0), ML_HEADS, 1)

    wk_p = proj(wk) * (ML_HEAD_DIM ** -0.5)
    wg = jnp.concatenate([_pad_heads(w_gate_d[i * ML_WIDTH:(i + 1) * ML_WIDTH], ML_HEADS, 0) for i in range(3)], axis=0)
    ng = w_gate_d.shape[1]
    wg = jnp.pad(wg, ((0, 0), (0, LANES - ng)))
    b_gate_d = jnp.pad(b_gate_d, (0, LANES - ng))
    gate_rows = 2 * SUBLANES
    return {
        "cw": _pad_heads(conv_w, ML_HEADS, 1), "cb": _pad_heads(conv_b.reshape(1, -1), ML_HEADS, 1),
        "wq": proj(wq).astype(BF16), "wk": wk_p.astype(BF16), "wkt": wk_p.T.astype(BF16), "wv": proj(wv).astype(BF16),
        "wg": wg.astype(BF16), "wgt": wg.T[:gate_rows].astype(BF16),
        "bg": b_gate_d.reshape(1, -1), "bgt": b_gate_d[:gate_rows].reshape(-1, 1),
        "ng": _pad_heads(norm_g.reshape(1, -1), ML_HEADS, 1), "sk": _pad_heads(skip.reshape(1, -1), ML_HEADS, 1),
    }


def _rope_tables(seq, ctx_len):
    rows = seq // GRID_W
    row = jnp.repeat(jnp.arange(rows, dtype=I32), GRID_W)
    col = jnp.tile(jnp.arange(GRID_W, dtype=I32), rows)
    freqs = ROPE_BASE ** (-jnp.arange(ROPE_AXIS_FREQ, dtype=F32) / ROPE_AXIS_FREQ)
    ang_r, ang_c = row[:, None] * freqs, col[:, None] * freqs
    cos32 = jnp.concatenate([jnp.cos(ang_r), jnp.cos(ang_r), jnp.cos(ang_c), jnp.cos(ang_c)], axis=1)
    sin32 = jnp.concatenate([-jnp.sin(ang_r), jnp.sin(ang_r), -jnp.sin(ang_c), jnp.sin(ang_c)], axis=1)
    cos32 = jnp.concatenate([cos32, jnp.ones((ctx_len, MLA_ROPE), F32)], axis=0)
    sin32 = jnp.concatenate([sin32, jnp.zeros((ctx_len, MLA_ROPE), F32)], axis=0)
    n = seq + ctx_len
    tail = jnp.zeros((n, LANES - MLA_NOPE - MLA_ROPE), F32)
    cos_t = jnp.concatenate([jnp.ones((n, MLA_NOPE), F32), cos32, tail], axis=1)
    sin_t = jnp.concatenate([jnp.zeros((n, MLA_NOPE), F32), sin32, tail], axis=1)
    return cos_t, sin_t


def kernel(x, c, ctx, c_ctx, norm1_g, norm2_g, w_mod, b_mod, w_in, w_out, lru_conv_w, lru_conv_b, lru_wa, lru_ba,
           lru_wx, lru_bx, lru_lambda, mla_q_norm_g, mla_w_qb, mla_kv_norm_g, mla_w_kvb, ml_conv_w, ml_conv_b,
           ml_wq, ml_wk, ml_wv, ml_w_gate, ml_b_gate, ml_norm_g, ml_skip, w_router, b_router, w_gu, b_gu, w_down,
           b_down, final_g):
    bsz, seq, d = x.shape
    ctx_len = ctx.shape[1]
    depth = w_mod.shape[0]
    assert ctx_len == TILE and seq % (2 * TILE) == 0 and bsz + 1 <= SUBLANES
    t_lat = bsz * seq
    t_all = t_lat + bsz * ctx_len
    geo = {"batch": bsz, "lpb": seq // TILE, "n_lat": t_lat // TILE, "ctx_row": bsz, "t_all": t_all}

    cv = jnp.zeros((SUBLANES, d), F32).at[:bsz].set(c).at[bsz].set(c_ctx)
    mod = _modulation(cv, w_mod, b_mod)
    cos_t, sin_t = _rope_tables(seq, ctx_len)
    x_all = jnp.concatenate([x.reshape(t_lat, d), ctx.reshape(bsz * ctx_len, d)], axis=0)

    out = None
    for l in range(depth):
        last = l == depth - 1
        u_a, u_b, u_m = _in_proj(x_all, mod[l], norm1_g[l], _prep_in_weight(w_in[l]), geo)

        hf = None
        for dd in range(2):
            wg = jnp.concatenate([_block_diag_dense(lru_wa[l, dd]), _block_diag_dense(lru_wx[l, dd])], axis=1)
            bg = jnp.concatenate([lru_ba[l, dd], lru_bx[l, dd]])
            hf = _lru_dir(u_a, hf, lru_conv_w[l], lru_conv_b[l], wg.astype(BF16), bg, lru_lambda[l, dd], geo, dd == 1)
        ya = hf

        q, k, v = _mla_proj(u_b, cos_t, sin_t, _prep_mla(mla_q_norm_g[l], mla_w_qb[l], mla_kv_norm_g[l], mla_w_kvb[l]), geo)
        tq = 2 * TILE
        yb = _flash(q, k, v, tq, 0, seq // tq, 0, seq + ctx_len, 0, t_lat)
        if not last:
            yb_c = _flash(q, k, v, TILE, seq // TILE, 1, seq, ctx_len, 0, bsz * ctx_len)
            yb = jnp.concatenate([yb, yb_c], axis=0)

        hf = None
        for dd in range(2):
            wts = _prep_mlstm(ml_conv_w[l], ml_conv_b[l], ml_wq[l], ml_wk[l], ml_wv[l], ml_w_gate[l, dd],
                              ml_b_gate[l, dd], ml_norm_g[l], ml_skip[l])
            hf = _mlstm_dir(u_m, hf, wts, geo, dd == 1)
        yc = hf

        n_tiles = (t_lat if last else t_all) // TILE
        x_mid, h2, te, tg, rk, cnt = _out_proj(ya, yb, yc, x_all, mod[l], _prep_out_weight(w_out[l]), norm2_g[l],
                                               w_router[l], b_router[l], geo, n_tiles)
        t_moe = n_tiles * TILE
        dest, block_e, n_used, zrow, n_slots = _route(te, rk, cnt, t_moe)
        x_sorted = _moe_dispatch(h2, dest, zrow, n_used, n_slots)
        y_sorted = _moe_experts(x_sorted, block_e, n_used, w_gu[l], b_gu[l], w_down[l], b_down[l])
        x_all = _moe_combine(y_sorted, dest, tg, x_mid, mod[l], final_g, geo, last)
        if last:
            out = x_all.reshape(bsz, seq, d)
    return out
```

```python
import functools
import math

import jax
import jax.numpy as jnp
import numpy as np
from jax import lax
from jax.experimental import pallas as pl
from jax.experimental.pallas import tpu as pltpu

F32 = jnp.float32
BF16 = jnp.bfloat16
I32 = jnp.int32
HIGHEST = lax.Precision.HIGHEST

LANES = 128
SUBLANES = 8
VMEM_LIMIT_BYTES = 56 * 1024 * 1024

GRID_W = 64
EPS = 1e-6
LRU_WIDTH = 256
LRU_C = 8.0
CONV_W = 4
MLA_HEADS = 8
MLA_NOPE = 64
MLA_ROPE = 32
MLA_V = 64
MLA_Q_RANK = 256
MLA_KV_RANK = 128
ROPE_AXIS_FREQ = MLA_ROPE // 4
ROPE_BASE = 10000.0
ML_HEADS = 4
ML_HEAD_DIM = 64
ML_WIDTH = ML_HEADS * ML_HEAD_DIM
ML_PAD = ML_HEADS * LANES
N_EXPERTS = 32
TOP_K = 4
D_EXPERT = 1024
SWIGLU_LIMIT = 7.0
SWIGLU_ALPHA = 1.702
MOE_BLOCK = 256

TILE = 256
HALO = SUBLANES
UB_W = MLA_Q_RANK + MLA_KV_RANK + 2 * LANES
COMB_TILE = 128

NT_DIMS = (((1,), (1,)), ((), ()))


def _params(*sem):
    return pltpu.CompilerParams(dimension_semantics=sem, vmem_limit_bytes=VMEM_LIMIT_BYTES)


def _sigmoid(x):
    return 1.0 / (1.0 + jnp.exp(-x))


def _log_sigmoid(x):
    return jnp.minimum(x, 0.0) - jnp.log1p(jnp.exp(-jnp.abs(x)))


def _rms(x, g):
    return x * lax.rsqrt(jnp.mean(x * x, axis=-1, keepdims=True) + EPS) * g


def _mod_kernel(cv_ref, w_ref, b_ref, o_ref):
    cv = cv_ref[...]
    a = cv * _sigmoid(cv)
    o_ref[0, 0] = jnp.dot(a, w_ref[0], precision=HIGHEST, preferred_element_type=F32) + b_ref[0, 0]


def _modulation(cv, w_mod, b_mod):
    depth, d, _ = w_mod.shape
    rows = cv.shape[0]
    return pl.pallas_call(
        _mod_kernel,
        grid=(depth, 6),
        in_specs=[
            pl.BlockSpec((rows, d), lambda l, j: (0, 0)),
            pl.BlockSpec((1, d, d), lambda l, j: (l, 0, j)),
            pl.BlockSpec((1, 1, 1, d), lambda l, j: (l, j, 0, 0)),
        ],
        out_specs=pl.BlockSpec((1, 1, rows, d), lambda l, j: (l, j, 0, 0)),
        out_shape=jax.ShapeDtypeStruct((depth, 6, rows, d), F32),
        compiler_params=_params("arbitrary", "arbitrary"),
        name="modulation",
    )(cv, w_mod, b_mod.reshape(depth, 6, 1, d))


def _mod_row(i, n_lat, lat_per_batch, ctx_row):
    return jnp.where(i < n_lat, i // lat_per_batch, ctx_row)


def _in_kernel(x_ref, mod_ref, g_ref, w_ref, ua_ref, ub_ref, um_ref, *, n_lat, lat_per_batch, ctx_row):
    r = _mod_row(pl.program_id(0), n_lat, lat_per_batch, ctx_row)
    sh = mod_ref[0, pl.ds(r, 1), :]
    sc = mod_ref[1, pl.ds(r, 1), :]
    h = _rms(x_ref[...], g_ref[...]) * (1.0 + sc) + sh
    u = jnp.dot(h.astype(BF16), w_ref[...], preferred_element_type=F32)
    wa = ua_ref.shape[1]
    wb = ub_ref.shape[1]
    ua_ref[...] = u[:, :wa]
    ub_ref[...] = u[:, wa:wa + wb]
    um_ref[...] = u[:, wa + wb:]


def _in_proj(x_all, mod_l, g, w_in_p, geo):
    t_all, d = x_all.shape
    n_tiles = t_all // TILE
    wa, wb, wm = 2 * LRU_WIDTH, UB_W, 2 * ML_PAD
    kern = functools.partial(_in_kernel, n_lat=geo["n_lat"], lat_per_batch=geo["lpb"], ctx_row=geo["ctx_row"])
    return pl.pallas_call(
        kern,
        grid=(n_tiles,),
        in_specs=[
            pl.BlockSpec((TILE, d), lambda i: (i, 0)),
            pl.BlockSpec(mod_l.shape, lambda i: (0, 0, 0)),
            pl.BlockSpec((1, d), lambda i: (0, 0)),
            pl.BlockSpec(w_in_p.shape, lambda i: (0, 0)),
        ],
        out_specs=[
            pl.BlockSpec((TILE, wa), lambda i: (i, 0)),
            pl.BlockSpec((TILE, wb), lambda i: (i, 0)),
            pl.BlockSpec((TILE, wm), lambda i: (i, 0)),
        ],
        out_shape=[
            jax.ShapeDtypeStruct((t_all, wa), F32),
            jax.ShapeDtypeStruct((t_all, wb), F32),
            jax.ShapeDtypeStruct((t_all, wm), F32),
        ],
        compiler_params=_params("arbitrary"),
        name="in_proj",
    )(x_all, mod_l, g.reshape(1, d), w_in_p)


def _chunk_block(b, j, geo, rev):
    lat = (geo["lpb"] - j) if rev else (j - 1)
    return jnp.where(j == 0, geo["n_lat"] + b, b * geo["lpb"] + lat)


def _local_block(j, lpb, rev):
    return jnp.where(j == 0, lpb, (lpb - j) if rev else (j - 1))


def _chunk_specs(width, col, geo, rev, b):
    per = TILE // HALO
    last = geo["t_all"] // HALO - 1

    def cur(j):
        return (_chunk_block(b, j, geo, rev), col)

    def prev(j):
        return (jnp.maximum(_chunk_block(b, j, geo, rev) * per - 1, 0), col)

    def nxt(j):
        return (jnp.minimum((_chunk_block(b, j, geo, rev) + 1) * per, last), col)

    return [pl.BlockSpec((TILE, width), cur), pl.BlockSpec((HALO, width), prev), pl.BlockSpec((HALO, width), nxt)]


def _scan_call(kern, name, src, width, hf, weights, scratch, geo, rev):
    bsz, lpb = geo["batch"], geo["lpb"]
    local = lambda j: (_local_block(j, lpb, rev), 0)
    specs, args = [], []
    for b in range(bsz):
        specs += _chunk_specs(width, 0, geo, rev, b)
        args += [src, src, src]
        if rev:
            specs += [pl.BlockSpec((TILE, width), lambda j, b=b: (_chunk_block(b, j, geo, rev), 1)),
                      pl.BlockSpec((TILE, width), local)]
            args += [src, hf[b]]
    for wgt in weights:
        specs.append(pl.BlockSpec(wgt.shape, lambda j: (0, 0)))
        args.append(wgt)
    rows = (lpb + 1) * TILE
    return pl.pallas_call(
        functools.partial(kern, rev=rev, lpb=lpb, bsz=bsz),
        grid=(lpb + 1,),
        in_specs=specs,
        out_specs=[pl.BlockSpec((TILE, width), local)] * bsz,
        out_shape=[jax.ShapeDtypeStruct((rows, width), BF16 if rev else F32)] * bsz,
        scratch_shapes=scratch,
        compiler_params=_params("arbitrary"),
        name=name,
    )(*args)


def _split_scan_refs(refs, rev, bsz, n_weights, n_scratch):
    n_in = 5 if rev else 3
    batch_refs = [refs[b * n_in:(b + 1) * n_in] for b in range(bsz)]
    w0 = bsz * n_in
    weights = refs[w0:w0 + n_weights]
    outs = refs[w0 + n_weights:w0 + n_weights + bsz]
    return batch_refs, weights, outs, refs[len(refs) - n_scratch:]


def _to_token_layout(per_batch, seq):
    return jnp.concatenate([o[:seq] for o in per_batch] + [o[seq:] for o in per_batch], axis=0)


def _short_conv(x, xp_ref, xn_ref, w_ref, b_ref, j, lpb, rev):
    n = x.shape[0]
    lat = (lpb - j) if rev else (j - 1)
    is_lat = j > 0
    prev_ok = jnp.logical_and(is_lat, lat > 0)
    next_ok = jnp.logical_and(is_lat, lat < lpb - 1)
    xp = xp_ref[...] * prev_ok.astype(F32)
    xn = xn_ref[...] * next_ok.astype(F32)
    row = lax.broadcasted_iota(I32, x.shape, 0)
    x_m1 = jnp.where(row == 0, xp[HALO - 1:HALO, :], pltpu.roll(x, 1, 0))
    x_m2 = jnp.where(row == 0, xp[HALO - 2:HALO - 1, :], jnp.where(row == 1, xp[HALO - 1:HALO, :], pltpu.roll(x, 2, 0)))
    x_p1 = jnp.where(row == n - 1, xn[0:1, :], pltpu.roll(x, n - 1, 0))
    return x_m2 * w_ref[0:1, :] + x_m1 * w_ref[1:2, :] + x * w_ref[2:3, :] + x_p1 * w_ref[3:4, :] + b_ref[...]


def _lin_scan(a, b, rev):
    n = a.shape[0]
    row = lax.broadcasted_iota(I32, a.shape, 0)
    d = 1
    while d < n:
        if rev:
            a_s, b_s, valid = pltpu.roll(a, n - d, 0), pltpu.roll(b, n - d, 0), row < n - d
        else:
            a_s, b_s, valid = pltpu.roll(a, d, 0), pltpu.roll(b, d, 0), row >= d
        a_s = jnp.where(valid, a_s, 1.0)
        b_s = jnp.where(valid, b_s, 0.0)
        b = a * b_s + b
        a = a * a_s
        d *= 2
        yield
    return a, b


def _lru_kernel(*refs, rev, lpb, bsz):
    batch_refs, (cw_ref, cb_ref, wg_ref, bg_ref, lam_ref), o_refs, (h_scr,) = _split_scan_refs(refs, rev, bsz, 5, 1)
    j = pl.program_id(0)

    @pl.when(j == 0)
    def _():
        h_scr[...] = jnp.zeros_like(h_scr)

    def chunk(b):
        x_ref, xp_ref, xn_ref = batch_refs[b][:3]
        x = x_ref[...]
        n, w = x.shape
        xc = _short_conv(x, xp_ref, xn_ref, cw_ref, cb_ref, j, lpb, rev)
        yield
        gates = jnp.dot(xc.astype(BF16), wg_ref[...], preferred_element_type=F32) + bg_ref[...]
        yield
        r = _sigmoid(gates[:, :w])
        ig = _sigmoid(gates[:, w:])
        log_a = LRU_C * r * _log_sigmoid(lam_ref[...])
        a = jnp.exp(log_a)
        bb = jnp.sqrt(-jnp.tanh(log_a) * (1.0 + a * a)) * ig * xc
        yield
        a_cum, h_loc = yield from _lin_scan(a, bb, rev)
        h = a_cum * h_scr[b] + h_loc
        h_scr[b] = h[0:1, :] if rev else h[n - 1:n, :]
        if rev:
            g_ref, hf_ref = batch_refs[b][3:]
            o_refs[b][...] = ((hf_ref[...] + h) * jax.nn.gelu(g_ref[...], approximate=True)).astype(o_refs[b].dtype)
        else:
            o_refs[b][...] = h

    _round_robin([chunk(b) for b in range(bsz)])


def _lru_dir(u_a, hf, cw, cb, wg, bg, lam, geo, rev):
    w = LRU_WIDTH
    weights = [cw, cb.reshape(1, w), wg, bg.reshape(1, 2 * w), lam.reshape(1, w)]
    scratch = [pltpu.VMEM((geo["batch"], 1, w), F32)]
    return _scan_call(_lru_kernel, "rglru_bwd" if rev else "rglru_fwd", u_a, w, hf, weights, scratch, geo, rev)


def _mlstm_kernel(*refs, rev, lpb, bsz):
    n_weights = 12 if rev else 10
    batch_refs, weights, o_refs, (c_scr, m_scr) = _split_scan_refs(refs, rev, bsz, n_weights, 2)
    j = pl.program_id(0)

    @pl.when(j == 0)
    def _():
        c_scr[...] = jnp.zeros_like(c_scr)
        m_scr[...] = jnp.zeros_like(m_scr)

    _round_robin([_mlstm_chunk(batch_refs[b], weights, o_refs[b], c_scr, m_scr, b, j, rev, lpb) for b in range(bsz)])


def _round_robin(stage_generators):
    live = list(stage_generators)
    while live:
        live = [g for g in live if next(g, StopIteration) is not StopIteration]


def _mlstm_chunk(in_refs, weights, o_ref, c_scr, m_scr, b, j, rev, lpb):
    if rev:
        x_ref, xp_ref, xn_ref, z_ref, hf_ref = in_refs
        cw_ref, cb_ref, wq_ref, wk_ref, wkt_ref, wv_ref, wg_ref, wgt_ref, bg_ref, bgt_ref, ng_ref, sk_ref = weights
    else:
        x_ref, xp_ref, xn_ref = in_refs
        cw_ref, cb_ref, wq_ref, wk_ref, wkt_ref, wv_ref, wg_ref, wgt_ref, bg_ref, bgt_ref = weights
    x = x_ref[...]
    n = x.shape[0]
    xc = _short_conv(x, xp_ref, xn_ref, cw_ref, cb_ref, j, lpb, rev)
    xc = xc * _sigmoid(xc)
    xcb = xc.astype(BF16)
    yield
    q = jnp.dot(xcb, wq_ref[...], preferred_element_type=F32)
    k = jnp.dot(xcb, wk_ref[...], preferred_element_type=F32)
    yield
    kt =lax.dot_general(wkt_ref[...], xcb, NT_DIMS, preferred_element_type=F32)
    lane_w = lax.broadcasted_iota(I32, (1, ML_PAD), 1)
    ones_lane = (lane_w % LANES == ML_HEAD_DIM).astype(F32)
    v = jnp.dot(x.astype(BF16), wv_ref[...], preferred_element_type=F32) + ones_lane
    yield
    qkv =jnp.concatenate([q, k, v], axis=1).astype(BF16)
    g_col = jnp.dot(qkv, wg_ref[...], preferred_element_type=F32) + bg_ref[...]
    g_row = lax.dot_general(wgt_ref[...], qkv, NT_DIMS, preferred_element_type=F32) + bgt_ref[...]
    yield
    ti =lax.broadcasted_iota(I32, (n, n), 0)
    si = lax.broadcasted_iota(I32, (n, n), 1)
    mask = (si >= ti) if rev else (si <= ti)
    tri = mask.astype(F32)
    tri_t = ((ti >= si) if rev else (ti <= si)).astype(F32)
    b_col = jnp.dot(tri, _log_sigmoid(g_col), precision=HIGHEST, preferred_element_type=F32)
    b_row = jnp.dot(_log_sigmoid(g_row), tri_t, precision=HIGHEST, preferred_element_type=F32)
    yield
    lane =lax.broadcasted_iota(I32, (1, LANES), 1)
    num_mask = (lane < ML_HEAD_DIM).astype(F32)
    den_mask = (lane == ML_HEAD_DIM).astype(F32)
    last = 0 if rev else n - 1
    outs = []
    for h in range(ML_HEADS):
        hs = slice(h * LANES, (h + 1) * LANES)
        bc = b_col[:, ML_HEADS + h:ML_HEADS + h + 1]
        br = b_row[ML_HEADS + h:ML_HEADS + h + 1, :]
        ic = g_col[:, h:h + 1]
        ir = g_row[h:h + 1, :]
        m_prev = m_scr[b, h:h + 1, 0:1]
        dmat = jnp.where(mask, bc - br + ir, -jnp.inf)
        inter = bc + m_prev
        m_t = jnp.maximum(inter, jnp.max(dmat, axis=1, keepdims=True))
        yield
        s =jnp.dot(q[:, hs].astype(BF16), kt[hs, :].astype(BF16), preferred_element_type=F32)
        p = s * jnp.exp(dmat - m_t)
        yield
        w_inter =jnp.exp(inter - m_t)
        vh = v[:, hs]
        c_old = c_scr[b * ML_HEADS + h]
        numden = (jnp.dot(p.astype(BF16), vh.astype(BF16), preferred_element_type=F32)
                  + w_inter * jnp.dot(q[:, hs].astype(BF16), c_old.astype(BF16), preferred_element_type=F32))
        den = jnp.sum(numden * den_mask, axis=1, keepdims=True)
        hh = numden * num_mask / jnp.maximum(jnp.abs(den), jnp.exp(-m_t))
        yield
        b_last =bc[last:last + 1, :]
        ws_col = b_last - bc + ic
        m_new = jnp.maximum(b_last + m_prev, jnp.max(ws_col, axis=0, keepdims=True))
        decay = jnp.exp(b_last + m_prev - m_new)
        wv = (jnp.exp(ws_col - m_new) * vh).astype(BF16)
        c_scr[b * ML_HEADS + h] = decay * c_old + jnp.dot(kt[hs, :].astype(BF16), wv, preferred_element_type=F32)
        m_scr[b, h:h + 1, :] = jnp.broadcast_to(m_new, (1, LANES))
        yield
        if rev:
            hsum = hf_ref[:, hs] + hh
            mu = jnp.sum(hsum, axis=1, keepdims=True) * (1.0 / ML_HEAD_DIM)
            cen = (hsum - mu) * num_mask
            var = jnp.sum(cen * cen, axis=1, keepdims=True) * (1.0 / ML_HEAD_DIM)
            hh = cen * lax.rsqrt(var + EPS)
        outs.append(hh)
    hcat = jnp.concatenate(outs, axis=1)
    if rev:
        z = z_ref[...]
        o_ref[...] = ((hcat * ng_ref[...] + sk_ref[...] * xc) * (z * _sigmoid(z))).astype(o_ref.dtype)
    else:
        o_ref[...] = hcat


def _mlstm_dir(u_m, hf, wts, geo, rev):
    names = ["cw", "cb", "wq", "wk", "wkt", "wv", "wg", "wgt", "bg", "bgt"] + (["ng", "sk"] if rev else [])
    bsz = geo["batch"]
    scratch = [pltpu.VMEM((bsz * ML_HEADS, LANES, LANES), F32), pltpu.VMEM((bsz, SUBLANES, LANES), F32)]
    return _scan_call(_mlstm_kernel, "mlstm_bwd" if rev else "mlstm_fwd", u_m, ML_PAD, hf, [wts[nm] for nm in names],
                      scratch, geo, rev)


def _mla_proj_kernel(ub_ref, cos_ref, sin_ref, gq_ref, w1_ref, w2_ref, gkv_ref, wk_ref, wv_ref, q_ref, k_ref, v_ref):
    ub = ub_ref[...]
    cos = cos_ref[...]
    sin = sin_ref[...]
    qn = _rms(ub[:, :MLA_Q_RANK], gq_ref[...]).astype(BF16)
    qa = jnp.dot(qn, w1_ref[...], preferred_element_type=F32)
    qb = jnp.dot(qn, w2_ref[...], preferred_element_type=F32)
    kvn = _rms(ub[:, MLA_Q_RANK:MLA_Q_RANK + MLA_KV_RANK], gkv_ref[...]).astype(BF16)
    kn = jnp.dot(kvn, wk_ref[...], preferred_element_type=F32)
    lane = lax.broadcasted_iota(I32, (1, LANES), 1)
    ones_lane = (lane == MLA_V).astype(F32)
    vn = jnp.dot(kvn, wv_ref[...], preferred_element_type=F32)
    off = MLA_Q_RANK + MLA_KV_RANK
    kr = ub[:, off:off + LANES] * cos + ub[:, off + LANES:off + 2 * LANES] * sin
    for h in range(MLA_HEADS):
        hs = slice(h * LANES, (h + 1) * LANES)
        q_ref[0, h] = (qa[:, hs] * cos + qb[:, hs] * sin).astype(BF16)
        k_ref[0, h] = (kn[:, hs] + kr).astype(BF16)
        v_ref[0, h] = (vn[:, hs] + ones_lane).astype(BF16)


def _mla_proj(u_b, cos_t, sin_t, wts, geo):
    n_tiles = geo["t_all"] // TILE
    n_lat, lpb, bsz = geo["n_lat"], geo["lpb"], geo["batch"]
    hw = MLA_HEADS * LANES

    def batch_of(i):
        return jnp.where(i < n_lat, i // lpb, i - n_lat)

    def blk_of(i):
        return jnp.where(i < n_lat, i % lpb, lpb)

    const = lambda i: (0, 0)
    head_spec = pl.BlockSpec((1, MLA_HEADS, TILE, LANES), lambda i: (batch_of(i), 0, blk_of(i), 0))
    head_shape = jax.ShapeDtypeStruct((bsz, MLA_HEADS, (lpb + 1) * TILE, LANES), BF16)
    return pl.pallas_call(
        _mla_proj_kernel,
        grid=(n_tiles,),
        in_specs=[
            pl.BlockSpec((TILE, UB_W), lambda i: (i, 0)),
            pl.BlockSpec((TILE, LANES), lambda i: (blk_of(i), 0)),
            pl.BlockSpec((TILE, LANES), lambda i: (blk_of(i), 0)),
            pl.BlockSpec((1, MLA_Q_RANK), const),
            pl.BlockSpec((MLA_Q_RANK, hw), const),
            pl.BlockSpec((MLA_Q_RANK, hw), const),
            pl.BlockSpec((1, MLA_KV_RANK), const),
            pl.BlockSpec((MLA_KV_RANK, hw), const),
            pl.BlockSpec((MLA_KV_RANK, hw), const),
        ],
        out_specs=[head_spec, head_spec, head_spec],
        out_shape=[head_shape, head_shape, head_shape],
        compiler_params=_params("arbitrary"),
        name="mla_proj",
    )(u_b, cos_t, sin_t, wts["gq"], wts["w1"], wts["w2"], wts["gkv"], wts["wk"], wts["wv"])


def _flash_kernel(q_ref, k_ref, v_ref, o_ref, m_scr, acc_scr, *, k_start, nk, tk):
    lane = lax.broadcasted_iota(I32, (1, LANES), 1)
    den_mask = (lane == MLA_V).astype(F32)
    outs = []
    for j in range(2):
        q = q_ref[0, j]
        m_scr[...] = jnp.full_like(m_scr, -jnp.inf)
        acc_scr[...] = jnp.zeros_like(acc_scr)

        def body(i, carry, j=j, q=q):
            start = k_start + i * tk
            kk = k_ref[0, j, start:start + tk, :]
            vv = v_ref[0, j, start:start + tk, :]
            s = lax.dot_general(q, kk, NT_DIMS, preferred_element_type=F32)
            cols = [s[:, c * LANES:(c + 1) * LANES] for c in range(tk // LANES)]
            mp = cols[0]
            for sc in cols[1:]:
                mp = jnp.maximum(mp, sc)
            m_old = m_scr[...]
            m_new = jnp.maximum(m_old, jnp.broadcast_to(jnp.max(mp, axis=1, keepdims=True), mp.shape))
            p = jnp.concatenate([jnp.exp2(sc - m_new).astype(BF16) for sc in cols], axis=1)
            acc_scr[...] = jnp.exp2(m_old - m_new) * acc_scr[...] + jnp.dot(p, vv, preferred_element_type=F32)
            m_scr[...] = m_new
            return carry

        for i in range(nk):
            body(i, 0)
        acc = acc_scr[...]
        den = jnp.sum(acc * den_mask, axis=1, keepdims=True)
        outs.append(acc / den)
    o_ref[...] = jnp.where(lane < MLA_V, outs[0], pltpu.roll(outs[1], MLA_V, 1)).astype(o_ref.dtype)


def _key_tile(n):
    for cand in range(min(n, 1024) // LANES * LANES, 0, -LANES):
        if n % cand == 0:
            return cand
    raise ValueError(n)


def _flash(q, k, v, tq, q_blk0, nq, k_start, k_len, out_blk0, t_all):
    bsz, heads, rows, _ = q.shape
    tk = _key_tile(k_len)
    kern = functools.partial(_flash_kernel, k_start=k_start, nk=k_len // tk, tk=tk)
    return pl.pallas_call(
        kern,
        grid=(bsz, heads // 2, nq),
        in_specs=[
            pl.BlockSpec((1, 2, tq, LANES), lambda b, h, i: (b, h, q_blk0 + i, 0)),
            pl.BlockSpec((1, 2, rows, LANES), lambda b, h, i: (b, h, 0, 0)),
            pl.BlockSpec((1, 2, rows, LANES), lambda b, h, i: (b, h, 0, 0)),
        ],
        out_specs=pl.BlockSpec((tq, LANES), lambda b, h, i: (out_blk0 + b * nq + i, h)),
        out_shape=jax.ShapeDtypeStruct((t_all, heads * MLA_V), BF16),
        scratch_shapes=[pltpu.VMEM((tq, LANES), F32), pltpu.VMEM((tq, LANES), F32)],
        compiler_params=_params("arbitrary", "arbitrary", "arbitrary"),
        name="mla_attention",
    )(q, k, v)


def _out_kernel(ya_ref, yb_ref, yc_ref, x_ref, mod_ref, w_ref, g_ref, wr_ref, br_ref,
                xo_ref, h2_ref, te_ref, tg_ref, rk_ref, cnt_ref, *, n_lat, lat_per_batch, ctx_row):
    i = pl.program_id(0)

    @pl.when(i == 0)
    def _():
        cnt_ref[...] = jnp.zeros_like(cnt_ref)

    r = _mod_row(i, n_lat, lat_per_batch, ctx_row)
    g1 = mod_ref[2, pl.ds(r, 1), :]
    sh2 = mod_ref[3, pl.ds(r, 1), :]
    sc2 = mod_ref[4, pl.ds(r, 1), :]
    y = jnp.concatenate([ya_ref[...], yb_ref[...], yc_ref[...]], axis=1)
    x1 = x_ref[...] + g1 * jnp.dot(y, w_ref[...], preferred_element_type=F32)
    xo_ref[...] = x1
    h2 = _rms(x1, g_ref[...]) * (1.0 + sc2) + sh2
    h2_ref[...] = h2.reshape(h2_ref.shape)
    logits = jnp.dot(h2, wr_ref[...], precision=HIGHEST, preferred_element_type=F32) + br_ref[...]
    n, ne = logits.shape
    lane_e = lax.broadcasted_iota(I32, (n, ne), 1).astype(F32)
    lane_o = lax.broadcasted_iota(I32, (n, LANES), 1)
    vals, idxs = [], []
    for _ in range(TOP_K):
        m = jnp.max(logits, axis=1, keepdims=True)
        idx = jnp.min(jnp.where(logits == m, lane_e, float(ne)), axis=1, keepdims=True)
        logits = jnp.where(lane_e == idx, -jnp.inf, logits)
        vals.append(m)
        idxs.append(idx)
    exps = [jnp.exp(vv - vals[0]) for vv in vals]
    tot = exps[0] + exps[1] + exps[2] + exps[3]
    hits = [(lane_e == idx).astype(F32) for idx in idxs]
    chosen = hits[0] + hits[1] + hits[2] + hits[3]
    ti = lax.broadcasted_iota(I32, (n, n), 0)
    si = lax.broadcasted_iota(I32, (n, n), 1)
    before = jnp.dot((si < ti).astype(BF16), chosen.astype(BF16), preferred_element_type=F32) + cnt_ref[0:1, :]
    cnt_ref[...] = cnt_ref[...] + jnp.sum(chosen, axis=0, keepdims=True)
    te = jnp.zeros((n, LANES), F32)
    tg = jnp.zeros((n, LANES), F32)
    rk = jnp.zeros((n, LANES), F32)
    for kk in range(TOP_K):
        te = jnp.where(lane_o == kk, idxs[kk], te)
        tg = jnp.where(lane_o == kk, exps[kk] / tot, tg)
        rk = jnp.where(lane_o == kk, jnp.sum(hits[kk] * before, axis=1, keepdims=True), rk)
    te_ref[...] = te.astype(I32)
    tg_ref[...] = tg
    rk_ref[...] = rk.astype(I32)


def _out_proj(ya, yb, yc, x_all, mod_l, w_out_p, g, w_router, b_router, geo, n_tiles):
    d = x_all.shape[1]
    rows = n_tiles * TILE
    kern = functools.partial(_out_kernel, n_lat=geo["n_lat"], lat_per_batch=geo["lpb"], ctx_row=geo["ctx_row"])
    row_blk = lambda i: (i, 0)
    const = lambda i: (0, 0)
    w_router = jnp.pad(w_router, ((0, 0), (0, LANES - N_EXPERTS)))
    b_router = jnp.pad(b_router, (0, LANES - N_EXPERTS), constant_values=-jnp.inf)
    return pl.pallas_call(
        kern,
        grid=(n_tiles,),
        in_specs=[
            pl.BlockSpec((TILE, ya.shape[1]), row_blk),
            pl.BlockSpec((TILE, yb.shape[1]), row_blk),
            pl.BlockSpec((TILE, yc.shape[1]), row_blk),
            pl.BlockSpec((TILE, d), row_blk),
            pl.BlockSpec(mod_l.shape, lambda i: (0, 0, 0)),
            pl.BlockSpec(w_out_p.shape, const),
            pl.BlockSpec((1, d), const),
            pl.BlockSpec(w_router.shape, const),
            pl.BlockSpec((1, LANES), const),
        ],
        out_specs=[pl.BlockSpec((TILE, d), row_blk), pl.BlockSpec((TILE, SUBLANES, d // SUBLANES), lambda i: (i, 0, 0)),
                   pl.BlockSpec((TILE, LANES), row_blk), pl.BlockSpec((TILE, LANES), row_blk),
                   pl.BlockSpec((TILE, LANES), row_blk), pl.BlockSpec((SUBLANES, LANES), const)],
        out_shape=[jax.ShapeDtypeStruct((rows, d), F32), jax.ShapeDtypeStruct((rows, SUBLANES, d // SUBLANES), F32),
                   jax.ShapeDtypeStruct((rows, LANES), I32), jax.ShapeDtypeStruct((rows, LANES), F32),
                   jax.ShapeDtypeStruct((rows, LANES), I32), jax.ShapeDtypeStruct((SUBLANES, LANES), F32)],
        compiler_params=_params("arbitrary"),
        name="out_proj_router",
    )(ya, yb, yc, x_all, mod_l, w_out_p, g.reshape(1, d), w_router, b_router.reshape(1, LANES))


def _dispatch_kernel(zrow_ref, nu_ref, dest_ref, h_ref, xs_hbm, zbuf, zsem, sem):
    i = pl.program_id(0)
    n_tok = h_ref.shape[0]
    n_blocks = xs_hbm.shape[0] // MOE_BLOCK

    def zero_copy(blk_row):
        row = pl.multiple_of(blk_row, MOE_BLOCK)
        return pltpu.make_async_copy(zbuf, xs_hbm.at[pl.ds(row, MOE_BLOCK)], zsem)

    @pl.when(i == 0)
    def _():
        zbuf[...] = jnp.zeros_like(zbuf)
        for start in (True, False):
            for e in range(N_EXPERTS):
                @pl.when(zrow_ref[e] >= 0)
                def _(e=e, start=start):
                    zero_copy(zrow_ref[e]).start() if start else zero_copy(zrow_ref[e]).wait()

            def tail(b, carry, start=start):
                zero_copy(b * MOE_BLOCK).start() if start else zero_copy(b * MOE_BLOCK).wait()
                return carry
            lax.fori_loop(nu_ref[0], n_blocks, tail, 0)

    for kk in range(TOP_K):
        def issue(t, carry, kk=kk):
            pltpu.make_async_copy(h_ref.at[t], xs_hbm.at[dest_ref[0, 0, kk * n_tok + t]], sem).start()
            return carry
        lax.fori_loop(0, n_tok, issue, 0, unroll=16)
    for _ in range(TOP_K):
        pltpu.make_async_copy(h_ref, xs_hbm.at[pl.ds(0, n_tok)], sem).wait()


def _moe_dispatch(h2, dest, zrow, n_used, n_slots):
    t_moe, tile_shape = dest.shape[0], h2.shape[1:]
    n_tiles = t_moe // TILE
    dest3 = dest.reshape(n_tiles, TILE, TOP_K).transpose(0, 2, 1).reshape(n_tiles, 1, TOP_K * TILE)
    grid_spec = pltpu.PrefetchScalarGridSpec(
        num_scalar_prefetch=2,
        grid=(n_tiles,),
        in_specs=[
            pl.BlockSpec((1, 1, TOP_K * TILE), lambda i, zr, nu: (i, 0, 0), memory_space=pltpu.SMEM),
            pl.BlockSpec((TILE,) + tile_shape, lambda i, zr, nu: (i, 0, 0)),
        ],
        out_specs=pl.BlockSpec(memory_space=pl.ANY),
        scratch_shapes=[pltpu.VMEM((MOE_BLOCK,) + tile_shape, F32), pltpu.SemaphoreType.DMA, pltpu.SemaphoreType.DMA],
    )
    return pl.pallas_call(
        _dispatch_kernel,
        grid_spec=grid_spec,
        out_shape=jax.ShapeDtypeStruct((n_slots,) + tile_shape, F32),
        compiler_params=_params("arbitrary"),
        name="moe_dispatch",
    )(zrow, n_used, dest3, h2)


def _moe_kernel(be_ref, nu_ref, x_ref, wgu_ref, bgu_ref, wd_ref, bd_ref, y_ref, wgu_bf, wd_bf):
    i = pl.program_id(0)

    @pl.when(i < nu_ref[0])
    def _():
        prev_e = be_ref[jnp.maximum(i - 1, 0)]

        @pl.when(jnp.logical_or(i == 0, be_ref[i] != prev_e))
        def _():
            wgu_bf[...] = wgu_ref[0, 0].astype(BF16)
            wd_bf[...] = wd_ref[0, 0].astype(BF16)

        x = x_ref[...].reshape(x_ref.shape[0], -1).astype(BF16)
        gu = jnp.dot(x, wgu_bf[...], preferred_element_type=F32) + bgu_ref[0, 0]
        glu = jnp.minimum(gu[:, :D_EXPERT], SWIGLU_LIMIT)
        lin = jnp.clip(gu[:, D_EXPERT:], -SWIGLU_LIMIT, SWIGLU_LIMIT)
        act = glu * _sigmoid(SWIGLU_ALPHA * glu) * (lin + 1.0)
        y = jnp.dot(act.astype(BF16), wd_bf[...], preferred_element_type=F32) + bd_ref[0, 0]
        y_ref[...] = y.reshape(y_ref.shape)

    @pl.when(i >= nu_ref[0])
    def _():
        y_ref[...] = jnp.zeros_like(y_ref)


def _moe_experts(x_sorted, block_e, n_used, w_gu, b_gu, w_down, b_down, layer):
    n_slots, tile_shape = x_sorted.shape[0], x_sorted.shape[1:]
    n_blocks = n_slots // MOE_BLOCK
    depth, d = w_gu.shape[0], w_gu.shape[2]
    slot_blk = pl.BlockSpec((MOE_BLOCK,) + tile_shape, lambda i, be, nu: (i, 0, 0))

    def expert(i, be, nu):
        return (layer, be[jnp.minimum(i, nu[0] - 1)], 0, 0)

    grid_spec = pltpu.PrefetchScalarGridSpec(
        num_scalar_prefetch=2,
        grid=(n_blocks,),
        in_specs=[
            slot_blk,
            pl.BlockSpec((1, 1, d, 2 * D_EXPERT), expert),
            pl.BlockSpec((1, 1, 1, 2 * D_EXPERT), expert),
            pl.BlockSpec((1, 1, D_EXPERT, d), expert),
            pl.BlockSpec((1, 1, 1, d), expert),
        ],
        out_specs=slot_blk,
        scratch_shapes=[pltpu.VMEM((d, 2 * D_EXPERT), BF16), pltpu.VMEM((D_EXPERT, d), BF16)],
    )
    return pl.pallas_call(
        _moe_kernel,
        grid_spec=grid_spec,
        out_shape=jax.ShapeDtypeStruct((n_slots,) + tile_shape, F32),
        compiler_params=_params("arbitrary"),
        name="moe_experts",
    )(block_e, n_used, x_sorted, w_gu, b_gu.reshape(depth, N_EXPERTS, 1, -1), w_down,
      b_down.reshape(depth, N_EXPERTS, 1, -1))


def _comb_kernel(inv_ref, inv_next_ref, y_hbm, tg_ref, x_ref, mod_ref, fg_ref, o_ref, cbuf, sem,
                 *, n_lat, lat_per_batch, ctx_row, final):
    i = pl.program_id(0)
    n_steps = pl.num_programs(0)
    slot = i % 2
    rows = cbuf.shape[1]
    per_tile = TILE // COMB_TILE

    def issue(idx_ref, dst_slot):
        def body(r, carry):
            pltpu.make_async_copy(y_hbm.at[idx_ref[0, 0, r]], cbuf.at[dst_slot, r], sem.at[dst_slot]).start()
            return carry
        lax.fori_loop(0, rows, body, 0, unroll=16)

    @pl.when(i == 0)
    def _():
        issue(inv_ref, 0)

    @pl.when(i + 1 < n_steps)
    def _():
        issue(inv_next_ref, 1 - slot)

    pltpu.make_async_copy(y_hbm.at[pl.ds(0, rows)], cbuf.at[slot], sem.at[slot]).wait()

    r = _mod_row(i // per_tile, n_lat, lat_per_batch, ctx_row)
    g2 = mod_ref[5, pl.ds(r, 1), :]
    tg = tg_ref[...]
    f = None
    for kk in range(TOP_K):
        rows_k = cbuf[slot, kk * COMB_TILE:(kk + 1) * COMB_TILE].reshape(COMB_TILE, -1)
        f = tg[:, kk:kk + 1] * rows_k if f is None else f + tg[:, kk:kk + 1] * rows_k
    x2 = x_ref[...] + g2 * f
    o_ref[...] = _rms(x2, fg_ref[...]) if final else x2


def _moe_combine(y_sorted, dest, tg, x_all, mod_l, final_g, geo, final):
    t_moe = dest.shape[0]
    d = x_all.shape[1]
    n_steps = t_moe // COMB_TILE
    inv3 = dest.reshape(n_steps, COMB_TILE, TOP_K).transpose(0, 2, 1).reshape(n_steps, 1, TOP_K * COMB_TILE)
    kern = functools.partial(_comb_kernel, n_lat=geo["n_lat"], lat_per_batch=geo["lpb"], ctx_row=geo["ctx_row"],
                             final=final)
    return pl.pallas_call(
        kern,
        grid=(n_steps,),
        in_specs=[
            pl.BlockSpec((1, 1, TOP_K * COMB_TILE), lambda i: (i, 0, 0), memory_space=pltpu.SMEM),
            pl.BlockSpec((1, 1, TOP_K * COMB_TILE), lambda i: (jnp.minimum(i + 1, n_steps - 1), 0, 0),
                         memory_space=pltpu.SMEM),
            pl.BlockSpec(memory_space=pl.ANY),
            pl.BlockSpec((COMB_TILE, LANES), lambda i: (i, 0)),
            pl.BlockSpec((COMB_TILE, d), lambda i: (i, 0)),
            pl.BlockSpec(mod_l.shape, lambda i: (0, 0, 0)),
            pl.BlockSpec((1, d), lambda i: (0, 0)),
        ],
        out_specs=pl.BlockSpec((COMB_TILE, d), lambda i: (i, 0)),
        out_shape=jax.ShapeDtypeStruct((t_moe, d), F32),
        scratch_shapes=[pltpu.VMEM((2, TOP_K * COMB_TILE) + y_sorted.shape[1:], F32), pltpu.SemaphoreType.DMA((2,))],
        compiler_params=_params("arbitrary"),
        name="moe_combine",
    )(inv3, inv3, y_sorted, tg, x_all, mod_l, final_g.reshape(1, d))


def _route(te, rk, cnt, t_moe):
    counts = cnt[0, :N_EXPERTS].astype(I32)
    padded = (counts + MOE_BLOCK - 1) // MOE_BLOCK * MOE_BLOCK
    padded_end = jnp.cumsum(padded)
    padded_start = padded_end - padded
    experts = jnp.arange(N_EXPERTS, dtype=I32)
    e_tok = te[:t_moe, :TOP_K]
    dest = rk[:t_moe, :TOP_K] + jnp.sum(jnp.where(e_tok[..., None] == experts, padded_start, 0), axis=-1)
    n_blocks = -(-(t_moe * TOP_K) // MOE_BLOCK) + N_EXPERTS
    blk_start = jnp.arange(n_blocks, dtype=I32) * MOE_BLOCK
    block_e = jnp.minimum(jnp.sum((padded_end[None, :] <= blk_start[:, None]).astype(I32), axis=1), N_EXPERTS - 1)
    n_used = (padded_end[-1:] // MOE_BLOCK).astype(I32)
    zrow = jnp.where(counts > 0, padded_end - MOE_BLOCK, -1).astype(I32)
    return dest.astype(I32), block_e, n_used, zrow, n_blocks * MOE_BLOCK


def _block_diag_dense(w):
    g, i, j = w.shape
    out = jnp.zeros((g * i, g * j), w.dtype)
    for n in range(g):
        out = out.at[n * i:(n + 1) * i, n * j:(n + 1) * j].set(w[n])
    return out


def _pad_heads(w, heads, axis):
    shape = list(w.shape)
    shape[axis:axis + 1] = [heads, shape[axis] // heads]
    w = w.reshape(shape)
    pad = [(0, 0)] * w.ndim
    pad[axis + 1] = (0, LANES - shape[axis + 1])
    w = jnp.pad(w, pad)
    shape[axis:axis + 2] = [heads * LANES]
    return w.reshape(shape)


_ROPE_SWAP = np.concatenate([np.arange(8, 16), np.arange(0, 8), np.arange(24, 32), np.arange(16, 24)])


def _prep_in_weight(w_in):
    d = w_in.shape[0]
    o = np.cumsum([0, LRU_WIDTH, LRU_WIDTH, MLA_Q_RANK, MLA_KV_RANK, MLA_ROPE, ML_WIDTH, ML_WIDTH])
    a_xg = w_in[:, o[0]:o[2]]
    b_qkv = w_in[:, o[2]:o[4]]
    kr = w_in[:, o[4]:o[5]]
    z_nope = jnp.zeros((d, MLA_NOPE), w_in.dtype)
    z_tail = jnp.zeros((d, LANES - MLA_NOPE - MLA_ROPE), w_in.dtype)
    m_x = _pad_heads(w_in[:, o[5]:o[6]], ML_HEADS, 1)
    m_z = _pad_heads(w_in[:, o[6]:o[7]], ML_HEADS, 1)
    return jnp.concatenate([a_xg, b_qkv, z_nope, kr, z_tail, z_nope, kr[:, _ROPE_SWAP], z_tail, m_x, m_z],
                           axis=1).astype(BF16)


def _prep_out_weight(w_out):
    a = w_out[:LRU_WIDTH]
    b = w_out[LRU_WIDTH:LRU_WIDTH + MLA_HEADS * MLA_V]
    c = _pad_heads(w_out[LRU_WIDTH + MLA_HEADS * MLA_V:], ML_HEADS, 0)
    return jnp.concatenate([a, b, c], axis=0).astype(BF16)


def _prep_mla(q_norm_g, w_qb, kv_norm_g, w_kvb):
    scale = (MLA_NOPE + MLA_ROPE) ** -0.5 * math.log2(math.e)
    rq = w_qb.shape[0]
    wq = w_qb.reshape(rq, MLA_HEADS, MLA_NOPE + MLA_ROPE) * scale
    nope, rope = wq[..., :MLA_NOPE], wq[..., MLA_NOPE:]
    z32 = jnp.zeros((rq, MLA_HEADS, LANES - MLA_NOPE - MLA_ROPE), w_qb.dtype)
    w1 = jnp.concatenate([nope, rope, z32], axis=-1).reshape(rq, MLA_HEADS * LANES)
    w2 = jnp.concatenate([jnp.zeros_like(nope), rope[..., _ROPE_SWAP], z32], axis=-1).reshape(rq, MLA_HEADS * LANES)
    rk = w_kvb.shape[0]
    wkv = w_kvb.reshape(rk, MLA_HEADS, MLA_NOPE + MLA_V)
    z64 = jnp.zeros((rk, MLA_HEADS, LANES - MLA_NOPE), w_kvb.dtype)
    wk = jnp.concatenate([wkv[..., :MLA_NOPE], z64], axis=-1).reshape(rk, MLA_HEADS * LANES)
    wv = jnp.concatenate([wkv[..., MLA_NOPE:], z64], axis=-1).reshape(rk, MLA_HEADS * LANES)
    return {"gq": q_norm_g.reshape(1, -1), "w1": w1.astype(BF16), "w2": w2.astype(BF16),
            "gkv": kv_norm_g.reshape(1, -1), "wk": wk.astype(BF16), "wv": wv.astype(BF16)}


def _prep_mlstm(conv_w, conv_b, wq, wk, wv, w_gate_d, b_gate_d, norm_g, skip):
    def proj(w):
        return _pad_heads(_pad_heads(_block_diag_dense(w), ML_HEADS, 0), ML_HEADS, 1)

    wk_p = proj(wk) * (ML_HEAD_DIM ** -0.5)
    wg = jnp.concatenate([_pad_heads(w_gate_d[i * ML_WIDTH:(i + 1) * ML_WIDTH], ML_HEADS, 0) for i in range(3)], axis=0)
    ng = w_gate_d.shape[1]
    wg = jnp.pad(wg, ((0, 0), (0, LANES - ng)))
    b_gate_d = jnp.pad(b_gate_d, (0, LANES - ng))
    gate_rows = 2 * SUBLANES
    return {
        "cw": _pad_heads(conv_w, ML_HEADS, 1), "cb": _pad_heads(conv_b.reshape(1, -1), ML_HEADS, 1),
        "wq": proj(wq).astype(BF16), "wk": wk_p.astype(BF16), "wkt": wk_p.T.astype(BF16), "wv": proj(wv).astype(BF16),
        "wg": wg.astype(BF16), "wgt": wg.T[:gate_rows].astype(BF16),
        "bg": b_gate_d.reshape(1, -1), "bgt": b_gate_d[:gate_rows].reshape(-1, 1),
        "ng": _pad_heads(norm_g.reshape(1, -1), ML_HEADS, 1), "sk": _pad_heads(skip.reshape(1, -1), ML_HEADS, 1),
    }


def _rope_tables(seq, ctx_len):
    rows = seq // GRID_W
    row = jnp.repeat(jnp.arange(rows, dtype=I32), GRID_W)
    col = jnp.tile(jnp.arange(GRID_W, dtype=I32), rows)
    freqs = ROPE_BASE ** (-jnp.arange(ROPE_AXIS_FREQ, dtype=F32) / ROPE_AXIS_FREQ)
    ang_r, ang_c = row[:, None] * freqs, col[:, None] * freqs
    cos32 = jnp.concatenate([jnp.cos(ang_r), jnp.cos(ang_r), jnp.cos(ang_c), jnp.cos(ang_c)], axis=1)
    sin32 = jnp.concatenate([-jnp.sin(ang_r), jnp.sin(ang_r), -jnp.sin(ang_c), jnp.sin(ang_c)], axis=1)
    cos32 = jnp.concatenate([cos32, jnp.ones((ctx_len, MLA_ROPE), F32)], axis=0)
    sin32 = jnp.concatenate([sin32, jnp.zeros((ctx_len, MLA_ROPE), F32)], axis=0)
    n = seq + ctx_len
    tail = jnp.zeros((n, LANES - MLA_NOPE - MLA_ROPE), F32)
    cos_t = jnp.concatenate([jnp.ones((n, MLA_NOPE), F32), cos32, tail], axis=1)
    sin_t = jnp.concatenate([jnp.zeros((n, MLA_NOPE), F32), sin32, tail], axis=1)
    return cos_t, sin_t


def kernel(x, c, ctx, c_ctx, norm1_g, norm2_g, w_mod, b_mod, w_in, w_out, lru_conv_w, lru_conv_b, lru_wa, lru_ba,
           lru_wx, lru_bx, lru_lambda, mla_q_norm_g, mla_w_qb, mla_kv_norm_g, mla_w_kvb, ml_conv_w, ml_conv_b,
           ml_wq, ml_wk, ml_wv, ml_w_gate, ml_b_gate, ml_norm_g, ml_skip, w_router, b_router, w_gu, b_gu, w_down,
           b_down, final_g):
    bsz, seq, d = x.shape
    ctx_len = ctx.shape[1]
    depth = w_mod.shape[0]
    assert ctx_len == TILE and seq % (2 * TILE) == 0 and bsz + 1 <= SUBLANES
    t_lat = bsz * seq
    t_all = t_lat + bsz * ctx_len
    geo = {"batch": bsz, "lpb": seq // TILE, "n_lat": t_lat // TILE, "ctx_row": bsz, "t_all": t_all}

    cv = jnp.zeros((SUBLANES, d), F32).at[:bsz].set(c).at[bsz].set(c_ctx)
    mod = _modulation(cv, w_mod, b_mod)
    cos_t, sin_t = _rope_tables(seq, ctx_len)
    x_all = jnp.concatenate([x.reshape(t_lat, d), ctx.reshape(bsz * ctx_len, d)], axis=0)

    out = None
    for l in range(depth):
        last = l == depth - 1
        u_a, u_b, u_m = _in_proj(x_all, mod[l], norm1_g[l], _prep_in_weight(w_in[l]), geo)

        hf = None
        for dd in range(2):
            wg = jnp.concatenate([_block_diag_dense(lru_wa[l, dd]), _block_diag_dense(lru_wx[l, dd])], axis=1)
            bg = jnp.concatenate([lru_ba[l, dd], lru_bx[l, dd]])
            hf = _lru_dir(u_a, hf, lru_conv_w[l], lru_conv_b[l], wg.astype(BF16), bg, lru_lambda[l, dd], geo, dd == 1)
        ya = _to_token_layout(hf, seq)

        q, k, v = _mla_proj(u_b, cos_t, sin_t, _prep_mla(mla_q_norm_g[l], mla_w_qb[l], mla_kv_norm_g[l], mla_w_kvb[l]), geo)
        tq = 2 * TILE
        yb = _flash(q, k, v, tq, 0, seq // tq, 0, seq + ctx_len, 0, t_lat)
        if not last:
            yb_c = _flash(q, k, v, TILE, seq // TILE, 1, seq, ctx_len, 0, bsz * ctx_len)
            yb = jnp.concatenate([yb, yb_c], axis=0)

        hf = None
        for dd in range(2):
            wts = _prep_mlstm(ml_conv_w[l], ml_conv_b[l], ml_wq[l], ml_wk[l], ml_wv[l], ml_w_gate[l, dd],
                              ml_b_gate[l, dd], ml_norm_g[l], ml_skip[l])
            hf = _mlstm_dir(u_m, hf, wts, geo, dd == 1)
        yc = _to_token_layout(hf, seq)

        n_tiles = (t_lat if last else t_all) // TILE
        x_mid, h2, te, tg, rk, cnt = _out_proj(ya, yb, yc, x_all, mod[l], _prep_out_weight(w_out[l]), norm2_g[l],
                                               w_router[l], b_router[l], geo, n_tiles)
        t_moe = n_tiles * TILE
        dest, block_e, n_used, zrow, n_slots = _route(te, rk, cnt, t_moe)
        x_sorted = _moe_dispatch(h2, dest, zrow, n_used, n_slots)
        y_sorted = _moe_experts(x_sorted, block_e, n_used, w_gu, b_gu, w_down, b_down, l)
        x_all = _moe_combine(y_sorted, dest, tg, x_mid, mod[l], final_g, geo, last)
        if last:
            out = x_all.reshape(bsz, seq, d)
    return out
```

```python
import functools
import math

import jax
import jax.numpy as jnp
import numpy as np
from jax import lax
from jax.experimental import pallas as pl
from jax.experimental.pallas import tpu as pltpu

F32 = jnp.float32
BF16 = jnp.bfloat16
I32 = jnp.int32
HIGHEST = lax.Precision.HIGHEST

LANES = 128
SUBLANES = 8
VMEM_LIMIT_BYTES = 56 * 1024 * 1024

GRID_W = 64
EPS = 1e-6
LRU_WIDTH = 256
LRU_C = 8.0
CONV_W = 4
MLA_HEADS = 8
MLA_NOPE = 64
MLA_ROPE = 32
MLA_V = 64
MLA_Q_RANK = 256
MLA_KV_RANK = 128
ROPE_AXIS_FREQ = MLA_ROPE // 4
ROPE_BASE = 10000.0
ML_HEADS = 4
ML_HEAD_DIM = 64
ML_WIDTH = ML_HEADS * ML_HEAD_DIM
ML_PAD = ML_HEADS * LANES
N_EXPERTS = 32
TOP_K = 4
D_EXPERT = 1024
SWIGLU_LIMIT = 7.0
SWIGLU_ALPHA = 1.702
MOE_BLOCK = 256

TILE = 256
HALO = SUBLANES
UB_W = MLA_Q_RANK + MLA_KV_RANK + 2 * LANES
COMB_TILE = 128
ATTN_Q_TILE = 1024

NT_DIMS = (((1,), (1,)), ((), ()))


def _params(*sem):
    return pltpu.CompilerParams(dimension_semantics=sem, vmem_limit_bytes=VMEM_LIMIT_BYTES)


def _sigmoid(x):
    return 1.0 / (1.0 + jnp.exp(-x))


def _log_sigmoid(x):
    return jnp.minimum(x, 0.0) - jnp.log1p(jnp.exp(-jnp.abs(x)))


def _rms(x, g):
    return x * lax.rsqrt(jnp.mean(x * x, axis=-1, keepdims=True) + EPS) * g


def _mod_kernel(cv_ref, w_ref, b_ref, o_ref):
    cv = cv_ref[...]
    a = cv * _sigmoid(cv)
    o_ref[0, 0] = jnp.dot(a, w_ref[0], precision=HIGHEST, preferred_element_type=F32) + b_ref[0, 0]


def _modulation(cv, w_mod, b_mod):
    depth, d, _ = w_mod.shape
    rows = cv.shape[0]
    return pl.pallas_call(
        _mod_kernel,
        grid=(depth, 6),
        in_specs=[
            pl.BlockSpec((rows, d), lambda l, j: (0, 0)),
            pl.BlockSpec((1, d, d), lambda l, j: (l, 0, j)),
            pl.BlockSpec((1, 1, 1, d), lambda l, j: (l, j, 0, 0)),
        ],
        out_specs=pl.BlockSpec((1, 1, rows, d), lambda l, j: (l, j, 0, 0)),
        out_shape=jax.ShapeDtypeStruct((depth, 6, rows, d), F32),
        compiler_params=_params("arbitrary", "arbitrary"),
        name="modulation",
    )(cv, w_mod, b_mod.reshape(depth, 6, 1, d))


def _mod_row(i, n_lat, lat_per_batch, ctx_row):
    return jnp.where(i < n_lat, i // lat_per_batch, ctx_row)


def _in_kernel(x_ref, mod_ref, g_ref, w_ref, ua_ref, ub_ref, um_ref, *, n_lat, lat_per_batch, ctx_row):
    r = _mod_row(pl.program_id(0), n_lat, lat_per_batch, ctx_row)
    sh = mod_ref[0, pl.ds(r, 1), :]
    sc = mod_ref[1, pl.ds(r, 1), :]
    h = _rms(x_ref[...], g_ref[...]) * (1.0 + sc) + sh
    u = jnp.dot(h.astype(BF16), w_ref[...], preferred_element_type=F32)
    wa = ua_ref.shape[1]
    wb = ub_ref.shape[1]
    ua_ref[...] = u[:, :wa]
    ub_ref[...] = u[:, wa:wa + wb]
    um_ref[...] = u[:, wa + wb:]


def _in_proj(x_all, mod_l, g, w_in_p, geo):
    t_all, d = x_all.shape
    n_tiles = t_all // TILE
    wa, wb, wm = 2 * LRU_WIDTH, UB_W, 2 * ML_PAD
    kern = functools.partial(_in_kernel, n_lat=geo["n_lat"], lat_per_batch=geo["lpb"], ctx_row=geo["ctx_row"])
    return pl.pallas_call(
        kern,
        grid=(n_tiles,),
        in_specs=[
            pl.BlockSpec((TILE, d), lambda i: (i, 0)),
            pl.BlockSpec(mod_l.shape, lambda i: (0, 0, 0)),
            pl.BlockSpec((1, d), lambda i: (0, 0)),
            pl.BlockSpec(w_in_p.shape, lambda i: (0, 0)),
        ],
        out_specs=[
            pl.BlockSpec((TILE, wa), lambda i: (i, 0)),
            pl.BlockSpec((TILE, wb), lambda i: (i, 0)),
            pl.BlockSpec((TILE, wm), lambda i: (i, 0)),
        ],
        out_shape=[
            jax.ShapeDtypeStruct((t_all, wa), F32),
            jax.ShapeDtypeStruct((t_all, wb), F32),
            jax.ShapeDtypeStruct((t_all, wm), F32),
        ],
        compiler_params=_params("arbitrary"),
        name="in_proj",
    )(x_all, mod_l, g.reshape(1, d), w_in_p)


def _chunk_block(b, j, geo, rev):
    lat = (geo["lpb"] - j) if rev else (j - 1)
    return jnp.where(j == 0, geo["n_lat"] + b, b * geo["lpb"] + lat)


def _local_block(j, lpb, rev):
    return jnp.where(j == 0, lpb, (lpb - j) if rev else (j - 1))


def _chunk_specs(width, col, geo, rev, b):
    per = TILE // HALO
    last = geo["t_all"] // HALO - 1

    def cur(j):
        return (_chunk_block(b, j, geo, rev), col)

    def prev(j):
        return (jnp.maximum(_chunk_block(b, j, geo, rev) * per - 1, 0), col)

    def nxt(j):
        return (jnp.minimum((_chunk_block(b, j, geo, rev) + 1) * per, last), col)

    return [pl.BlockSpec((TILE, width), cur), pl.BlockSpec((HALO, width), prev), pl.BlockSpec((HALO, width), nxt)]


def _scan_call(kern, name, src, width, hf, weights, scratch, geo, rev):
    bsz, lpb = geo["batch"], geo["lpb"]
    chunk_of_all = pl.BlockSpec((bsz, TILE, width), lambda j: (0, _local_block(j, lpb, rev), 0))
    specs, args = [], []
    for b in range(bsz):
        specs += _chunk_specs(width, 0, geo, rev, b)
        args += [src, src, src]
        if rev:
            specs.append(pl.BlockSpec((TILE, width), lambda j, b=b: (_chunk_block(b, j, geo, rev), 1)))
            args.append(src)
    if rev:
        specs.append(chunk_of_all)
        args.append(hf)
    for wgt in weights:
        specs.append(pl.BlockSpec(wgt.shape, lambda j: (0, 0)))
        args.append(wgt)
    return pl.pallas_call(
        functools.partial(kern, rev=rev, lpb=lpb, bsz=bsz),
        grid=(lpb + 1,),
        in_specs=specs,
        out_specs=chunk_of_all,
        out_shape=jax.ShapeDtypeStruct((bsz, (lpb + 1) * TILE, width), BF16 if rev else F32),
        scratch_shapes=scratch,
        compiler_params=_params("arbitrary"),
        name=name,
    )(*args)


def _split_scan_refs(refs, rev, bsz, n_weights, n_scratch):
    n_in = 4 if rev else 3
    batch_refs = [refs[b * n_in:(b + 1) * n_in] for b in range(bsz)]
    w0 = bsz * n_in
    hf_ref = refs[w0] if rev else None
    w0 += 1 if rev else 0
    weights = refs[w0:w0 + n_weights]
    return batch_refs, hf_ref, weights, refs[w0 + n_weights], refs[len(refs) - n_scratch:]


def _short_conv(x, xp_ref, xn_ref, w_ref, b_ref, j, lpb, rev):
    n = x.shape[0]
    lat = (lpb - j) if rev else (j - 1)
    is_lat = j > 0
    prev_ok = jnp.logical_and(is_lat, lat > 0)
    next_ok = jnp.logical_and(is_lat, lat < lpb - 1)
    xp = xp_ref[...] * prev_ok.astype(F32)
    xn = xn_ref[...] * next_ok.astype(F32)
    row = lax.broadcasted_iota(I32, x.shape, 0)
    x_m1 = jnp.where(row == 0, xp[HALO - 1:HALO, :], pltpu.roll(x, 1, 0))
    x_m2 = jnp.where(row == 0, xp[HALO - 2:HALO - 1, :], jnp.where(row == 1, xp[HALO - 1:HALO, :], pltpu.roll(x, 2, 0)))
    x_p1 = jnp.where(row == n - 1, xn[0:1, :], pltpu.roll(x, n - 1, 0))
    return x_m2 * w_ref[0:1, :] + x_m1 * w_ref[1:2, :] + x * w_ref[2:3, :] + x_p1 * w_ref[3:4, :] + b_ref[...]


def _lin_scan(a, b, rev):
    n = a.shape[0]
    row = lax.broadcasted_iota(I32, a.shape, 0)
    d = 1
    while d < n:
        if rev:
            a_s, b_s, valid = pltpu.roll(a, n - d, 0), pltpu.roll(b, n - d, 0), row < n - d
        else:
            a_s, b_s, valid = pltpu.roll(a, d, 0), pltpu.roll(b, d, 0), row >= d
        a_s = jnp.where(valid, a_s, 1.0)
        b_s = jnp.where(valid, b_s, 0.0)
        b = a * b_s + b
        a = a * a_s
        d *= 2
        yield
    return a, b


def _lru_kernel(*refs, rev, lpb, bsz):
    batch_refs, hf_ref, (cw_ref, cb_ref, wg_ref, bg_ref, lam_ref), o_ref, (h_scr,) = _split_scan_refs(
        refs, rev, bsz, 5, 1)
    j = pl.program_id(0)

    @pl.when(j == 0)
    def _():
        h_scr[...] = jnp.zeros_like(h_scr)

    def chunk(b):
        x_ref, xp_ref, xn_ref = batch_refs[b][:3]
        x = x_ref[...]
        n, w = x.shape
        xc = _short_conv(x, xp_ref, xn_ref, cw_ref, cb_ref, j, lpb, rev)
        yield
        gates = jnp.dot(xc.astype(BF16), wg_ref[...], preferred_element_type=F32) + bg_ref[...]
        yield
        r = _sigmoid(gates[:, :w])
        ig = _sigmoid(gates[:, w:])
        log_a = LRU_C * r * _log_sigmoid(lam_ref[...])
        a = jnp.exp(log_a)
        bb = jnp.sqrt(-jnp.tanh(log_a) * (1.0 + a * a)) * ig * xc
        yield
        a_cum, h_loc = yield from _lin_scan(a, bb, rev)
        h = a_cum * h_scr[b] + h_loc
        h_scr[b] = h[0:1, :] if rev else h[n - 1:n, :]
        if rev:
            g_ref = batch_refs[b][3]
            o_ref[b] = ((hf_ref[b] + h) * jax.nn.gelu(g_ref[...], approximate=True)).astype(o_ref.dtype)
        else:
            o_ref[b] = h

    _round_robin([chunk(b) for b in range(bsz)])


def _lru_dir(u_a, hf, cw, cb, wg, bg, lam, geo, rev):
    w = LRU_WIDTH
    weights = [cw, cb.reshape(1, w), wg, bg.reshape(1, 2 * w), lam.reshape(1, w)]
    scratch = [pltpu.VMEM((geo["batch"], 1, w), F32)]
    return _scan_call(_lru_kernel, "rglru_bwd" if rev else "rglru_fwd", u_a, w, hf, weights, scratch, geo, rev)


def _mlstm_kernel(*refs, rev, lpb, bsz):
    n_weights = 12 if rev else 10
    batch_refs, hf_ref, weights, o_ref, (c_scr, m_scr) = _split_scan_refs(refs, rev, bsz, n_weights, 2)
    j = pl.program_id(0)

    @pl.when(j == 0)
    def _():
        c_scr[...] = jnp.zeros_like(c_scr)
        m_scr[...] = jnp.zeros_like(m_scr)

    _round_robin([_mlstm_chunk(batch_refs[b], hf_ref, weights, o_ref, c_scr, m_scr, b, j, rev, lpb)
                  for b in range(bsz)])


def _round_robin(stage_generators):
    live = list(stage_generators)
    while live:
        live = [g for g in live if next(g, StopIteration) is not StopIteration]


def _mlstm_chunk(in_refs, hf_ref, weights, o_ref, c_scr, m_scr, b, j, rev, lpb):
    if rev:
        x_ref, xp_ref, xn_ref, z_ref = in_refs
        cw_ref, cb_ref, wq_ref, wk_ref, wkt_ref, wv_ref, wg_ref, wgt_ref, bg_ref, bgt_ref, ng_ref, sk_ref = weights
    else:
        x_ref, xp_ref, xn_ref = in_refs
        cw_ref, cb_ref, wq_ref, wk_ref, wkt_ref, wv_ref, wg_ref, wgt_ref, bg_ref, bgt_ref = weights
    x = x_ref[...]
    n = x.shape[0]
    xc = _short_conv(x, xp_ref, xn_ref, cw_ref, cb_ref, j, lpb, rev)
    xc = xc * _sigmoid(xc)
    xcb = xc.astype(BF16)
    yield
    q = jnp.dot(xcb, wq_ref[...], preferred_element_type=F32)
    k = jnp.dot(xcb, wk_ref[...], preferred_element_type=F32)
    yield
    kt =lax.dot_general(wkt_ref[...], xcb, NT_DIMS, preferred_element_type=F32)
    lane_w = lax.broadcasted_iota(I32, (1, ML_PAD), 1)
    ones_lane = (lane_w % LANES == ML_HEAD_DIM).astype(F32)
    v = jnp.dot(x.astype(BF16), wv_ref[...], preferred_element_type=F32) + ones_lane
    yield
    qkv =jnp.concatenate([q, k, v], axis=1).astype(BF16)
    g_col = jnp.dot(qkv, wg_ref[...], preferred_element_type=F32) + bg_ref[...]
    g_row = lax.dot_general(wgt_ref[...], qkv, NT_DIMS, preferred_element_type=F32) + bgt_ref[...]
    yield
    ti =lax.broadcasted_iota(I32, (n, n), 0)
    si = lax.broadcasted_iota(I32, (n, n), 1)
    mask = (si >= ti) if rev else (si <= ti)
    tri = mask.astype(F32)
    tri_t = ((ti >= si) if rev else (ti <= si)).astype(F32)
    b_col = jnp.dot(tri, _log_sigmoid(g_col), precision=HIGHEST, preferred_element_type=F32)
    b_row = jnp.dot(_log_sigmoid(g_row), tri_t, precision=HIGHEST, preferred_element_type=F32)
    yield
    lane =lax.broadcasted_iota(I32, (1, LANES), 1)
    num_mask = (lane < ML_HEAD_DIM).astype(F32)
    den_mask = (lane == ML_HEAD_DIM).astype(F32)
    last = 0 if rev else n - 1
    outs = []
    for h in range(ML_HEADS):
        hs = slice(h * LANES, (h + 1) * LANES)
        bc = b_col[:, ML_HEADS + h:ML_HEADS + h + 1]
        br = b_row[ML_HEADS + h:ML_HEADS + h + 1, :]
        ic = g_col[:, h:h + 1]
        ir = g_row[h:h + 1, :]
        m_prev = m_scr[b, h:h + 1, 0:1]
        dmat = jnp.where(mask, bc - br + ir, -jnp.inf)
        inter = bc + m_prev
        m_t = jnp.maximum(inter, jnp.max(dmat, axis=1, keepdims=True))
        yield
        s =jnp.dot(q[:, hs].astype(BF16), kt[hs, :].astype(BF16), preferred_element_type=F32)
        p = s * jnp.exp(dmat - m_t)
        yield
        w_inter =jnp.exp(inter - m_t)
        vh = v[:, hs]
        c_old = c_scr[b * ML_HEADS + h]
        numden = (jnp.dot(p.astype(BF16), vh.astype(BF16), preferred_element_type=F32)
                  + w_inter * jnp.dot(q[:, hs].astype(BF16), c_old.astype(BF16), preferred_element_type=F32))
        den = jnp.sum(numden * den_mask, axis=1, keepdims=True)
        hh = numden * num_mask / jnp.maximum(jnp.abs(den), jnp.exp(-m_t))
        yield
        b_last =bc[last:last + 1, :]
        ws_col = b_last - bc + ic
        m_new = jnp.maximum(b_last + m_prev, jnp.max(ws_col, axis=0, keepdims=True))
        decay = jnp.exp(b_last + m_prev - m_new)
        wv = (jnp.exp(ws_col - m_new) * vh).astype(BF16)
        c_scr[b * ML_HEADS + h] = decay * c_old + jnp.dot(kt[hs, :].astype(BF16), wv, preferred_element_type=F32)
        m_scr[b, h:h + 1, :] = jnp.broadcast_to(m_new, (1, LANES))
        yield
        if rev:
            hsum = hf_ref[b, :, hs] + hh
            mu = jnp.sum(hsum, axis=1, keepdims=True) * (1.0 / ML_HEAD_DIM)
            cen = (hsum - mu) * num_mask
            var = jnp.sum(cen * cen, axis=1, keepdims=True) * (1.0 / ML_HEAD_DIM)
            hh = cen * lax.rsqrt(var + EPS)
        outs.append(hh)
    hcat = jnp.concatenate(outs, axis=1)
    if rev:
        z = z_ref[...]
        o_ref[b] = ((hcat * ng_ref[...] + sk_ref[...] * xc) * (z * _sigmoid(z))).astype(o_ref.dtype)
    else:
        o_ref[b] = hcat


def _mlstm_dir(u_m, hf, wts, geo, rev):
    names = ["cw", "cb", "wq", "wk", "wkt", "wv", "wg", "wgt", "bg", "bgt"] + (["ng", "sk"] if rev else [])
    bsz = geo["batch"]
    scratch = [pltpu.VMEM((bsz * ML_HEADS, LANES, LANES), F32), pltpu.VMEM((bsz, SUBLANES, LANES), F32)]
    return _scan_call(_mlstm_kernel, "mlstm_bwd" if rev else "mlstm_fwd", u_m, ML_PAD, hf, [wts[nm] for nm in names],
                      scratch, geo, rev)


def _mla_proj_kernel(ub_ref, cos_ref, sin_ref, gq_ref, w1_ref, w2_ref, gkv_ref, wk_ref, wv_ref, q_ref, k_ref, v_ref):
    ub = ub_ref[...]
    cos = cos_ref[...]
    sin = sin_ref[...]
    qn = _rms(ub[:, :MLA_Q_RANK], gq_ref[...]).astype(BF16)
    qa = jnp.dot(qn, w1_ref[...], preferred_element_type=F32)
    qb = jnp.dot(qn, w2_ref[...], preferred_element_type=F32)
    kvn = _rms(ub[:, MLA_Q_RANK:MLA_Q_RANK + MLA_KV_RANK], gkv_ref[...]).astype(BF16)
    kn = jnp.dot(kvn, wk_ref[...], preferred_element_type=F32)
    lane = lax.broadcasted_iota(I32, (1, LANES), 1)
    ones_lane = (lane == MLA_V).astype(F32)
    vn = jnp.dot(kvn, wv_ref[...], preferred_element_type=F32)
    off = MLA_Q_RANK + MLA_KV_RANK
    kr = ub[:, off:off + LANES] * cos + ub[:, off + LANES:off + 2 * LANES] * sin
    for h in range(MLA_HEADS):
        hs = slice(h * LANES, (h + 1) * LANES)
        q_ref[0, h] = (qa[:, hs] * cos + qb[:, hs] * sin).astype(BF16)
        k_ref[0, h] = (kn[:, hs] + kr).astype(BF16)
        v_ref[0, h] = (vn[:, hs] + ones_lane).astype(BF16)


def _mla_proj(u_b, cos_t, sin_t, wts, geo):
    n_tiles = geo["t_all"] // TILE
    n_lat, lpb, bsz = geo["n_lat"], geo["lpb"], geo["batch"]
    hw = MLA_HEADS * LANES

    def batch_of(i):
        return jnp.where(i < n_lat, i // lpb, i - n_lat)

    def blk_of(i):
        return jnp.where(i < n_lat, i % lpb, lpb)

    const = lambda i: (0, 0)
    head_spec = pl.BlockSpec((1, MLA_HEADS, TILE, LANES), lambda i: (batch_of(i), 0, blk_of(i), 0))
    head_shape = jax.ShapeDtypeStruct((bsz, MLA_HEADS, (lpb + 1) * TILE, LANES), BF16)
    return pl.pallas_call(
        _mla_proj_kernel,
        grid=(n_tiles,),
        in_specs=[
            pl.BlockSpec((TILE, UB_W), lambda i: (i, 0)),
            pl.BlockSpec((TILE, LANES), lambda i: (blk_of(i), 0)),
            pl.BlockSpec((TILE, LANES), lambda i: (blk_of(i), 0)),
            pl.BlockSpec((1, MLA_Q_RANK), const),
            pl.BlockSpec((MLA_Q_RANK, hw), const),
            pl.BlockSpec((MLA_Q_RANK, hw), const),
            pl.BlockSpec((1, MLA_KV_RANK), const),
            pl.BlockSpec((MLA_KV_RANK, hw), const),
            pl.BlockSpec((MLA_KV_RANK, hw), const),
        ],
        out_specs=[head_spec, head_spec, head_spec],
        out_shape=[head_shape, head_shape, head_shape],
        compiler_params=_params("arbitrary"),
        name="mla_proj",
    )(u_b, cos_t, sin_t, wts["gq"], wts["w1"], wts["w2"], wts["gkv"], wts["wk"], wts["wv"])


def _flash_kernel(q_ref, k_ref, v_ref, o_ref, m_scr, acc_scr, *, k_start, nk, tk):
    lane = lax.broadcasted_iota(I32, (1, LANES), 1)
    den_mask = (lane == MLA_V).astype(F32)
    outs = []
    for j in range(2):
        q = q_ref[0, j]
        m_scr[...] = jnp.full_like(m_scr, -jnp.inf)
        acc_scr[...] = jnp.zeros_like(acc_scr)

        def body(i, carry, j=j, q=q):
            start = k_start + i * tk
            kk = k_ref[0, j, start:start + tk, :]
            vv = v_ref[0, j, start:start + tk, :]
            s = lax.dot_general(q, kk, NT_DIMS, preferred_element_type=F32)
            cols = [s[:, c * LANES:(c + 1) * LANES] for c in range(tk // LANES)]
            mp = cols[0]
            for sc in cols[1:]:
                mp = jnp.maximum(mp, sc)
            m_old = m_scr[...]
            m_new = jnp.maximum(m_old, jnp.broadcast_to(jnp.max(mp, axis=1, keepdims=True), mp.shape))
            p = jnp.concatenate([jnp.exp2(sc - m_new).astype(BF16) for sc in cols], axis=1)
            acc_scr[...] = jnp.exp2(m_old - m_new) * acc_scr[...] + jnp.dot(p, vv, preferred_element_type=F32)
            m_scr[...] = m_new
            return carry

        for i in range(nk):
            body(i, 0)
        acc = acc_scr[...]
        den = jnp.sum(acc * den_mask, axis=1, keepdims=True)
        outs.append(acc / den)
    o_ref[...] = jnp.where(lane < MLA_V, outs[0], pltpu.roll(outs[1], MLA_V, 1)).astype(o_ref.dtype)


def _key_tile(n):
    for cand in range(min(n, 1024) // LANES * LANES, 0, -LANES):
        if n % cand == 0:
            return cand
    raise ValueError(n)


def _flash(q, k, v, tq, q_blk0, nq, k_start, k_len, out_blk0, t_all):
    bsz, heads, rows, _ = q.shape
    tk = _key_tile(k_len)
    kern = functools.partial(_flash_kernel, k_start=k_start, nk=k_len // tk, tk=tk)
    return pl.pallas_call(
        kern,
        grid=(bsz, heads // 2, nq),
        in_specs=[
            pl.BlockSpec((1, 2, tq, LANES), lambda b, h, i: (b, h, q_blk0 + i, 0)),
            pl.BlockSpec((1, 2, rows, LANES), lambda b, h, i: (b, h, 0, 0)),
            pl.BlockSpec((1, 2, rows, LANES), lambda b, h, i: (b, h, 0, 0)),
        ],
        out_specs=pl.BlockSpec((tq, LANES), lambda b, h, i: (out_blk0 + b * nq + i, h)),
        out_shape=jax.ShapeDtypeStruct((t_all, heads * MLA_V), BF16),
        scratch_shapes=[pltpu.VMEM((tq, LANES), F32), pltpu.VMEM((tq, LANES), F32)],
        compiler_params=_params("arbitrary", "arbitrary", "arbitrary"),
        name="mla_attention",
    )(q, k, v)


def _out_kernel(ya_ref, yb_ref, yc_ref, x_ref, mod_ref, w_ref, g_ref, wr_ref, br_ref,
                xo_ref, h2_ref, te_ref, tg_ref, rk_ref, cnt_ref, *, n_lat, lat_per_batch, ctx_row):
    i = pl.program_id(0)

    @pl.when(i == 0)
    def _():
        cnt_ref[...] = jnp.zeros_like(cnt_ref)

    r = _mod_row(i, n_lat, lat_per_batch, ctx_row)
    g1 = mod_ref[2, pl.ds(r, 1), :]
    sh2 = mod_ref[3, pl.ds(r, 1), :]
    sc2 = mod_ref[4, pl.ds(r, 1), :]
    y = jnp.concatenate([ya_ref[0], yb_ref[...], yc_ref[0]], axis=1)
    x1 = x_ref[...] + g1 * jnp.dot(y, w_ref[...], preferred_element_type=F32)
    xo_ref[...] = x1
    h2 = _rms(x1, g_ref[...]) * (1.0 + sc2) + sh2
    h2_ref[...] = h2.reshape(h2_ref.shape)
    logits = jnp.dot(h2, wr_ref[...], precision=HIGHEST, preferred_element_type=F32) + br_ref[...]
    n, ne = logits.shape
    lane_e = lax.broadcasted_iota(I32, (n, ne), 1).astype(F32)
    lane_o = lax.broadcasted_iota(I32, (n, LANES), 1)
    vals, idxs = [], []
    for _ in range(TOP_K):
        m = jnp.max(logits, axis=1, keepdims=True)
        idx = jnp.min(jnp.where(logits == m, lane_e, float(ne)), axis=1, keepdims=True)
        logits = jnp.where(lane_e == idx, -jnp.inf, logits)
        vals.append(m)
        idxs.append(idx)
    exps = [jnp.exp(vv - vals[0]) for vv in vals]
    tot = exps[0] + exps[1] + exps[2] + exps[3]
    hits = [(lane_e == idx).astype(F32) for idx in idxs]
    chosen = hits[0] + hits[1] + hits[2] + hits[3]
    ti = lax.broadcasted_iota(I32, (n, n), 0)
    si = lax.broadcasted_iota(I32, (n, n), 1)
    before = jnp.dot((si < ti).astype(BF16), chosen.astype(BF16), preferred_element_type=F32) + cnt_ref[0:1, :]
    cnt_ref[...] = cnt_ref[...] + jnp.sum(chosen, axis=0, keepdims=True)
    te = jnp.zeros((n, LANES), F32)
    tg = jnp.zeros((n, LANES), F32)
    rk = jnp.zeros((n, LANES), F32)
    for kk in range(TOP_K):
        te = jnp.where(lane_o == kk, idxs[kk], te)
        tg = jnp.where(lane_o == kk, exps[kk] / tot, tg)
        rk = jnp.where(lane_o == kk, jnp.sum(hits[kk] * before, axis=1, keepdims=True), rk)
    te_ref[...] = te.astype(I32)
    tg_ref[...] = tg
    rk_ref[...] = rk.astype(I32)


def _out_proj(ya, yb, yc, x_all, mod_l, w_out_p, g, w_router, b_router, geo, n_tiles):
    d = x_all.shape[1]
    rows = n_tiles * TILE
    kern = functools.partial(_out_kernel, n_lat=geo["n_lat"], lat_per_batch=geo["lpb"], ctx_row=geo["ctx_row"])
    row_blk = lambda i: (i, 0)
    const = lambda i: (0, 0)
    n_lat, lpb = geo["n_lat"], geo["lpb"]
    scan_blk = lambda i: (jnp.where(i < n_lat, i // lpb, i - n_lat), jnp.where(i < n_lat, i % lpb, lpb), 0)
    w_router = jnp.pad(w_router, ((0, 0), (0, LANES - N_EXPERTS)))
    b_router = jnp.pad(b_router, (0, LANES - N_EXPERTS), constant_values=-jnp.inf)
    return pl.pallas_call(
        kern,
        grid=(n_tiles,),
        in_specs=[
            pl.BlockSpec((1, TILE, ya.shape[2]), scan_blk),
            pl.BlockSpec((TILE, yb.shape[1]), row_blk),
            pl.BlockSpec((1, TILE, yc.shape[2]), scan_blk),
            pl.BlockSpec((TILE, d), row_blk),
            pl.BlockSpec(mod_l.shape, lambda i: (0, 0, 0)),
            pl.BlockSpec(w_out_p.shape, const),
            pl.BlockSpec((1, d), const),
            pl.BlockSpec(w_router.shape, const),
            pl.BlockSpec((1, LANES), const),
        ],
        out_specs=[pl.BlockSpec((TILE, d), row_blk), pl.BlockSpec((TILE, SUBLANES, d // SUBLANES), lambda i: (i, 0, 0)),
                   pl.BlockSpec((TILE, LANES), row_blk), pl.BlockSpec((TILE, LANES), row_blk),
                   pl.BlockSpec((TILE, LANES), row_blk), pl.BlockSpec((SUBLANES, LANES), const)],
        out_shape=[jax.ShapeDtypeStruct((rows, d), F32), jax.ShapeDtypeStruct((rows, SUBLANES, d // SUBLANES), F32),
                   jax.ShapeDtypeStruct((rows, LANES), I32), jax.ShapeDtypeStruct((rows, LANES), F32),
                   jax.ShapeDtypeStruct((rows, LANES), I32), jax.ShapeDtypeStruct((SUBLANES, LANES), F32)],
        compiler_params=_params("arbitrary"),
        name="out_proj_router",
    )(ya, yb, yc, x_all, mod_l, w_out_p, g.reshape(1, d), w_router, b_router.reshape(1, LANES))


def _dispatch_kernel(zrow_ref, nu_ref, dest_ref, h_ref, xs_hbm, zbuf, zsem, sem):
    i = pl.program_id(0)
    n_tok = h_ref.shape[0]
    n_blocks = xs_hbm.shape[0] // MOE_BLOCK

    def zero_copy(blk_row):
        row = pl.multiple_of(blk_row, MOE_BLOCK)
        return pltpu.make_async_copy(zbuf, xs_hbm.at[pl.ds(row, MOE_BLOCK)], zsem)

    @pl.when(i == 0)
    def _():
        zbuf[...] = jnp.zeros_like(zbuf)
        for start in (True, False):
            for e in range(N_EXPERTS):
                @pl.when(zrow_ref[e] >= 0)
                def _(e=e, start=start):
                    zero_copy(zrow_ref[e]).start() if start else zero_copy(zrow_ref[e]).wait()

            def tail(b, carry, start=start):
                zero_copy(b * MOE_BLOCK).start() if start else zero_copy(b * MOE_BLOCK).wait()
                return carry
            lax.fori_loop(nu_ref[0], n_blocks, tail, 0)

    for kk in range(TOP_K):
        def issue(t, carry, kk=kk):
            pltpu.make_async_copy(h_ref.at[t], xs_hbm.at[dest_ref[0, 0, kk * n_tok + t]], sem).start()
            return carry
        lax.fori_loop(0, n_tok, issue, 0, unroll=16)
    for _ in range(TOP_K):
        pltpu.make_async_copy(h_ref, xs_hbm.at[pl.ds(0, n_tok)], sem).wait()


def _moe_dispatch(h2, dest, zrow, n_used, n_slots):
    t_moe, tile_shape = dest.shape[0], h2.shape[1:]
    n_tiles = t_moe // TILE
    dest3 = dest.reshape(n_tiles, TILE, TOP_K).transpose(0, 2, 1).reshape(n_tiles, 1, TOP_K * TILE)
    grid_spec = pltpu.PrefetchScalarGridSpec(
        num_scalar_prefetch=2,
        grid=(n_tiles,),
        in_specs=[
            pl.BlockSpec((1, 1, TOP_K * TILE), lambda i, zr, nu: (i, 0, 0), memory_space=pltpu.SMEM),
            pl.BlockSpec((TILE,) + tile_shape, lambda i, zr, nu: (i, 0, 0)),
        ],
        out_specs=pl.BlockSpec(memory_space=pl.ANY),
        scratch_shapes=[pltpu.VMEM((MOE_BLOCK,) + tile_shape, F32), pltpu.SemaphoreType.DMA, pltpu.SemaphoreType.DMA],
    )
    return pl.pallas_call(
        _dispatch_kernel,
        grid_spec=grid_spec,
        out_shape=jax.ShapeDtypeStruct((n_slots,) + tile_shape, F32),
        compiler_params=_params("arbitrary"),
        name="moe_dispatch",
    )(zrow, n_used, dest3, h2)


def _moe_kernel(be_ref, nu_ref, x_ref, wgu_ref, bgu_ref, wd_ref, bd_ref, y_ref, wgu_bf, wd_bf):
    i = pl.program_id(0)

    @pl.when(i < nu_ref[0])
    def _():
        prev_e = be_ref[jnp.maximum(i - 1, 0)]

        @pl.when(jnp.logical_or(i == 0, be_ref[i] != prev_e))
        def _():
            wgu_bf[...] = wgu_ref[0, 0].astype(BF16)
            wd_bf[...] = wd_ref[0, 0].astype(BF16)

        x = x_ref[...].reshape(x_ref.shape[0], -1).astype(BF16)
        gu = jnp.dot(x, wgu_bf[...], preferred_element_type=F32) + bgu_ref[0, 0]
        glu = jnp.minimum(gu[:, :D_EXPERT], SWIGLU_LIMIT)
        lin = jnp.clip(gu[:, D_EXPERT:], -SWIGLU_LIMIT, SWIGLU_LIMIT)
        act = glu * _sigmoid(SWIGLU_ALPHA * glu) * (lin + 1.0)
        y = jnp.dot(act.astype(BF16), wd_bf[...], preferred_element_type=F32) + bd_ref[0, 0]
        y_ref[...] = y.reshape(y_ref.shape)

    @pl.when(i >= nu_ref[0])
    def _():
        y_ref[...] = jnp.zeros_like(y_ref)


def _moe_experts(x_sorted, block_e, n_used, w_gu, b_gu, w_down, b_down, layer):
    n_slots, tile_shape = x_sorted.shape[0], x_sorted.shape[1:]
    n_blocks = n_slots // MOE_BLOCK
    depth, d = w_gu.shape[0], w_gu.shape[2]
    slot_blk = pl.BlockSpec((MOE_BLOCK,) + tile_shape, lambda i, be, nu: (i, 0, 0))

    def expert(i, be, nu):
        return (layer, be[jnp.minimum(i, nu[0] - 1)], 0, 0)

    grid_spec = pltpu.PrefetchScalarGridSpec(
        num_scalar_prefetch=2,
        grid=(n_blocks,),
        in_specs=[
            slot_blk,
            pl.BlockSpec((1, 1, d, 2 * D_EXPERT), expert),
            pl.BlockSpec((1, 1, 1, 2 * D_EXPERT), expert),
            pl.BlockSpec((1, 1, D_EXPERT, d), expert),
            pl.BlockSpec((1, 1, 1, d), expert),
        ],
        out_specs=slot_blk,
        scratch_shapes=[pltpu.VMEM((d, 2 * D_EXPERT), BF16), pltpu.VMEM((D_EXPERT, d), BF16)],
    )
    return pl.pallas_call(
        _moe_kernel,
        grid_spec=grid_spec,
        out_shape=jax.ShapeDtypeStruct((n_slots,) + tile_shape, F32),
        compiler_params=_params("arbitrary"),
        name="moe_experts",
    )(block_e, n_used, x_sorted, w_gu, b_gu.reshape(depth, N_EXPERTS, 1, -1), w_down,
      b_down.reshape(depth, N_EXPERTS, 1, -1))


def _comb_kernel(inv_ref, inv_next_ref, y_hbm, tg_ref, x_ref, mod_ref, fg_ref, o_ref, cbuf, sem,
                 *, n_lat, lat_per_batch, ctx_row, final):
    i = pl.program_id(0)
    n_steps = pl.num_programs(0)
    slot = i % 2
    rows = cbuf.shape[1]
    per_tile = TILE // COMB_TILE

    def issue(idx_ref, dst_slot):
        def body(r, carry):
            pltpu.make_async_copy(y_hbm.at[idx_ref[0, 0, r]], cbuf.at[dst_slot, r], sem.at[dst_slot]).start()
            return carry
        lax.fori_loop(0, rows, body, 0, unroll=16)

    @pl.when(i == 0)
    def _():
        issue(inv_ref, 0)

    @pl.when(i + 1 < n_steps)
    def _():
        issue(inv_next_ref, 1 - slot)

    pltpu.make_async_copy(y_hbm.at[pl.ds(0, rows)], cbuf.at[slot], sem.at[slot]).wait()

    r = _mod_row(i // per_tile, n_lat, lat_per_batch, ctx_row)
    g2 = mod_ref[5, pl.ds(r, 1), :]
    tg = tg_ref[...]
    f = None
    for kk in range(TOP_K):
        rows_k = cbuf[slot, kk * COMB_TILE:(kk + 1) * COMB_TILE].reshape(COMB_TILE, -1)
        f = tg[:, kk:kk + 1] * rows_k if f is None else f + tg[:, kk:kk + 1] * rows_k
    x2 = x_ref[...] + g2 * f
    o_ref[...] = _rms(x2, fg_ref[...]) if final else x2


def _moe_combine(y_sorted, dest, tg, x_all, mod_l, final_g, geo, final):
    t_moe = dest.shape[0]
    d = x_all.shape[1]
    n_steps = t_moe // COMB_TILE
    inv3 = dest.reshape(n_steps, COMB_TILE, TOP_K).transpose(0, 2, 1).reshape(n_steps, 1, TOP_K * COMB_TILE)
    kern = functools.partial(_comb_kernel, n_lat=geo["n_lat"], lat_per_batch=geo["lpb"], ctx_row=geo["ctx_row"],
                             final=final)
    return pl.pallas_call(
        kern,
        grid=(n_steps,),
        in_specs=[
            pl.BlockSpec((1, 1, TOP_K * COMB_TILE), lambda i: (i, 0, 0), memory_space=pltpu.SMEM),
            pl.BlockSpec((1, 1, TOP_K * COMB_TILE), lambda i: (jnp.minimum(i + 1, n_steps - 1), 0, 0),
                         memory_space=pltpu.SMEM),
            pl.BlockSpec(memory_space=pl.ANY),
            pl.BlockSpec((COMB_TILE, LANES), lambda i: (i, 0)),
            pl.BlockSpec((COMB_TILE, d), lambda i: (i, 0)),
            pl.BlockSpec(mod_l.shape, lambda i: (0, 0, 0)),
            pl.BlockSpec((1, d), lambda i: (0, 0)),
        ],
        out_specs=pl.BlockSpec((COMB_TILE, d), lambda i: (i, 0)),
        out_shape=jax.ShapeDtypeStruct((t_moe, d), F32),
        scratch_shapes=[pltpu.VMEM((2, TOP_K * COMB_TILE) + y_sorted.shape[1:], F32), pltpu.SemaphoreType.DMA((2,))],
        compiler_params=_params("arbitrary"),
        name="moe_combine",
    )(inv3, inv3, y_sorted, tg, x_all, mod_l, final_g.reshape(1, d))


def _route(te, rk, cnt, t_moe):
    counts = cnt[0, :N_EXPERTS].astype(I32)
    padded = (counts + MOE_BLOCK - 1) // MOE_BLOCK * MOE_BLOCK
    padded_end = jnp.cumsum(padded)
    padded_start = padded_end - padded
    experts = jnp.arange(N_EXPERTS, dtype=I32)
    e_tok = te[:t_moe, :TOP_K]
    dest = rk[:t_moe, :TOP_K] + jnp.sum(jnp.where(e_tok[..., None] == experts, padded_start, 0), axis=-1)
    n_blocks = -(-(t_moe * TOP_K) // MOE_BLOCK) + N_EXPERTS
    blk_start = jnp.arange(n_blocks, dtype=I32) * MOE_BLOCK
    block_e = jnp.minimum(jnp.sum((padded_end[None, :] <= blk_start[:, None]).astype(I32), axis=1), N_EXPERTS - 1)
    n_used = (padded_end[-1:] // MOE_BLOCK).astype(I32)
    zrow = jnp.where(counts > 0, padded_end - MOE_BLOCK, -1).astype(I32)
    return dest.astype(I32), block_e, n_used, zrow, n_blocks * MOE_BLOCK


def _block_diag_dense(w):
    g, i, j = w.shape
    out = jnp.zeros((g * i, g * j), w.dtype)
    for n in range(g):
        out = out.at[n * i:(n + 1) * i, n * j:(n + 1) * j].set(w[n])
    return out


def _pad_heads(w, heads, axis):
    shape = list(w.shape)
    shape[axis:axis + 1] = [heads, shape[axis] // heads]
    w = w.reshape(shape)
    pad = [(0, 0)] * w.ndim
    pad[axis + 1] = (0, LANES - shape[axis + 1])
    w = jnp.pad(w, pad)
    shape[axis:axis + 2] = [heads * LANES]
    return w.reshape(shape)


_ROPE_SWAP = np.concatenate([np.arange(8, 16), np.arange(0, 8), np.arange(24, 32), np.arange(16, 24)])


def _prep_in_weight(w_in):
    d = w_in.shape[0]
    o = np.cumsum([0, LRU_WIDTH, LRU_WIDTH, MLA_Q_RANK, MLA_KV_RANK, MLA_ROPE, ML_WIDTH, ML_WIDTH])
    a_xg = w_in[:, o[0]:o[2]]
    b_qkv = w_in[:, o[2]:o[4]]
    kr = w_in[:, o[4]:o[5]]
    z_nope = jnp.zeros((d, MLA_NOPE), w_in.dtype)
    z_tail = jnp.zeros((d, LANES - MLA_NOPE - MLA_ROPE), w_in.dtype)
    m_x = _pad_heads(w_in[:, o[5]:o[6]], ML_HEADS, 1)
    m_z = _pad_heads(w_in[:, o[6]:o[7]], ML_HEADS, 1)
    return jnp.concatenate([a_xg, b_qkv, z_nope, kr, z_tail, z_nope, kr[:, _ROPE_SWAP], z_tail, m_x, m_z],
                           axis=1).astype(BF16)


def _prep_out_weight(w_out):
    a = w_out[:LRU_WIDTH]
    b = w_out[LRU_WIDTH:LRU_WIDTH + MLA_HEADS * MLA_V]
    c = _pad_heads(w_out[LRU_WIDTH + MLA_HEADS * MLA_V:], ML_HEADS, 0)
    return jnp.concatenate([a, b, c], axis=0).astype(BF16)


def _prep_mla(q_norm_g, w_qb, kv_norm_g, w_kvb):
    scale = (MLA_NOPE + MLA_ROPE) ** -0.5 * math.log2(math.e)
    rq = w_qb.shape[0]
    wq = w_qb.reshape(rq, MLA_HEADS, MLA_NOPE + MLA_ROPE) * scale
    nope, rope = wq[..., :MLA_NOPE], wq[..., MLA_NOPE:]
    z32 = jnp.zeros((rq, MLA_HEADS, LANES - MLA_NOPE - MLA_ROPE), w_qb.dtype)
    w1 = jnp.concatenate([nope, rope, z32], axis=-1).reshape(rq, MLA_HEADS * LANES)
    w2 = jnp.concatenate([jnp.zeros_like(nope), rope[..., _ROPE_SWAP], z32], axis=-1).reshape(rq, MLA_HEADS * LANES)
    rk = w_kvb.shape[0]
    wkv = w_kvb.reshape(rk, MLA_HEADS, MLA_NOPE + MLA_V)
    z64 = jnp.zeros((rk, MLA_HEADS, LANES - MLA_NOPE), w_kvb.dtype)
    wk = jnp.concatenate([wkv[..., :MLA_NOPE], z64], axis=-1).reshape(rk, MLA_HEADS * LANES)
    wv = jnp.concatenate([wkv[..., MLA_NOPE:], z64], axis=-1).reshape(rk, MLA_HEADS * LANES)
    return {"gq": q_norm_g.reshape(1, -1), "w1": w1.astype(BF16), "w2": w2.astype(BF16),
            "gkv": kv_norm_g.reshape(1, -1), "wk": wk.astype(BF16), "wv": wv.astype(BF16)}


def _prep_mlstm(conv_w, conv_b, wq, wk, wv, w_gate_d, b_gate_d, norm_g, skip):
    def proj(w):
        return _pad_heads(_pad_heads(_block_diag_dense(w), ML_HEADS, 0), ML_HEADS, 1)

    wk_p = proj(wk) * (ML_HEAD_DIM ** -0.5)
    wg = jnp.concatenate([_pad_heads(w_gate_d[i * ML_WIDTH:(i + 1) * ML_WIDTH], ML_HEADS, 0) for i in range(3)], axis=0)
    ng = w_gate_d.shape[1]
    wg = jnp.pad(wg, ((0, 0), (0, LANES - ng)))
    b_gate_d = jnp.pad(b_gate_d, (0, LANES - ng))
    gate_rows = 2 * SUBLANES
    return {
        "cw": _pad_heads(conv_w, ML_HEADS, 1), "cb": _pad_heads(conv_b.reshape(1, -1), ML_HEADS, 1),
        "wq": proj(wq).astype(BF16), "wk": wk_p.astype(BF16), "wkt": wk_p.T.astype(BF16), "wv": proj(wv).astype(BF16),
        "wg": wg.astype(BF16), "wgt": wg.T[:gate_rows].astype(BF16),
        "bg": b_gate_d.reshape(1, -1), "bgt": b_gate_d[:gate_rows].reshape(-1, 1),
        "ng": _pad_heads(norm_g.reshape(1, -1), ML_HEADS, 1), "sk": _pad_heads(skip.reshape(1, -1), ML_HEADS, 1),
    }


def _rope_tables(seq, ctx_len):
    rows = seq // GRID_W
    row = jnp.repeat(jnp.arange(rows, dtype=I32), GRID_W)
    col = jnp.tile(jnp.arange(GRID_W, dtype=I32), rows)
    freqs = ROPE_BASE ** (-jnp.arange(ROPE_AXIS_FREQ, dtype=F32) / ROPE_AXIS_FREQ)
    ang_r, ang_c = row[:, None] * freqs, col[:, None] * freqs
    cos32 = jnp.concatenate([jnp.cos(ang_r), jnp.cos(ang_r), jnp.cos(ang_c), jnp.cos(ang_c)], axis=1)
    sin32 = jnp.concatenate([-jnp.sin(ang_r), jnp.sin(ang_r), -jnp.sin(ang_c), jnp.sin(ang_c)], axis=1)
    cos32 = jnp.concatenate([cos32, jnp.ones((ctx_len, MLA_ROPE), F32)], axis=0)
    sin32 = jnp.concatenate([sin32, jnp.zeros((ctx_len, MLA_ROPE), F32)], axis=0)
    n = seq + ctx_len
    tail = jnp.zeros((n, LANES - MLA_NOPE - MLA_ROPE), F32)
    cos_t = jnp.concatenate([jnp.ones((n, MLA_NOPE), F32), cos32, tail], axis=1)
    sin_t = jnp.concatenate([jnp.zeros((n, MLA_NOPE), F32), sin32, tail], axis=1)
    return cos_t, sin_t


def kernel(x, c, ctx, c_ctx, norm1_g, norm2_g, w_mod, b_mod, w_in, w_out, lru_conv_w, lru_conv_b, lru_wa, lru_ba,
           lru_wx, lru_bx, lru_lambda, mla_q_norm_g, mla_w_qb, mla_kv_norm_g, mla_w_kvb, ml_conv_w, ml_conv_b,
           ml_wq, ml_wk, ml_wv, ml_w_gate, ml_b_gate, ml_norm_g, ml_skip, w_router, b_router, w_gu, b_gu, w_down,
           b_down, final_g):
    bsz, seq, d = x.shape
    ctx_len = ctx.shape[1]
    depth = w_mod.shape[0]
    assert ctx_len == TILE and seq % ATTN_Q_TILE == 0 and bsz + 1 <= SUBLANES
    t_lat = bsz * seq
    t_all = t_lat + bsz * ctx_len
    geo = {"batch": bsz, "lpb": seq // TILE, "n_lat": t_lat // TILE, "ctx_row": bsz, "t_all": t_all}

    cv = jnp.zeros((SUBLANES, d), F32).at[:bsz].set(c).at[bsz].set(c_ctx)
    mod = _modulation(cv, w_mod, b_mod)
    cos_t, sin_t = _rope_tables(seq, ctx_len)
    x_all = jnp.concatenate([x.reshape(t_lat, d), ctx.reshape(bsz * ctx_len, d)], axis=0)

    out = None
    for l in range(depth):
        last = l == depth - 1
        u_a, u_b, u_m = _in_proj(x_all, mod[l], norm1_g[l], _prep_in_weight(w_in[l]), geo)

        hf = None
        for dd in range(2):
            wg = jnp.concatenate([_block_diag_dense(lru_wa[l, dd]), _block_diag_dense(lru_wx[l, dd])], axis=1)
            bg = jnp.concatenate([lru_ba[l, dd], lru_bx[l, dd]])
            hf = _lru_dir(u_a, hf, lru_conv_w[l], lru_conv_b[l], wg.astype(BF16), bg, lru_lambda[l, dd], geo, dd == 1)
        ya = hf

        q, k, v = _mla_proj(u_b, cos_t, sin_t, _prep_mla(mla_q_norm_g[l], mla_w_qb[l], mla_kv_norm_g[l], mla_w_kvb[l]), geo)
        tq = ATTN_Q_TILE
        yb = _flash(q, k, v, tq, 0, seq // tq, 0, seq + ctx_len, 0, t_lat)
        if not last:
            yb_c = _flash(q, k, v, TILE, seq // TILE, 1, seq, ctx_len, 0, bsz * ctx_len)
            yb = jnp.concatenate([yb, yb_c], axis=0)

        hf = None
        for dd in range(2):
            wts = _prep_mlstm(ml_conv_w[l], ml_conv_b[l], ml_wq[l], ml_wk[l], ml_wv[l], ml_w_gate[l, dd],
                              ml_b_gate[l, dd], ml_norm_g[l], ml_skip[l])
            hf = _mlstm_dir(u_m, hf, wts, geo, dd == 1)
        yc = hf

        n_tiles = (t_lat if last else t_all) // TILE
        x_mid, h2, te, tg, rk, cnt = _out_proj(ya, yb, yc, x_all, mod[l], _prep_out_weight(w_out[l]), norm2_g[l],
                                               w_router[l], b_router[l], geo, n_tiles)
        t_moe = n_tiles * TILE
        dest, block_e, n_used, zrow, n_slots = _route(te, rk, cnt, t_moe)
        x_sorted = _moe_dispatch(h2, dest, zrow, n_used, n_slots)
        y_sorted = _moe_experts(x_sorted, block_e, n_used, w_gu, b_gu, w_down, b_down, l)
        x_all = _moe_combine(y_sorted, dest, tg, x_mid, mod[l], final_g, geo, last)
        if last:
            out = x_all.reshape(bsz, seq, d)
    return out
```

```python
import functools
import math

import jax
import jax.numpy as jnp
import numpy as np
from jax import lax
from jax.experimental import pallas as pl
from jax.experimental.pallas import tpu as pltpu

F32 = jnp.float32
BF16 = jnp.bfloat16
I32 = jnp.int32
HIGHEST = lax.Precision.HIGHEST

LANES = 128
SUBLANES = 8
VMEM_LIMIT_BYTES = 56 * 1024 * 1024

GRID_W = 64
EPS = 1e-6
LRU_WIDTH = 256
LRU_C = 8.0
CONV_W = 4
MLA_HEADS = 8
MLA_NOPE = 64
MLA_ROPE = 32
MLA_V = 64
MLA_Q_RANK = 256
MLA_KV_RANK = 128
ROPE_AXIS_FREQ = MLA_ROPE // 4
ROPE_BASE = 10000.0
ML_HEADS = 4
ML_HEAD_DIM = 64
ML_WIDTH = ML_HEADS * ML_HEAD_DIM
ML_PAD = ML_HEADS * LANES
N_EXPERTS = 32
TOP_K = 4
D_EXPERT = 1024
SWIGLU_LIMIT = 7.0
SWIGLU_ALPHA = 1.702
MOE_BLOCK = 256

TILE = 256
HALO = SUBLANES
UB_W = MLA_Q_RANK + MLA_KV_RANK + 2 * LANES
COMB_TILE = 128
ATTN_Q_TILE = 1024

NT_DIMS = (((1,), (1,)), ((), ()))


def _params(*sem):
    return pltpu.CompilerParams(dimension_semantics=sem, vmem_limit_bytes=VMEM_LIMIT_BYTES)


def _sigmoid(x):
    return 1.0 / (1.0 + jnp.exp(-x))


def _log_sigmoid(x):
    return jnp.minimum(x, 0.0) - jnp.log1p(jnp.exp(-jnp.abs(x)))


def _split3(x):
    hi = x.astype(BF16)
    rest = x - hi.astype(F32)
    mid = rest.astype(BF16)
    return hi, mid, (rest - mid.astype(F32)).astype(BF16)


def _rms(x, g):
    return x * lax.rsqrt(jnp.mean(x * x, axis=-1, keepdims=True) + EPS) * g


def _mod_kernel(cv_ref, w_ref, b_ref, o_ref):
    cv = cv_ref[...]
    a = cv * _sigmoid(cv)
    o_ref[0, 0] = jnp.dot(a, w_ref[0], precision=HIGHEST, preferred_element_type=F32) + b_ref[0, 0]


def _modulation(cv, w_mod, b_mod):
    depth, d, _ = w_mod.shape
    rows = cv.shape[0]
    return pl.pallas_call(
        _mod_kernel,
        grid=(depth, 6),
        in_specs=[
            pl.BlockSpec((rows, d), lambda l, j: (0, 0)),
            pl.BlockSpec((1, d, d), lambda l, j: (l, 0, j)),
            pl.BlockSpec((1, 1, 1, d), lambda l, j: (l, j, 0, 0)),
        ],
        out_specs=pl.BlockSpec((1, 1, rows, d), lambda l, j: (l, j, 0, 0)),
        out_shape=jax.ShapeDtypeStruct((depth, 6, rows, d), F32),
        compiler_params=_params("arbitrary", "arbitrary"),
        name="modulation",
    )(cv, w_mod, b_mod.reshape(depth, 6, 1, d))


def _mod_row(i, n_lat, lat_per_batch, ctx_row):
    return jnp.where(i < n_lat, i // lat_per_batch, ctx_row)


def _in_kernel(x_ref, mod_ref, g_ref, w_ref, ua_ref, ub_ref, um_ref, *, n_lat, lat_per_batch, ctx_row):
    r = _mod_row(pl.program_id(0), n_lat, lat_per_batch, ctx_row)
    sh = mod_ref[0, pl.ds(r, 1), :]
    sc = mod_ref[1, pl.ds(r, 1), :]
    h = _rms(x_ref[...], g_ref[...]) * (1.0 + sc) + sh
    u = jnp.dot(h.astype(BF16), w_ref[...], preferred_element_type=F32)
    wa = ua_ref.shape[1]
    wb = ub_ref.shape[1]
    ua_ref[...] = u[:, :wa]
    ub_ref[...] = u[:, wa:wa + wb]
    um_ref[...] = u[:, wa + wb:]


def _in_proj(x_all, mod_l, g, w_in_p, geo):
    t_all, d = x_all.shape
    n_tiles = t_all // TILE
    wa, wb, wm = 2 * LRU_WIDTH, UB_W, 2 * ML_PAD
    kern = functools.partial(_in_kernel, n_lat=geo["n_lat"], lat_per_batch=geo["lpb"], ctx_row=geo["ctx_row"])
    return pl.pallas_call(
        kern,
        grid=(n_tiles,),
        in_specs=[
            pl.BlockSpec((TILE, d), lambda i: (i, 0)),
            pl.BlockSpec(mod_l.shape, lambda i: (0, 0, 0)),
            pl.BlockSpec((1, d), lambda i: (0, 0)),
            pl.BlockSpec(w_in_p.shape, lambda i: (0, 0)),
        ],
        out_specs=[
            pl.BlockSpec((TILE, wa), lambda i: (i, 0)),
            pl.BlockSpec((TILE, wb), lambda i: (i, 0)),
            pl.BlockSpec((TILE, wm), lambda i: (i, 0)),
        ],
        out_shape=[
            jax.ShapeDtypeStruct((t_all, wa), F32),
            jax.ShapeDtypeStruct((t_all, wb), F32),
            jax.ShapeDtypeStruct((t_all, wm), F32),
        ],
        compiler_params=_params("arbitrary"),
        name="in_proj",
    )(x_all, mod_l, g.reshape(1, d), w_in_p)


def _chunk_block(b, j, geo, rev):
    lat = (geo["lpb"] - j) if rev else (j - 1)
    return jnp.where(j == 0, geo["n_lat"] + b, b * geo["lpb"] + lat)


def _local_block(j, lpb, rev):
    return jnp.where(j == 0, lpb, (lpb - j) if rev else (j - 1))


def _chunk_specs(width, col, geo, rev, b):
    per = TILE // HALO
    last = geo["t_all"] // HALO - 1

    def cur(j):
        return (_chunk_block(b, j, geo, rev), col)

    def prev(j):
        return (jnp.maximum(_chunk_block(b, j, geo, rev) * per - 1, 0), col)

    def nxt(j):
        return (jnp.minimum((_chunk_block(b, j, geo, rev) + 1) * per, last), col)

    return [pl.BlockSpec((TILE, width), cur), pl.BlockSpec((HALO, width), prev), pl.BlockSpec((HALO, width), nxt)]


def _scan_call(kern, name, src, width, hf, weights, scratch, geo, rev):
    bsz, lpb = geo["batch"], geo["lpb"]
    chunk_of_all = pl.BlockSpec((bsz, TILE, width), lambda j: (0, _local_block(j, lpb, rev), 0))
    specs, args = [], []
    for b in range(bsz):
        specs += _chunk_specs(width, 0, geo, rev, b)
        args += [src, src, src]
        if rev:
            specs.append(pl.BlockSpec((TILE, width), lambda j, b=b: (_chunk_block(b, j, geo, rev), 1)))
            args.append(src)
    if rev:
        specs.append(chunk_of_all)
        args.append(hf)
    for wgt in weights:
        specs.append(pl.BlockSpec(wgt.shape, lambda j: (0, 0)))
        args.append(wgt)
    return pl.pallas_call(
        functools.partial(kern, rev=rev, lpb=lpb, bsz=bsz),
        grid=(lpb + 1,),
        in_specs=specs,
        out_specs=chunk_of_all,
        out_shape=jax.ShapeDtypeStruct((bsz, (lpb + 1) * TILE, width), BF16 if rev else F32),
        scratch_shapes=scratch,
        compiler_params=_params("arbitrary"),
        name=name,
    )(*args)


def _split_scan_refs(refs, rev, bsz, n_weights, n_scratch):
    n_in = 4 if rev else 3
    batch_refs = [refs[b * n_in:(b + 1) * n_in] for b in range(bsz)]
    w0 = bsz * n_in
    hf_ref = refs[w0] if rev else None
    w0 += 1 if rev else 0
    weights = refs[w0:w0 + n_weights]
    return batch_refs, hf_ref, weights, refs[w0 + n_weights], refs[len(refs) - n_scratch:]


def _short_conv(x, xp_ref, xn_ref, w_ref, b_ref, j, lpb, rev):
    n = x.shape[0]
    lat = (lpb - j) if rev else (j - 1)
    is_lat = j > 0
    prev_ok = jnp.logical_and(is_lat, lat > 0)
    next_ok = jnp.logical_and(is_lat, lat < lpb - 1)
    xp = xp_ref[...] * prev_ok.astype(F32)
    xn = xn_ref[...] * next_ok.astype(F32)
    row = lax.broadcasted_iota(I32, x.shape, 0)
    x_m1 = jnp.where(row == 0, xp[HALO - 1:HALO, :], pltpu.roll(x, 1, 0))
    x_m2 = jnp.where(row == 0, xp[HALO - 2:HALO - 1, :], jnp.where(row == 1, xp[HALO - 1:HALO, :], pltpu.roll(x, 2, 0)))
    x_p1 = jnp.where(row == n - 1, xn[0:1, :], pltpu.roll(x, n - 1, 0))
    return x_m2 * w_ref[0:1, :] + x_m1 * w_ref[1:2, :] + x * w_ref[2:3, :] + x_p1 * w_ref[3:4, :] + b_ref[...]


def _lin_scan(a, b, rev):
    n = a.shape[0]
    row = lax.broadcasted_iota(I32, a.shape, 0)
    d = 1
    while d < n:
        if rev:
            a_s, b_s, valid = pltpu.roll(a, n - d, 0), pltpu.roll(b, n - d, 0), row < n - d
        else:
            a_s, b_s, valid = pltpu.roll(a, d, 0), pltpu.roll(b, d, 0), row >= d
        a_s = jnp.where(valid, a_s, 1.0)
        b_s = jnp.where(valid, b_s, 0.0)
        b = a * b_s + b
        a = a * a_s
        d *= 2
        yield
    return a, b


def _lru_kernel(*refs, rev, lpb, bsz):
    batch_refs, hf_ref, (cw_ref, cb_ref, wg_ref, bg_ref, lam_ref), o_ref, (h_scr,) = _split_scan_refs(
        refs, rev, bsz, 5, 1)
    j = pl.program_id(0)

    @pl.when(j == 0)
    def _():
        h_scr[...] = jnp.zeros_like(h_scr)

    def chunk(b):
        x_ref, xp_ref, xn_ref = batch_refs[b][:3]
        x = x_ref[...]
        n, w = x.shape
        xc = _short_conv(x, xp_ref, xn_ref, cw_ref, cb_ref, j, lpb, rev)
        yield
        gates = jnp.dot(xc.astype(BF16), wg_ref[...], preferred_element_type=F32) + bg_ref[...]
        yield
        r = _sigmoid(gates[:, :w])
        ig = _sigmoid(gates[:, w:])
        log_a = LRU_C * r * _log_sigmoid(lam_ref[...])
        a = jnp.exp(log_a)
        bb = jnp.sqrt(-jnp.tanh(log_a) * (1.0 + a * a)) * ig * xc
        yield
        a_cum, h_loc = yield from _lin_scan(a, bb, rev)
        h = a_cum * h_scr[b] + h_loc
        h_scr[b] = h[0:1, :] if rev else h[n - 1:n, :]
        if rev:
            g_ref = batch_refs[b][3]
            o_ref[b] = ((hf_ref[b] + h) * jax.nn.gelu(g_ref[...], approximate=True)).astype(o_ref.dtype)
        else:
            o_ref[b] = h

    _round_robin([chunk(b) for b in range(bsz)])


def _lru_dir(u_a, hf, cw, cb, wg, bg, lam, geo, rev):
    w = LRU_WIDTH
    weights = [cw, cb.reshape(1, w), wg, bg.reshape(1, 2 * w), lam.reshape(1, w)]
    scratch = [pltpu.VMEM((geo["batch"], 1, w), F32)]
    return _scan_call(_lru_kernel, "rglru_bwd" if rev else "rglru_fwd", u_a, w, hf, weights, scratch, geo, rev)


def _mlstm_kernel(*refs, rev, lpb, bsz):
    n_weights = 12 if rev else 10
    batch_refs, hf_ref, weights, o_ref, (c_scr, m_scr) = _split_scan_refs(refs, rev, bsz, n_weights, 2)
    j = pl.program_id(0)

    @pl.when(j == 0)
    def _():
        c_scr[...] = jnp.zeros_like(c_scr)
        m_scr[...] = jnp.zeros_like(m_scr)

    _round_robin([_mlstm_chunk(batch_refs[b], hf_ref, weights, o_ref, c_scr, m_scr, b, j, rev, lpb)
                  for b in range(bsz)])


def _round_robin(stage_generators):
    live = list(stage_generators)
    while live:
        live = [g for g in live if next(g, StopIteration) is not StopIteration]


def _mlstm_chunk(in_refs, hf_ref, weights, o_ref, c_scr, m_scr, b, j, rev, lpb):
    if rev:
        x_ref, xp_ref, xn_ref, z_ref = in_refs
        cw_ref, cb_ref, wq_ref, wk_ref, wkt_ref, wv_ref, wg_ref, wgt_ref, bg_ref, bgt_ref, ng_ref, sk_ref = weights
    else:
        x_ref, xp_ref, xn_ref = in_refs
        cw_ref, cb_ref, wq_ref, wk_ref, wkt_ref, wv_ref, wg_ref, wgt_ref, bg_ref, bgt_ref = weights
    x = x_ref[...]
    n = x.shape[0]
    xc = _short_conv(x, xp_ref, xn_ref, cw_ref, cb_ref, j, lpb, rev)
    xc = xc * _sigmoid(xc)
    xcb = xc.astype(BF16)
    yield
    q = jnp.dot(xcb, wq_ref[...], preferred_element_type=F32)
    k = jnp.dot(xcb, wk_ref[...], preferred_element_type=F32)
    yield
    kt =lax.dot_general(wkt_ref[...], xcb, NT_DIMS, preferred_element_type=F32)
    lane_w = lax.broadcasted_iota(I32, (1, ML_PAD), 1)
    ones_lane = (lane_w % LANES == ML_HEAD_DIM).astype(F32)
    v = jnp.dot(x.astype(BF16), wv_ref[...], preferred_element_type=F32) + ones_lane
    yield
    qkv =jnp.concatenate([q, k, v], axis=1).astype(BF16)
    g_col = jnp.dot(qkv, wg_ref[...], preferred_element_type=F32) + bg_ref[...]
    g_row = lax.dot_general(wgt_ref[...], qkv, NT_DIMS, preferred_element_type=F32) + bgt_ref[...]
    yield
    ti =lax.broadcasted_iota(I32, (n, n), 0)
    si = lax.broadcasted_iota(I32, (n, n), 1)
    mask = (si >= ti) if rev else (si <= ti)
    tri = mask.astype(BF16)
    tri_t = ((ti >= si) if rev else (ti <= si)).astype(BF16)
    b_col = sum(jnp.dot(tri, part, preferred_element_type=F32) for part in _split3(_log_sigmoid(g_col)))
    b_row = sum(jnp.dot(part, tri_t, preferred_element_type=F32) for part in _split3(_log_sigmoid(g_row)))
    yield
    lane =lax.broadcasted_iota(I32, (1, LANES), 1)
    num_mask = (lane < ML_HEAD_DIM).astype(F32)
    den_mask = (lane == ML_HEAD_DIM).astype(F32)
    last = 0 if rev else n - 1
    outs = []
    for h in range(ML_HEADS):
        hs = slice(h * LANES, (h + 1) * LANES)
        bc = b_col[:, ML_HEADS + h:ML_HEADS + h + 1]
        br = b_row[ML_HEADS + h:ML_HEADS + h + 1, :]
        ic = g_col[:, h:h + 1]
        ir = g_row[h:h + 1, :]
        m_prev = m_scr[b, h:h + 1, 0:1]
        dmat = jnp.where(mask, bc - br + ir, -jnp.inf)
        inter = bc + m_prev
        m_t = jnp.maximum(inter, jnp.max(dmat, axis=1, keepdims=True))
        yield
        s =jnp.dot(q[:, hs].astype(BF16), kt[hs, :].astype(BF16), preferred_element_type=F32)
        p = s * jnp.exp(dmat - m_t)
        yield
        w_inter =jnp.exp(inter - m_t)
        vh = v[:, hs]
        c_old = c_scr[b * ML_HEADS + h]
        numden = (jnp.dot(p.astype(BF16), vh.astype(BF16), preferred_element_type=F32)
                  + w_inter * jnp.dot(q[:, hs].astype(BF16), c_old.astype(BF16), preferred_element_type=F32))
        den = jnp.sum(numden * den_mask, axis=1, keepdims=True)
        hh = numden * num_mask / jnp.maximum(jnp.abs(den), jnp.exp(-m_t))
        yield
        b_last =bc[last:last + 1, :]
        ws_col = b_last - bc + ic
        m_new = jnp.maximum(b_last + m_prev, jnp.max(ws_col, axis=0, keepdims=True))
        decay = jnp.exp(b_last + m_prev - m_new)
        wv = (jnp.exp(ws_col - m_new) * vh).astype(BF16)
        c_scr[b * ML_HEADS + h] = decay * c_old + jnp.dot(kt[hs, :].astype(BF16), wv, preferred_element_type=F32)
        m_scr[b, h:h + 1, :] = jnp.broadcast_to(m_new, (1, LANES))
        yield
        if rev:
            hsum = hf_ref[b, :, hs] + hh
            mu = jnp.sum(hsum, axis=1, keepdims=True) * (1.0 / ML_HEAD_DIM)
            cen = (hsum - mu) * num_mask
            var = jnp.sum(cen * cen, axis=1, keepdims=True) * (1.0 / ML_HEAD_DIM)
            hh = cen * lax.rsqrt(var + EPS)
        outs.append(hh)
    hcat = jnp.concatenate(outs, axis=1)
    if rev:
        z = z_ref[...]
        o_ref[b] = ((hcat * ng_ref[...] + sk_ref[...] * xc) * (z * _sigmoid(z))).astype(o_ref.dtype)
    else:
        o_ref[b] = hcat


def _mlstm_dir(u_m, hf, wts, geo, rev):
    names = ["cw", "cb", "wq", "wk", "wkt", "wv", "wg", "wgt", "bg", "bgt"] + (["ng", "sk"] if rev else [])
    bsz = geo["batch"]
    scratch = [pltpu.VMEM((bsz * ML_HEADS, LANES, LANES), F32), pltpu.VMEM((bsz, SUBLANES, LANES), F32)]
    return _scan_call(_mlstm_kernel, "mlstm_bwd" if rev else "mlstm_fwd", u_m, ML_PAD, hf, [wts[nm] for nm in names],
                      scratch, geo, rev)


def _mla_proj_kernel(ub_ref, cos_ref, sin_ref, gq_ref, w1_ref, w2_ref, gkv_ref, wk_ref, wv_ref, q_ref, k_ref, v_ref):
    ub = ub_ref[...]
    cos = cos_ref[...]
    sin = sin_ref[...]
    qn = _rms(ub[:, :MLA_Q_RANK], gq_ref[...]).astype(BF16)
    qa = jnp.dot(qn, w1_ref[...], preferred_element_type=F32)
    qb = jnp.dot(qn, w2_ref[...], preferred_element_type=F32)
    kvn = _rms(ub[:, MLA_Q_RANK:MLA_Q_RANK + MLA_KV_RANK], gkv_ref[...]).astype(BF16)
    kn = jnp.dot(kvn, wk_ref[...], preferred_element_type=F32)
    lane = lax.broadcasted_iota(I32, (1, LANES), 1)
    ones_lane = (lane == MLA_V).astype(F32)
    vn = jnp.dot(kvn, wv_ref[...], preferred_element_type=F32)
    off = MLA_Q_RANK + MLA_KV_RANK
    kr = ub[:, off:off + LANES] * cos + ub[:, off + LANES:off + 2 * LANES] * sin
    for h in range(MLA_HEADS):
        hs = slice(h * LANES, (h + 1) * LANES)
        q_ref[0, h] = (qa[:, hs] * cos + qb[:, hs] * sin).astype(BF16)
        k_ref[0, h] = (kn[:, hs] + kr).astype(BF16)
        v_ref[0, h] = (vn[:, hs] + ones_lane).astype(BF16)


def _mla_proj(u_b, cos_t, sin_t, wts, geo):
    n_tiles = geo["t_all"] // TILE
    n_lat, lpb, bsz = geo["n_lat"], geo["lpb"], geo["batch"]
    hw = MLA_HEADS * LANES

    def batch_of(i):
        return jnp.where(i < n_lat, i // lpb, i - n_lat)

    def blk_of(i):
        return jnp.where(i < n_lat, i % lpb, lpb)

    const = lambda i: (0, 0)
    head_spec = pl.BlockSpec((1, MLA_HEADS, TILE, LANES), lambda i: (batch_of(i), 0, blk_of(i), 0))
    head_shape = jax.ShapeDtypeStruct((bsz, MLA_HEADS, (lpb + 1) * TILE, LANES), BF16)
    return pl.pallas_call(
        _mla_proj_kernel,
        grid=(n_tiles,),
        in_specs=[
            pl.BlockSpec((TILE, UB_W), lambda i: (i, 0)),
            pl.BlockSpec((TILE, LANES), lambda i: (blk_of(i), 0)),
            pl.BlockSpec((TILE, LANES), lambda i: (blk_of(i), 0)),
            pl.BlockSpec((1, MLA_Q_RANK), const),
            pl.BlockSpec((MLA_Q_RANK, hw), const),
            pl.BlockSpec((MLA_Q_RANK, hw), const),
            pl.BlockSpec((1, MLA_KV_RANK), const),
            pl.BlockSpec((MLA_KV_RANK, hw), const),
            pl.BlockSpec((MLA_KV_RANK, hw), const),
        ],
        out_specs=[head_spec, head_spec, head_spec],
        out_shape=[head_shape, head_shape, head_shape],
        compiler_params=_params("arbitrary"),
        name="mla_proj",
    )(u_b, cos_t, sin_t, wts["gq"], wts["w1"], wts["w2"], wts["gkv"], wts["wk"], wts["wv"])


def _flash_kernel(q_ref, k_ref, v_ref, o_ref, m_scr, acc_scr, *, k_start, nk, tk):
    lane = lax.broadcasted_iota(I32, (1, LANES), 1)
    den_mask = (lane == MLA_V).astype(F32)
    outs = []
    for j in range(2):
        q = q_ref[0, j]
        m_scr[...] = jnp.full_like(m_scr, -jnp.inf)
        acc_scr[...] = jnp.zeros_like(acc_scr)

        def body(i, carry, j=j, q=q):
            start = k_start + i * tk
            kk = k_ref[0, j, start:start + tk, :]
            vv = v_ref[0, j, start:start + tk, :]
            s = lax.dot_general(q, kk, NT_DIMS, preferred_element_type=F32)
            cols = [s[:, c * LANES:(c + 1) * LANES] for c in range(tk // LANES)]
            mp = cols[0]
            for sc in cols[1:]:
                mp = jnp.maximum(mp, sc)
            m_old = m_scr[...]
            m_new = jnp.maximum(m_old, jnp.broadcast_to(jnp.max(mp, axis=1, keepdims=True), mp.shape))
            p = jnp.concatenate([jnp.exp2(sc - m_new).astype(BF16) for sc in cols], axis=1)
            acc_scr[...] = jnp.exp2(m_old - m_new) * acc_scr[...] + jnp.dot(p, vv, preferred_element_type=F32)
            m_scr[...] = m_new
            return carry

        for i in range(nk):
            body(i, 0)
        acc = acc_scr[...]
        den = jnp.sum(acc * den_mask, axis=1, keepdims=True)
        outs.append(acc / den)
    o_ref[...] = jnp.where(lane < MLA_V, outs[0], pltpu.roll(outs[1], MLA_V, 1)).astype(o_ref.dtype)


def _key_tile(n):
    for cand in range(min(n, 1024) // LANES * LANES, 0, -LANES):
        if n % cand == 0:
            return cand
    raise ValueError(n)


def _flash(q, k, v, tq, q_blk0, nq, k_start, k_len, out_blk0, t_all):
    bsz, heads, rows, _ = q.shape
    tk = _key_tile(k_len)
    kern = functools.partial(_flash_kernel, k_start=k_start, nk=k_len // tk, tk=tk)
    return pl.pallas_call(
        kern,
        grid=(bsz, heads // 2, nq),
        in_specs=[
            pl.BlockSpec((1, 2, tq, LANES), lambda b, h, i: (b, h, q_blk0 + i, 0)),
            pl.BlockSpec((1, 2, rows, LANES), lambda b, h, i: (b, h, 0, 0)),
            pl.BlockSpec((1, 2, rows, LANES), lambda b, h, i: (b, h, 0, 0)),
        ],
        out_specs=pl.BlockSpec((tq, LANES), lambda b, h, i: (out_blk0 + b * nq + i, h)),
        out_shape=jax.ShapeDtypeStruct((t_all, heads * MLA_V), BF16),
        scratch_shapes=[pltpu.VMEM((tq, LANES), F32), pltpu.VMEM((tq, LANES), F32)],
        compiler_params=_params("arbitrary", "arbitrary", "arbitrary"),
        name="mla_attention",
    )(q, k, v)


def _out_kernel(ya_ref, yb_ref, yc_ref, x_ref, mod_ref, w_ref, g_ref, wr_ref, br_ref,
                xo_ref, h2_ref, te_ref, tg_ref, rk_ref, cnt_ref, *, n_lat, lat_per_batch, ctx_row):
    i = pl.program_id(0)

    @pl.when(i == 0)
    def _():
        cnt_ref[...] = jnp.zeros_like(cnt_ref)

    r = _mod_row(i, n_lat, lat_per_batch, ctx_row)
    g1 = mod_ref[2, pl.ds(r, 1), :]
    sh2 = mod_ref[3, pl.ds(r, 1), :]
    sc2 = mod_ref[4, pl.ds(r, 1), :]
    y = jnp.concatenate([ya_ref[0], yb_ref[...], yc_ref[0]], axis=1)
    x1 = x_ref[...] + g1 * jnp.dot(y, w_ref[...], preferred_element_type=F32)
    xo_ref[...] = x1
    h2 = _rms(x1, g_ref[...]) * (1.0 + sc2) + sh2
    h2_ref[...] = h2.reshape(h2_ref.shape)
    h_hi, h_mid, _ = _split3(h2)
    logits = (jnp.dot(h_hi, wr_ref[0], preferred_element_type=F32)
              + (jnp.dot(h_mid, wr_ref[0], preferred_element_type=F32)
                 + jnp.dot(h_hi, wr_ref[1], preferred_element_type=F32))) + br_ref[...]
    n, ne = logits.shape
    lane_e = lax.broadcasted_iota(I32, (n, ne), 1).astype(F32)
    lane_o = lax.broadcasted_iota(I32, (n, LANES), 1)
    vals, idxs = [], []
    for _ in range(TOP_K):
        m = jnp.max(logits, axis=1, keepdims=True)
        idx = jnp.min(jnp.where(logits == m, lane_e, float(ne)), axis=1, keepdims=True)
        logits = jnp.where(lane_e == idx, -jnp.inf, logits)
        vals.append(m)
        idxs.append(idx)
    exps = [jnp.exp(vv - vals[0]) for vv in vals]
    tot = exps[0] + exps[1] + exps[2] + exps[3]
    hits = [(lane_e == idx).astype(F32) for idx in idxs]
    chosen = hits[0] + hits[1] + hits[2] + hits[3]
    ti = lax.broadcasted_iota(I32, (n, n), 0)
    si = lax.broadcasted_iota(I32, (n, n), 1)
    before = jnp.dot((si < ti).astype(BF16), chosen.astype(BF16), preferred_element_type=F32) + cnt_ref[0:1, :]
    cnt_ref[...] = cnt_ref[...] + jnp.sum(chosen, axis=0, keepdims=True)
    te = jnp.zeros((n, LANES), F32)
    tg = jnp.zeros((n, LANES), F32)
    rk = jnp.zeros((n, LANES), F32)
    for kk in range(TOP_K):
        te = jnp.where(lane_o == kk, idxs[kk], te)
        tg = jnp.where(lane_o == kk, exps[kk] / tot, tg)
        rk = jnp.where(lane_o == kk, jnp.sum(hits[kk] * before, axis=1, keepdims=True), rk)
    te_ref[...] = te.astype(I32)
    tg_ref[...] = tg
    rk_ref[...] = rk.astype(I32)


def _out_proj(ya, yb, yc, x_all, mod_l, w_out_p, g, w_router, b_router, geo, n_tiles):
    d = x_all.shape[1]
    rows = n_tiles * TILE
    kern = functools.partial(_out_kernel, n_lat=geo["n_lat"], lat_per_batch=geo["lpb"], ctx_row=geo["ctx_row"])
    row_blk = lambda i: (i, 0)
    const = lambda i: (0, 0)
    n_lat, lpb = geo["n_lat"], geo["lpb"]
    scan_blk = lambda i: (jnp.where(i < n_lat, i // lpb, i - n_lat), jnp.where(i < n_lat, i % lpb, lpb), 0)
    w_router = jnp.pad(w_router, ((0, 0), (0, LANES - N_EXPERTS)))
    w_router = jnp.stack(_split3(w_router)[:2])
    b_router = jnp.pad(b_router, (0, LANES - N_EXPERTS), constant_values=-jnp.inf)
    return pl.pallas_call(
        kern,
        grid=(n_tiles,),
        in_specs=[
            pl.BlockSpec((1, TILE, ya.shape[2]), scan_blk),
            pl.BlockSpec((TILE, yb.shape[1]), row_blk),
            pl.BlockSpec((1, TILE, yc.shape[2]), scan_blk),
            pl.BlockSpec((TILE, d), row_blk),
            pl.BlockSpec(mod_l.shape, lambda i: (0, 0, 0)),
            pl.BlockSpec(w_out_p.shape, const),
            pl.BlockSpec((1, d), const),
            pl.BlockSpec(w_router.shape, lambda i: (0, 0, 0)),
            pl.BlockSpec((1, LANES), const),
        ],
        out_specs=[pl.BlockSpec((TILE, d), row_blk), pl.BlockSpec((TILE, SUBLANES, d // SUBLANES), lambda i: (i, 0, 0)),
                   pl.BlockSpec((TILE, LANES), row_blk), pl.BlockSpec((TILE, LANES), row_blk),
                   pl.BlockSpec((TILE, LANES), row_blk), pl.BlockSpec((SUBLANES, LANES), const)],
        out_shape=[jax.ShapeDtypeStruct((rows, d), F32), jax.ShapeDtypeStruct((rows, SUBLANES, d // SUBLANES), F32),
                   jax.ShapeDtypeStruct((rows, LANES), I32), jax.ShapeDtypeStruct((rows, LANES), F32),
                   jax.ShapeDtypeStruct((rows, LANES), I32), jax.ShapeDtypeStruct((SUBLANES, LANES), F32)],
        compiler_params=_params("arbitrary"),
        name="out_proj_router",
    )(ya, yb, yc, x_all, mod_l, w_out_p, g.reshape(1, d), w_router, b_router.reshape(1, LANES))


def _dispatch_kernel(zrow_ref, nu_ref, dest_ref, h_ref, xs_hbm, zbuf, zsem, sem):
    i = pl.program_id(0)
    n_tok = h_ref.shape[0]
    n_blocks = xs_hbm.shape[0] // MOE_BLOCK

    def zero_copy(blk_row):
        row = pl.multiple_of(blk_row, MOE_BLOCK)
        return pltpu.make_async_copy(zbuf, xs_hbm.at[pl.ds(row, MOE_BLOCK)], zsem)

    @pl.when(i == 0)
    def _():
        zbuf[...] = jnp.zeros_like(zbuf)
        for start in (True, False):
            for e in range(N_EXPERTS):
                @pl.when(zrow_ref[e] >= 0)
                def _(e=e, start=start):
                    zero_copy(zrow_ref[e]).start() if start else zero_copy(zrow_ref[e]).wait()

            def tail(b, carry, start=start):
                zero_copy(b * MOE_BLOCK).start() if start else zero_copy(b * MOE_BLOCK).wait()
                return carry
            lax.fori_loop(nu_ref[0], n_blocks, tail, 0)

    for kk in range(TOP_K):
        def issue(t, carry, kk=kk):
            pltpu.make_async_copy(h_ref.at[t], xs_hbm.at[dest_ref[0, 0, kk * n_tok + t]], sem).start()
            return carry
        lax.fori_loop(0, n_tok, issue, 0, unroll=16)
    for _ in range(TOP_K):
        pltpu.make_async_copy(h_ref, xs_hbm.at[pl.ds(0, n_tok)], sem).wait()


def _moe_dispatch(h2, dest, zrow, n_used, n_slots):
    t_moe, tile_shape = dest.shape[0], h2.shape[1:]
    n_tiles = t_moe // TILE
    dest3 = dest.reshape(n_tiles, TILE, TOP_K).transpose(0, 2, 1).reshape(n_tiles, 1, TOP_K * TILE)
    grid_spec = pltpu.PrefetchScalarGridSpec(
        num_scalar_prefetch=2,
        grid=(n_tiles,),
        in_specs=[
            pl.BlockSpec((1, 1, TOP_K * TILE), lambda i, zr, nu: (i, 0, 0), memory_space=pltpu.SMEM),
            pl.BlockSpec((TILE,) + tile_shape, lambda i, zr, nu: (i, 0, 0)),
        ],
        out_specs=pl.BlockSpec(memory_space=pl.ANY),
        scratch_shapes=[pltpu.VMEM((MOE_BLOCK,) + tile_shape, F32), pltpu.SemaphoreType.DMA, pltpu.SemaphoreType.DMA],
    )
    return pl.pallas_call(
        _dispatch_kernel,
        grid_spec=grid_spec,
        out_shape=jax.ShapeDtypeStruct((n_slots,) + tile_shape, F32),
        compiler_params=_params("arbitrary"),
        name="moe_dispatch",
    )(zrow, n_used, dest3, h2)


def _moe_kernel(be_ref, nu_ref, slot_ref, nxt_ref, x_ref, wgu_hbm, bgu_ref, wd_hbm, bd_ref, y_ref,
                wgu_f32, wd_f32, wgu_bf, wd_bf, sem, *, layer):
    i = pl.program_id(0)

    def weight_copies(e, s):
        return (pltpu.make_async_copy(wgu_hbm.at[layer, e], wgu_f32.at[s], sem.at[0, s]),
                pltpu.make_async_copy(wd_hbm.at[layer, e], wd_f32.at[s], sem.at[1, s]))

    @pl.when(i < nu_ref[0])
    def _():
        e, s = be_ref[i], slot_ref[i]

        @pl.when(i == 0)
        def _():
            for cp in weight_copies(e, s):
                cp.start()

        @pl.when(jnp.logical_or(i == 0, e != be_ref[jnp.maximum(i - 1, 0)]))
        def _():
            for cp in weight_copies(e, s):
                cp.wait()

            @pl.when(nxt_ref[i] >= 0)
            def _():
                for cp in weight_copies(nxt_ref[i], 1 - s):
                    cp.start()

            wgu_bf[...] = wgu_f32[s].astype(BF16)
            wd_bf[...] = wd_f32[s].astype(BF16)

        x = x_ref[...].reshape(x_ref.shape[0], -1).astype(BF16)
        gu = jnp.dot(x, wgu_bf[...], preferred_element_type=F32) + bgu_ref[0, 0]
        glu = jnp.minimum(gu[:, :D_EXPERT], SWIGLU_LIMIT)
        lin = jnp.clip(gu[:, D_EXPERT:], -SWIGLU_LIMIT, SWIGLU_LIMIT)
        act = glu * _sigmoid(SWIGLU_ALPHA * glu) * (lin + 1.0)
        y = jnp.dot(act.astype(BF16), wd_bf[...], preferred_element_type=F32) + bd_ref[0, 0]
        y_ref[...] = y.reshape(y_ref.shape)

    @pl.when(i >= nu_ref[0])
    def _():
        y_ref[...] = jnp.zeros_like(y_ref)


def _moe_experts(x_sorted, block_e, n_used, w_slot, next_e, w_gu, b_gu, w_down, b_down, layer):
    n_slots, tile_shape = x_sorted.shape[0], x_sorted.shape[1:]
    n_blocks = n_slots // MOE_BLOCK
    depth, d = w_gu.shape[0], w_gu.shape[2]
    slot_blk = pl.BlockSpec((MOE_BLOCK,) + tile_shape, lambda i, be, nu, sl, nx: (i, 0, 0))

    def expert(i, be, nu, sl, nx):
        return (layer, be[jnp.minimum(i, nu[0] - 1)], 0, 0)

    grid_spec = pltpu.PrefetchScalarGridSpec(
        num_scalar_prefetch=4,
        grid=(n_blocks,),
        in_specs=[
            slot_blk,
            pl.BlockSpec(memory_space=pl.ANY),
            pl.BlockSpec((1, 1, 1, 2 * D_EXPERT), expert),
            pl.BlockSpec(memory_space=pl.ANY),
            pl.BlockSpec((1, 1, 1, d), expert),
        ],
        out_specs=slot_blk,
        scratch_shapes=[pltpu.VMEM((2, d, 2 * D_EXPERT), F32), pltpu.VMEM((2, D_EXPERT, d), F32),
                        pltpu.VMEM((d, 2 * D_EXPERT), BF16), pltpu.VMEM((D_EXPERT, d), BF16),
                        pltpu.SemaphoreType.DMA((2, 2))],
    )
    return pl.pallas_call(
        functools.partial(_moe_kernel, layer=layer),
        grid_spec=grid_spec,
        out_shape=jax.ShapeDtypeStruct((n_slots,) + tile_shape, F32),
        compiler_params=_params("arbitrary"),
        name="moe_experts",
    )(block_e, n_used, w_slot, next_e, x_sorted, w_gu, b_gu.reshape(depth, N_EXPERTS, 1, -1), w_down,
      b_down.reshape(depth, N_EXPERTS, 1, -1))


def _comb_kernel(inv_ref, inv_next_ref, y_hbm, tg_ref, x_ref, mod_ref, fg_ref, o_ref, cbuf, sem,
                 *, n_lat, lat_per_batch, ctx_row, final):
    i = pl.program_id(0)
    n_steps = pl.num_programs(0)
    slot = i % 2
    rows = cbuf.shape[1]
    per_tile = TILE // COMB_TILE

    def issue(idx_ref, dst_slot):
        def body(r, carry):
            pltpu.make_async_copy(y_hbm.at[idx_ref[0, 0, r]], cbuf.at[dst_slot, r], sem.at[dst_slot]).start()
            return carry
        lax.fori_loop(0, rows, body, 0, unroll=16)

    @pl.when(i == 0)
    def _():
        issue(inv_ref, 0)

    @pl.when(i + 1 < n_steps)
    def _():
        issue(inv_next_ref, 1 - slot)

    pltpu.make_async_copy(y_hbm.at[pl.ds(0, rows)], cbuf.at[slot], sem.at[slot]).wait()

    r = _mod_row(i // per_tile, n_lat, lat_per_batch, ctx_row)
    g2 = mod_ref[5, pl.ds(r, 1), :]
    tg = tg_ref[...]
    f = None
    for kk in range(TOP_K):
        rows_k = cbuf[slot, kk * COMB_TILE:(kk + 1) * COMB_TILE].reshape(COMB_TILE, -1)
        f = tg[:, kk:kk + 1] * rows_k if f is None else f + tg[:, kk:kk + 1] * rows_k
    x2 = x_ref[...] + g2 * f
    o_ref[...] = _rms(x2, fg_ref[...]) if final else x2


def _moe_combine(y_sorted, dest, tg, x_all, mod_l, final_g, geo, final):
    t_moe = dest.shape[0]
    d = x_all.shape[1]
    n_steps = t_moe // COMB_TILE
    inv3 = dest.reshape(n_steps, COMB_TILE, TOP_K).transpose(0, 2, 1).reshape(n_steps, 1, TOP_K * COMB_TILE)
    kern = functools.partial(_comb_kernel, n_lat=geo["n_lat"], lat_per_batch=geo["lpb"], ctx_row=geo["ctx_row"],
                             final=final)
    return pl.pallas_call(
        kern,
        grid=(n_steps,),
        in_specs=[
            pl.BlockSpec((1, 1, TOP_K * COMB_TILE), lambda i: (i, 0, 0), memory_space=pltpu.SMEM),
            pl.BlockSpec((1, 1, TOP_K * COMB_TILE), lambda i: (jnp.minimum(i + 1, n_steps - 1), 0, 0),
                         memory_space=pltpu.SMEM),
            pl.BlockSpec(memory_space=pl.ANY),
            pl.BlockSpec((COMB_TILE, LANES), lambda i: (i, 0)),
            pl.BlockSpec((COMB_TILE, d), lambda i: (i, 0)),
            pl.BlockSpec(mod_l.shape, lambda i: (0, 0, 0)),
            pl.BlockSpec((1, d), lambda i: (0, 0)),
        ],
        out_specs=pl.BlockSpec((COMB_TILE, d), lambda i: (i, 0)),
        out_shape=jax.ShapeDtypeStruct((t_moe, d), F32),
        scratch_shapes=[pltpu.VMEM((2, TOP_K * COMB_TILE) + y_sorted.shape[1:], F32), pltpu.SemaphoreType.DMA((2,))],
        compiler_params=_params("arbitrary"),
        name="moe_combine",
    )(inv3, inv3, y_sorted, tg, x_all, mod_l, final_g.reshape(1, d))


def _route(te, rk, cnt, t_moe):
    counts = cnt[0, :N_EXPERTS].astype(I32)
    padded = (counts + MOE_BLOCK - 1) // MOE_BLOCK * MOE_BLOCK
    padded_end = jnp.cumsum(padded)
    padded_start = padded_end - padded
    experts = jnp.arange(N_EXPERTS, dtype=I32)
    e_tok = te[:t_moe, :TOP_K]
    dest = rk[:t_moe, :TOP_K] + jnp.sum(jnp.where(e_tok[..., None] == experts, padded_start, 0), axis=-1)
    n_blocks = -(-(t_moe * TOP_K) // MOE_BLOCK) + N_EXPERTS
    blk_start = jnp.arange(n_blocks, dtype=I32) * MOE_BLOCK
    block_e = jnp.minimum(jnp.sum((padded_end[None, :] <= blk_start[:, None]).astype(I32), axis=1), N_EXPERTS - 1)
    n_used = (padded_end[-1:] // MOE_BLOCK).astype(I32)
    zrow = jnp.where(counts > 0, padded_end - MOE_BLOCK, -1).astype(I32)
    nonempty = counts > 0
    ordinal = jnp.cumsum(nonempty.astype(I32)) - 1
    later = jnp.where(jnp.logical_and(nonempty[None, :], experts[None, :] > experts[:, None]), experts[None, :], N_EXPERTS)
    nxt = jnp.min(later, axis=1)
    nxt = jnp.where(nxt == N_EXPERTS, -1, nxt)
    w_slot = (jnp.take(ordinal, block_e) % 2).astype(I32)
    next_e = jnp.take(nxt, block_e).astype(I32)
    return dest.astype(I32), block_e, n_used, zrow, w_slot, next_e, n_blocks * MOE_BLOCK


def _block_diag_dense(w):
    g, i, j = w.shape
    out = jnp.zeros((g * i, g * j), w.dtype)
    for n in range(g):
        out = out.at[n * i:(n + 1) * i, n * j:(n + 1) * j].set(w[n])
    return out


def _pad_heads(w, heads, axis):
    shape = list(w.shape)
    shape[axis:axis + 1] = [heads, shape[axis] // heads]
    w = w.reshape(shape)
    pad = [(0, 0)] * w.ndim
    pad[axis + 1] = (0, LANES - shape[axis + 1])
    w = jnp.pad(w, pad)
    shape[axis:axis + 2] = [heads * LANES]
    return w.reshape(shape)


_ROPE_SWAP = np.concatenate([np.arange(8, 16), np.arange(0, 8), np.arange(24, 32), np.arange(16, 24)])


def _prep_in_weight(w_in):
    d = w_in.shape[0]
    o = np.cumsum([0, LRU_WIDTH, LRU_WIDTH, MLA_Q_RANK, MLA_KV_RANK, MLA_ROPE, ML_WIDTH, ML_WIDTH])
    a_xg = w_in[:, o[0]:o[2]]
    b_qkv = w_in[:, o[2]:o[4]]
    kr = w_in[:, o[4]:o[5]]
    z_nope = jnp.zeros((d, MLA_NOPE), w_in.dtype)
    z_tail = jnp.zeros((d, LANES - MLA_NOPE - MLA_ROPE), w_in.dtype)
    m_x = _pad_heads(w_in[:, o[5]:o[6]], ML_HEADS, 1)
    m_z = _pad_heads(w_in[:, o[6]:o[7]], ML_HEADS, 1)
    return jnp.concatenate([a_xg, b_qkv, z_nope, kr, z_tail, z_nope, kr[:, _ROPE_SWAP], z_tail, m_x, m_z],
                           axis=1).astype(BF16)


def _prep_out_weight(w_out):
    a = w_out[:LRU_WIDTH]
    b = w_out[LRU_WIDTH:LRU_WIDTH + MLA_HEADS * MLA_V]
    c = _pad_heads(w_out[LRU_WIDTH + MLA_HEADS * MLA_V:], ML_HEADS, 0)
    return jnp.concatenate([a, b, c], axis=0).astype(BF16)


def _prep_mla(q_norm_g, w_qb, kv_norm_g, w_kvb):
    scale = (MLA_NOPE + MLA_ROPE) ** -0.5 * math.log2(math.e)
    rq = w_qb.shape[0]
    wq = w_qb.reshape(rq, MLA_HEADS, MLA_NOPE + MLA_ROPE) * scale
    nope, rope = wq[..., :MLA_NOPE], wq[..., MLA_NOPE:]
    z32 = jnp.zeros((rq, MLA_HEADS, LANES - MLA_NOPE - MLA_ROPE), w_qb.dtype)
    w1 = jnp.concatenate([nope, rope, z32], axis=-1).reshape(rq, MLA_HEADS * LANES)
    w2 = jnp.concatenate([jnp.zeros_like(nope), rope[..., _ROPE_SWAP], z32], axis=-1).reshape(rq, MLA_HEADS * LANES)
    rk = w_kvb.shape[0]
    wkv = w_kvb.reshape(rk, MLA_HEADS, MLA_NOPE + MLA_V)
    z64 = jnp.zeros((rk, MLA_HEADS, LANES - MLA_NOPE), w_kvb.dtype)
    wk = jnp.concatenate([wkv[..., :MLA_NOPE], z64], axis=-1).reshape(rk, MLA_HEADS * LANES)
    wv = jnp.concatenate([wkv[..., MLA_NOPE:], z64], axis=-1).reshape(rk, MLA_HEADS * LANES)
    return {"gq": q_norm_g.reshape(1, -1), "w1": w1.astype(BF16), "w2": w2.astype(BF16),
            "gkv": kv_norm_g.reshape(1, -1), "wk": wk.astype(BF16), "wv": wv.astype(BF16)}


def _prep_mlstm(conv_w, conv_b, wq, wk, wv, w_gate_d, b_gate_d, norm_g, skip):
    def proj(w):
        return _pad_heads(_pad_heads(_block_diag_dense(w), ML_HEADS, 0), ML_HEADS, 1)

    wk_p = proj(wk) * (ML_HEAD_DIM ** -0.5)
    wg = jnp.concatenate([_pad_heads(w_gate_d[i * ML_WIDTH:(i + 1) * ML_WIDTH], ML_HEADS, 0) for i in range(3)], axis=0)
    ng = w_gate_d.shape[1]
    wg = jnp.pad(wg, ((0, 0), (0, LANES - ng)))
    b_gate_d = jnp.pad(b_gate_d, (0, LANES - ng))
    gate_rows = 2 * SUBLANES
    return {
        "cw": _pad_heads(conv_w, ML_HEADS, 1), "cb": _pad_heads(conv_b.reshape(1, -1), ML_HEADS, 1),
        "wq": proj(wq).astype(BF16), "wk": wk_p.astype(BF16), "wkt": wk_p.T.astype(BF16), "wv": proj(wv).astype(BF16),
        "wg": wg.astype(BF16), "wgt": wg.T[:gate_rows].astype(BF16),
        "bg": b_gate_d.reshape(1, -1), "bgt": b_gate_d[:gate_rows].reshape(-1, 1),
        "ng": _pad_heads(norm_g.reshape(1, -1), ML_HEADS, 1), "sk": _pad_heads(skip.reshape(1, -1), ML_HEADS, 1),
    }


def _rope_tables(seq, ctx_len):
    rows = seq // GRID_W
    row = jnp.repeat(jnp.arange(rows, dtype=I32), GRID_W)
    col = jnp.tile(jnp.arange(GRID_W, dtype=I32), rows)
    freqs = ROPE_BASE ** (-jnp.arange(ROPE_AXIS_FREQ, dtype=F32) / ROPE_AXIS_FREQ)
    ang_r, ang_c = row[:, None] * freqs, col[:, None] * freqs
    cos32 = jnp.concatenate([jnp.cos(ang_r), jnp.cos(ang_r), jnp.cos(ang_c), jnp.cos(ang_c)], axis=1)
    sin32 = jnp.concatenate([-jnp.sin(ang_r), jnp.sin(ang_r), -jnp.sin(ang_c), jnp.sin(ang_c)], axis=1)
    cos32 = jnp.concatenate([cos32, jnp.ones((ctx_len, MLA_ROPE), F32)], axis=0)
    sin32 = jnp.concatenate([sin32, jnp.zeros((ctx_len, MLA_ROPE), F32)], axis=0)
    n = seq + ctx_len
    tail = jnp.zeros((n, LANES - MLA_NOPE - MLA_ROPE), F32)
    cos_t = jnp.concatenate([jnp.ones((n, MLA_NOPE), F32), cos32, tail], axis=1)
    sin_t = jnp.concatenate([jnp.zeros((n, MLA_NOPE), F32), sin32, tail], axis=1)
    return cos_t, sin_t


def kernel(x, c, ctx, c_ctx, norm1_g, norm2_g, w_mod, b_mod, w_in, w_out, lru_conv_w, lru_conv_b, lru_wa, lru_ba,
           lru_wx, lru_bx, lru_lambda, mla_q_norm_g, mla_w_qb, mla_kv_norm_g, mla_w_kvb, ml_conv_w, ml_conv_b,
           ml_wq, ml_wk, ml_wv, ml_w_gate, ml_b_gate, ml_norm_g, ml_skip, w_router, b_router, w_gu, b_gu, w_down,
           b_down, final_g):
    bsz, seq, d = x.shape
    ctx_len = ctx.shape[1]
    depth = w_mod.shape[0]
    assert ctx_len == TILE and seq % ATTN_Q_TILE == 0 and bsz + 1 <= SUBLANES
    t_lat = bsz * seq
    t_all = t_lat + bsz * ctx_len
    geo = {"batch": bsz, "lpb": seq // TILE, "n_lat": t_lat // TILE, "ctx_row": bsz, "t_all": t_all}

    cv = jnp.zeros((SUBLANES, d), F32).at[:bsz].set(c).at[bsz].set(c_ctx)
    mod = _modulation(cv, w_mod, b_mod)
    cos_t, sin_t = _rope_tables(seq, ctx_len)
    x_all = jnp.concatenate([x.reshape(t_lat, d), ctx.reshape(bsz * ctx_len, d)], axis=0)

    out = None
    for l in range(depth):
        last = l == depth - 1
        u_a, u_b, u_m = _in_proj(x_all, mod[l], norm1_g[l], _prep_in_weight(w_in[l]), geo)

        hf = None
        for dd in range(2):
            wg = jnp.concatenate([_block_diag_dense(lru_wa[l, dd]), _block_diag_dense(lru_wx[l, dd])], axis=1)
            bg = jnp.concatenate([lru_ba[l, dd], lru_bx[l, dd]])
            hf = _lru_dir(u_a, hf, lru_conv_w[l], lru_conv_b[l], wg.astype(BF16), bg, lru_lambda[l, dd], geo, dd == 1)
        ya = hf

        q, k, v = _mla_proj(u_b, cos_t, sin_t, _prep_mla(mla_q_norm_g[l], mla_w_qb[l], mla_kv_norm_g[l], mla_w_kvb[l]), geo)
        tq = ATTN_Q_TILE
        yb = _flash(q, k, v, tq, 0, seq // tq, 0, seq + ctx_len, 0, t_lat)
        if not last:
            yb_c = _flash(q, k, v, TILE, seq // TILE, 1, seq, ctx_len, 0, bsz * ctx_len)
            yb = jnp.concatenate([yb, yb_c], axis=0)

        hf = None
        for dd in range(2):
            wts = _prep_mlstm(ml_conv_w[l], ml_conv_b[l], ml_wq[l], ml_wk[l], ml_wv[l], ml_w_gate[l, dd],
                              ml_b_gate[l, dd], ml_norm_g[l], ml_skip[l])
            hf = _mlstm_dir(u_m, hf, wts, geo, dd == 1)
        yc = hf

        n_tiles = (t_lat if last else t_all) // TILE
        x_mid, h2, te, tg, rk, cnt = _out_proj(ya, yb, yc, x_all, mod[l], _prep_out_weight(w_out[l]), norm2_g[l],
                                               w_router[l], b_router[l], geo, n_tiles)
        t_moe = n_tiles * TILE
        dest, block_e, n_used, zrow, w_slot, next_e, n_slots = _route(te, rk, cnt, t_moe)
        x_sorted = _moe_dispatch(h2, dest, zrow, n_used, n_slots)
        y_sorted = _moe_experts(x_sorted, block_e, n_used, w_slot, next_e, w_gu, b_gu, w_down, b_down, l)
        x_all = _moe_combine(y_sorted, dest, tg, x_mid, mod[l], final_g, geo, last)
        if last:
            out = x_all.reshape(bsz, seq, d)
    return out
```

```python
import functools
import math

import jax
import jax.numpy as jnp
import numpy as np
from jax import lax
from jax.experimental import pallas as pl
from jax.experimental.pallas import tpu as pltpu

F32 = jnp.float32
BF16 = jnp.bfloat16
I32 = jnp.int32
HIGHEST = lax.Precision.HIGHEST

LANES = 128
SUBLANES = 8
VMEM_LIMIT_BYTES = 56 * 1024 * 1024

GRID_W = 64
EPS = 1e-6
LRU_WIDTH = 256
LRU_C = 8.0
CONV_W = 4
MLA_HEADS = 8
MLA_NOPE = 64
MLA_ROPE = 32
MLA_V = 64
MLA_Q_RANK = 256
MLA_KV_RANK = 128
ROPE_AXIS_FREQ = MLA_ROPE // 4
ROPE_BASE = 10000.0
ML_HEADS = 4
ML_HEAD_DIM = 64
ML_WIDTH = ML_HEADS * ML_HEAD_DIM
ML_PAD = ML_HEADS * LANES
N_EXPERTS = 32
TOP_K = 4
D_EXPERT = 1024
SWIGLU_LIMIT = 7.0
SWIGLU_ALPHA = 1.702
MOE_BLOCK = 256

TILE = 256
HALO = SUBLANES
UB_W = MLA_Q_RANK + MLA_KV_RANK + 2 * LANES
COMB_TILE = 128
ATTN_Q_TILE = 1024

NT_DIMS = (((1,), (1,)), ((), ()))


def _params(*sem):
    return pltpu.CompilerParams(dimension_semantics=sem, vmem_limit_bytes=VMEM_LIMIT_BYTES)


def _sigmoid(x):
    return 1.0 / (1.0 + jnp.exp(-x))


def _log_sigmoid(x):
    return jnp.minimum(x, 0.0) - jnp.log1p(jnp.exp(-jnp.abs(x)))


def _split3(x):
    hi = x.astype(BF16)
    rest = x - hi.astype(F32)
    mid = rest.astype(BF16)
    return hi, mid, (rest - mid.astype(F32)).astype(BF16)


def _rms(x, g):
    return x * lax.rsqrt(jnp.mean(x * x, axis=-1, keepdims=True) + EPS) * g


def _mod_kernel(cv_ref, w_ref, b_ref, o_ref):
    cv = cv_ref[...]
    a = cv * _sigmoid(cv)
    o_ref[0, 0] = jnp.dot(a, w_ref[0], precision=HIGHEST, preferred_element_type=F32) + b_ref[0, 0]


def _modulation(cv, w_mod, b_mod):
    depth, d, _ = w_mod.shape
    rows = cv.shape[0]
    return pl.pallas_call(
        _mod_kernel,
        grid=(depth, 6),
        in_specs=[
            pl.BlockSpec((rows, d), lambda l, j: (0, 0)),
            pl.BlockSpec((1, d, d), lambda l, j: (l, 0, j)),
            pl.BlockSpec((1, 1, 1, d), lambda l, j: (l, j, 0, 0)),
        ],
        out_specs=pl.BlockSpec((1, 1, rows, d), lambda l, j: (l, j, 0, 0)),
        out_shape=jax.ShapeDtypeStruct((depth, 6, rows, d), F32),
        compiler_params=_params("arbitrary", "arbitrary"),
        name="modulation",
    )(cv, w_mod, b_mod.reshape(depth, 6, 1, d))


def _mod_row(i, n_lat, lat_per_batch, ctx_row):
    return jnp.where(i < n_lat, i // lat_per_batch, ctx_row)


def _in_kernel(x_ref, mod_ref, g_ref, w_ref, ua_ref, ub_ref, um_ref, *, n_lat, lat_per_batch, ctx_row):
    r = _mod_row(pl.program_id(0), n_lat, lat_per_batch, ctx_row)
    sh = mod_ref[0, pl.ds(r, 1), :]
    sc = mod_ref[1, pl.ds(r, 1), :]
    h = _rms(x_ref[...], g_ref[...]) * (1.0 + sc) + sh
    u = jnp.dot(h.astype(BF16), w_ref[...], preferred_element_type=F32)
    wa = ua_ref.shape[1]
    wb = ub_ref.shape[1]
    ua_ref[...] = u[:, :wa]
    ub_ref[...] = u[:, wa:wa + wb]
    um_ref[...] = u[:, wa + wb:]


def _in_proj(x_all, mod_l, g, w_in_p, geo):
    t_all, d = x_all.shape
    n_tiles = t_all // TILE
    wa, wb, wm = 2 * LRU_WIDTH, UB_W, 2 * ML_PAD
    kern = functools.partial(_in_kernel, n_lat=geo["n_lat"], lat_per_batch=geo["lpb"], ctx_row=geo["ctx_row"])
    return pl.pallas_call(
        kern,
        grid=(n_tiles,),
        in_specs=[
            pl.BlockSpec((TILE, d), lambda i: (i, 0)),
            pl.BlockSpec(mod_l.shape, lambda i: (0, 0, 0)),
            pl.BlockSpec((1, d), lambda i: (0, 0)),
            pl.BlockSpec(w_in_p.shape, lambda i: (0, 0)),
        ],
        out_specs=[
            pl.BlockSpec((TILE, wa), lambda i: (i, 0)),
            pl.BlockSpec((TILE, wb), lambda i: (i, 0)),
            pl.BlockSpec((TILE, wm), lambda i: (i, 0)),
        ],
        out_shape=[
            jax.ShapeDtypeStruct((t_all, wa), F32),
            jax.ShapeDtypeStruct((t_all, wb), F32),
            jax.ShapeDtypeStruct((t_all, wm), F32),
        ],
        compiler_params=_params("arbitrary"),
        name="in_proj",
    )(x_all, mod_l, g.reshape(1, d), w_in_p)


def _chunk_block(b, j, geo, rev):
    lat = (geo["lpb"] - j) if rev else (j - 1)
    return jnp.where(j == 0, geo["n_lat"] + b, b * geo["lpb"] + lat)


def _local_block(j, lpb, rev):
    return jnp.where(j == 0, lpb, (lpb - j) if rev else (j - 1))


def _chunk_specs(width, col, geo, rev, b):
    per = TILE // HALO
    last = geo["t_all"] // HALO - 1

    def cur(j):
        return (_chunk_block(b, j, geo, rev), col)

    def prev(j):
        return (jnp.maximum(_chunk_block(b, j, geo, rev) * per - 1, 0), col)

    def nxt(j):
        return (jnp.minimum((_chunk_block(b, j, geo, rev) + 1) * per, last), col)

    return [pl.BlockSpec((TILE, width), cur), pl.BlockSpec((HALO, width), prev), pl.BlockSpec((HALO, width), nxt)]


def _scan_call(kern, name, src, width, hf, weights, scratch, geo, rev):
    bsz, lpb = geo["batch"], geo["lpb"]
    chunk_of_all = pl.BlockSpec((bsz, TILE, width), lambda j: (0, _local_block(j, lpb, rev), 0))
    specs, args = [], []
    for b in range(bsz):
        specs += _chunk_specs(width, 0, geo, rev, b)
        args += [src, src, src]
        if rev:
            specs.append(pl.BlockSpec((TILE, width), lambda j, b=b: (_chunk_block(b, j, geo, rev), 1)))
            args.append(src)
    if rev:
        specs.append(chunk_of_all)
        args.append(hf)
    for wgt in weights:
        specs.append(pl.BlockSpec(wgt.shape, lambda j: (0, 0)))
        args.append(wgt)
    return pl.pallas_call(
        functools.partial(kern, rev=rev, lpb=lpb, bsz=bsz),
        grid=(lpb + 1,),
        in_specs=specs,
        out_specs=chunk_of_all,
        out_shape=jax.ShapeDtypeStruct((bsz, (lpb + 1) * TILE, width), BF16 if rev else F32),
        scratch_shapes=scratch,
        compiler_params=_params("arbitrary"),
        name=name,
    )(*args)


def _split_scan_refs(refs, rev, bsz, n_weights, n_scratch):
    n_in = 4 if rev else 3
    batch_refs = [refs[b * n_in:(b + 1) * n_in] for b in range(bsz)]
    w0 = bsz * n_in
    hf_ref = refs[w0] if rev else None
    w0 += 1 if rev else 0
    weights = refs[w0:w0 + n_weights]
    return batch_refs, hf_ref, weights, refs[w0 + n_weights], refs[len(refs) - n_scratch:]


def _short_conv(x, xp_ref, xn_ref, w_ref, b_ref, j, lpb, rev):
    n = x.shape[0]
    lat = (lpb - j) if rev else (j - 1)
    is_lat = j > 0
    prev_ok = jnp.logical_and(is_lat, lat > 0)
    next_ok = jnp.logical_and(is_lat, lat < lpb - 1)
    xp = xp_ref[...] * prev_ok.astype(F32)
    xn = xn_ref[...] * next_ok.astype(F32)
    row = lax.broadcasted_iota(I32, x.shape, 0)
    x_m1 = jnp.where(row == 0, xp[HALO - 1:HALO, :], pltpu.roll(x, 1, 0))
    x_m2 = jnp.where(row == 0, xp[HALO - 2:HALO - 1, :], jnp.where(row == 1, xp[HALO - 1:HALO, :], pltpu.roll(x, 2, 0)))
    x_p1 = jnp.where(row == n - 1, xn[0:1, :], pltpu.roll(x, n - 1, 0))
    return x_m2 * w_ref[0:1, :] + x_m1 * w_ref[1:2, :] + x * w_ref[2:3, :] + x_p1 * w_ref[3:4, :] + b_ref[...]


def _lin_scan(a, b, rev):
    n = a.shape[0]
    row = lax.broadcasted_iota(I32, a.shape, 0)
    d = 1
    while d < n:
        if rev:
            a_s, b_s, valid = pltpu.roll(a, n - d, 0), pltpu.roll(b, n - d, 0), row < n - d
        else:
            a_s, b_s, valid = pltpu.roll(a, d, 0), pltpu.roll(b, d, 0), row >= d
        a_s = jnp.where(valid, a_s, 1.0)
        b_s = jnp.where(valid, b_s, 0.0)
        b = a * b_s + b
        a = a * a_s
        d *= 2
        yield
    return a, b


def _lru_kernel(*refs, rev, lpb, bsz):
    batch_refs, hf_ref, (cw_ref, cb_ref, wg_ref, bg_ref, lam_ref), o_ref, (h_scr,) = _split_scan_refs(
        refs, rev, bsz, 5, 1)
    j = pl.program_id(0)

    @pl.when(j == 0)
    def _():
        h_scr[...] = jnp.zeros_like(h_scr)

    def chunk(b):
        x_ref, xp_ref, xn_ref = batch_refs[b][:3]
        x = x_ref[...]
        n, w = x.shape
        xc = _short_conv(x, xp_ref, xn_ref, cw_ref, cb_ref, j, lpb, rev)
        yield
        gates = jnp.dot(xc.astype(BF16), wg_ref[...], preferred_element_type=F32) + bg_ref[...]
        yield
        r = _sigmoid(gates[:, :w])
        ig = _sigmoid(gates[:, w:])
        log_a = LRU_C * r * _log_sigmoid(lam_ref[...])
        a = jnp.exp(log_a)
        bb = jnp.sqrt(-jnp.tanh(log_a) * (1.0 + a * a)) * ig * xc
        yield
        a_cum, h_loc = yield from _lin_scan(a, bb, rev)
        h = a_cum * h_scr[b] + h_loc
        h_scr[b] = h[0:1, :] if rev else h[n - 1:n, :]
        if rev:
            g_ref = batch_refs[b][3]
            o_ref[b] = ((hf_ref[b] + h) * jax.nn.gelu(g_ref[...], approximate=True)).astype(o_ref.dtype)
        else:
            o_ref[b] = h

    _round_robin([chunk(b) for b in range(bsz)])


def _lru_dir(u_a, hf, cw, cb, wg, bg, lam, geo, rev):
    w = LRU_WIDTH
    weights = [cw, cb.reshape(1, w), wg, bg.reshape(1, 2 * w), lam.reshape(1, w)]
    scratch = [pltpu.VMEM((geo["batch"], 1, w), F32)]
    return _scan_call(_lru_kernel, "rglru_bwd" if rev else "rglru_fwd", u_a, w, hf, weights, scratch, geo, rev)


def _mlstm_kernel(*refs, rev, lpb, bsz):
    n_weights = 12 if rev else 10
    batch_refs, hf_ref, weights, o_ref, (c_scr, m_scr) = _split_scan_refs(refs, rev, bsz, n_weights, 2)
    j = pl.program_id(0)

    @pl.when(j == 0)
    def _():
        c_scr[...] = jnp.zeros_like(c_scr)
        m_scr[...] = jnp.zeros_like(m_scr)

    _round_robin([_mlstm_chunk(batch_refs[b], hf_ref, weights, o_ref, c_scr, m_scr, b, j, rev, lpb)
                  for b in range(bsz)])


def _round_robin(stage_generators):
    live = list(stage_generators)
    while live:
        live = [g for g in live if next(g, StopIteration) is not StopIteration]


def _mlstm_chunk(in_refs, hf_ref, weights, o_ref, c_scr, m_scr, b, j, rev, lpb):
    if rev:
        x_ref, xp_ref, xn_ref, z_ref = in_refs
        cw_ref, cb_ref, wq_ref, wk_ref, wkt_ref, wv_ref, wg_ref, wgt_ref, bg_ref, bgt_ref, ng_ref, sk_ref = weights
    else:
        x_ref, xp_ref, xn_ref = in_refs
        cw_ref, cb_ref, wq_ref, wk_ref, wkt_ref, wv_ref, wg_ref, wgt_ref, bg_ref, bgt_ref = weights
    x = x_ref[...]
    n = x.shape[0]
    xc = _short_conv(x, xp_ref, xn_ref, cw_ref, cb_ref, j, lpb, rev)
    xc = xc * _sigmoid(xc)
    xcb = xc.astype(BF16)
    yield
    q = jnp.dot(xcb, wq_ref[...], preferred_element_type=F32)
    k = jnp.dot(xcb, wk_ref[...], preferred_element_type=F32)
    yield
    kt =lax.dot_general(wkt_ref[...], xcb, NT_DIMS, preferred_element_type=F32)
    lane_w = lax.broadcasted_iota(I32, (1, ML_PAD), 1)
    ones_lane = (lane_w % LANES == ML_HEAD_DIM).astype(F32)
    v = jnp.dot(x.astype(BF16), wv_ref[...], preferred_element_type=F32) + ones_lane
    yield
    qkv =jnp.concatenate([q, k, v], axis=1).astype(BF16)
    g_col = jnp.dot(qkv, wg_ref[...], preferred_element_type=F32) + bg_ref[...]
    g_row = lax.dot_general(wgt_ref[...], qkv, NT_DIMS, preferred_element_type=F32) + bgt_ref[...]
    yield
    ti =lax.broadcasted_iota(I32, (n, n), 0)
    si = lax.broadcasted_iota(I32, (n, n), 1)
    mask = (si >= ti) if rev else (si <= ti)
    tri = mask.astype(BF16)
    tri_t = ((ti >= si) if rev else (ti <= si)).astype(BF16)
    b_col = sum(jnp.dot(tri, part, preferred_element_type=F32) for part in _split3(_log_sigmoid(g_col)))
    b_row = sum(jnp.dot(part, tri_t, preferred_element_type=F32) for part in _split3(_log_sigmoid(g_row)))
    yield
    lane =lax.broadcasted_iota(I32, (1, LANES), 1)
    num_mask = (lane < ML_HEAD_DIM).astype(F32)
    den_mask = (lane == ML_HEAD_DIM).astype(F32)
    last = 0 if rev else n - 1
    outs = []
    for h in range(ML_HEADS):
        hs = slice(h * LANES, (h + 1) * LANES)
        bc = b_col[:, ML_HEADS + h:ML_HEADS + h + 1]
        br = b_row[ML_HEADS + h:ML_HEADS + h + 1, :]
        ic = g_col[:, h:h + 1]
        ir = g_row[h:h + 1, :]
        m_prev = m_scr[b, h:h + 1, 0:1]
        dmat = jnp.where(mask, bc - br + ir, -jnp.inf)
        inter = bc + m_prev
        m_t = jnp.maximum(inter, jnp.max(dmat, axis=1, keepdims=True))
        yield
        s =jnp.dot(q[:, hs].astype(BF16), kt[hs, :].astype(BF16), preferred_element_type=F32)
        p = s * jnp.exp(dmat - m_t)
        yield
        w_inter =jnp.exp(inter - m_t)
        vh = v[:, hs]
        c_old = c_scr[b * ML_HEADS + h]
        numden = (jnp.dot(p.astype(BF16), vh.astype(BF16), preferred_element_type=F32)
                  + w_inter * jnp.dot(q[:, hs].astype(BF16), c_old.astype(BF16), preferred_element_type=F32))
        den = jnp.sum(numden * den_mask, axis=1, keepdims=True)
        hh = numden * num_mask / jnp.maximum(jnp.abs(den), jnp.exp(-m_t))
        yield
        b_last =bc[last:last + 1, :]
        ws_col = b_last - bc + ic
        m_new = jnp.maximum(b_last + m_prev, jnp.max(ws_col, axis=0, keepdims=True))
        decay = jnp.exp(b_last + m_prev - m_new)
        wv = (jnp.exp(ws_col - m_new) * vh).astype(BF16)
        c_scr[b * ML_HEADS + h] = decay * c_old + jnp.dot(kt[hs, :].astype(BF16), wv, preferred_element_type=F32)
        m_scr[b, h:h + 1, :] = jnp.broadcast_to(m_new, (1, LANES))
        yield
        if rev:
            hsum = hf_ref[b, :, hs] + hh
            mu = jnp.sum(hsum, axis=1, keepdims=True) * (1.0 / ML_HEAD_DIM)
            cen = (hsum - mu) * num_mask
            var = jnp.sum(cen * cen, axis=1, keepdims=True) * (1.0 / ML_HEAD_DIM)
            hh = cen * lax.rsqrt(var + EPS)
        outs.append(hh)
    hcat = jnp.concatenate(outs, axis=1)
    if rev:
        z = z_ref[...]
        o_ref[b] = ((hcat * ng_ref[...] + sk_ref[...] * xc) * (z * _sigmoid(z))).astype(o_ref.dtype)
    else:
        o_ref[b] = hcat


def _mlstm_dir(u_m, hf, wts, geo, rev):
    names = ["cw", "cb", "wq", "wk", "wkt", "wv", "wg", "wgt", "bg", "bgt"] + (["ng", "sk"] if rev else [])
    bsz = geo["batch"]
    scratch = [pltpu.VMEM((bsz * ML_HEADS, LANES, LANES), F32), pltpu.VMEM((bsz, SUBLANES, LANES), F32)]
    return _scan_call(_mlstm_kernel, "mlstm_bwd" if rev else "mlstm_fwd", u_m, ML_PAD, hf, [wts[nm] for nm in names],
                      scratch, geo, rev)


def _mla_proj_kernel(ub_ref, cos_ref, sin_ref, gq_ref, w1_ref, w2_ref, gkv_ref, wk_ref, wv_ref, q_ref, k_ref, v_ref):
    ub = ub_ref[...]
    cos = cos_ref[...]
    sin = sin_ref[...]
    qn = _rms(ub[:, :MLA_Q_RANK], gq_ref[...]).astype(BF16)
    qa = jnp.dot(qn, w1_ref[...], preferred_element_type=F32)
    qb = jnp.dot(qn, w2_ref[...], preferred_element_type=F32)
    kvn = _rms(ub[:, MLA_Q_RANK:MLA_Q_RANK + MLA_KV_RANK], gkv_ref[...]).astype(BF16)
    kn = jnp.dot(kvn, wk_ref[...], preferred_element_type=F32)
    lane = lax.broadcasted_iota(I32, (1, LANES), 1)
    ones_lane = (lane == MLA_V).astype(F32)
    vn = jnp.dot(kvn, wv_ref[...], preferred_element_type=F32)
    off = MLA_Q_RANK + MLA_KV_RANK
    kr = ub[:, off:off + LANES] * cos + ub[:, off + LANES:off + 2 * LANES] * sin
    for h in range(MLA_HEADS):
        hs = slice(h * LANES, (h + 1) * LANES)
        q_ref[0, h] = (qa[:, hs] * cos + qb[:, hs] * sin).astype(BF16)
        k_ref[0, h] = (kn[:, hs] + kr).astype(BF16)
        v_ref[0, h] = (vn[:, hs] + ones_lane).astype(BF16)


def _mla_proj(u_b, cos_t, sin_t, wts, geo):
    n_tiles = geo["t_all"] // TILE
    n_lat, lpb, bsz = geo["n_lat"], geo["lpb"], geo["batch"]
    hw = MLA_HEADS * LANES

    def batch_of(i):
        return jnp.where(i < n_lat, i // lpb, i - n_lat)

    def blk_of(i):
        return jnp.where(i < n_lat, i % lpb, lpb)

    const = lambda i: (0, 0)
    head_spec = pl.BlockSpec((1, MLA_HEADS, TILE, LANES), lambda i: (batch_of(i), 0, blk_of(i), 0))
    head_shape = jax.ShapeDtypeStruct((bsz, MLA_HEADS, (lpb + 1) * TILE, LANES), BF16)
    return pl.pallas_call(
        _mla_proj_kernel,
        grid=(n_tiles,),
        in_specs=[
            pl.BlockSpec((TILE, UB_W), lambda i: (i, 0)),
            pl.BlockSpec((TILE, LANES), lambda i: (blk_of(i), 0)),
            pl.BlockSpec((TILE, LANES), lambda i: (blk_of(i), 0)),
            pl.BlockSpec((1, MLA_Q_RANK), const),
            pl.BlockSpec((MLA_Q_RANK, hw), const),
            pl.BlockSpec((MLA_Q_RANK, hw), const),
            pl.BlockSpec((1, MLA_KV_RANK), const),
            pl.BlockSpec((MLA_KV_RANK, hw), const),
            pl.BlockSpec((MLA_KV_RANK, hw), const),
        ],
        out_specs=[head_spec, head_spec, head_spec],
        out_shape=[head_shape, head_shape, head_shape],
        compiler_params=_params("arbitrary"),
        name="mla_proj",
    )(u_b, cos_t, sin_t, wts["gq"], wts["w1"], wts["w2"], wts["gkv"], wts["wk"], wts["wv"])


def _flash_kernel(q_ref, k_ref, v_ref, o_ref, m_scr, acc_scr, *, k_start, nk, tk):
    lane = lax.broadcasted_iota(I32, (1, LANES), 1)
    den_mask = (lane == MLA_V).astype(F32)
    outs = []
    for j in range(2):
        q = q_ref[0, j]
        m_scr[...] = jnp.full_like(m_scr, -jnp.inf)
        acc_scr[...] = jnp.zeros_like(acc_scr)

        def body(i, carry, j=j, q=q):
            start = k_start + i * tk
            kk = k_ref[0, j, start:start + tk, :]
            vv = v_ref[0, j, start:start + tk, :]
            s = lax.dot_general(q, kk, NT_DIMS, preferred_element_type=F32)
            cols = [s[:, c * LANES:(c + 1) * LANES] for c in range(tk // LANES)]
            mp = cols[0]
            for sc in cols[1:]:
                mp = jnp.maximum(mp, sc)
            m_old = m_scr[...]
            m_new = jnp.maximum(m_old, jnp.broadcast_to(jnp.max(mp, axis=1, keepdims=True), mp.shape))
            p = jnp.concatenate([jnp.exp2(sc - m_new).astype(BF16) for sc in cols], axis=1)
            acc_scr[...] = jnp.exp2(m_old - m_new) * acc_scr[...] + jnp.dot(p, vv, preferred_element_type=F32)
            m_scr[...] = m_new
            return carry

        for i in range(nk):
            body(i, 0)
        acc = acc_scr[...]
        den = jnp.sum(acc * den_mask, axis=1, keepdims=True)
        outs.append(acc / den)
    o_ref[...] = jnp.where(lane < MLA_V, outs[0], pltpu.roll(outs[1], MLA_V, 1)).astype(o_ref.dtype)


def _key_tile(n):
    for cand in range(min(n, 1024) // LANES * LANES, 0, -LANES):
        if n % cand == 0:
            return cand
    raise ValueError(n)


def _flash(q, k, v, tq, q_blk0, nq, k_start, k_len, out_blk0, t_all):
    bsz, heads, rows, _ = q.shape
    tk = _key_tile(k_len)
    kern = functools.partial(_flash_kernel, k_start=k_start, nk=k_len // tk, tk=tk)
    return pl.pallas_call(
        kern,
        grid=(bsz, heads // 2, nq),
        in_specs=[
            pl.BlockSpec((1, 2, tq, LANES), lambda b, h, i: (b, h, q_blk0 + i, 0)),
            pl.BlockSpec((1, 2, rows, LANES), lambda b, h, i: (b, h, 0, 0)),
            pl.BlockSpec((1, 2, rows, LANES), lambda b, h, i: (b, h, 0, 0)),
        ],
        out_specs=pl.BlockSpec((tq, LANES), lambda b, h, i: (out_blk0 + b * nq + i, h)),
        out_shape=jax.ShapeDtypeStruct((t_all, heads * MLA_V), BF16),
        scratch_shapes=[pltpu.VMEM((tq, LANES), F32), pltpu.VMEM((tq, LANES), F32)],
        compiler_params=_params("arbitrary", "arbitrary", "arbitrary"),
        name="mla_attention",
    )(q, k, v)


def _out_kernel(ya_ref, yb_ref, yc_ref, x_ref, mod_ref, w_ref, g_ref, wr_ref, br_ref,
                xo_ref, h2_ref, te_ref, tg_ref, rk_ref, cnt_ref, *, n_lat, lat_per_batch, ctx_row):
    i = pl.program_id(0)

    @pl.when(i == 0)
    def _():
        cnt_ref[...] = jnp.zeros_like(cnt_ref)

    r = _mod_row(i, n_lat, lat_per_batch, ctx_row)
    g1 = mod_ref[2, pl.ds(r, 1), :]
    sh2 = mod_ref[3, pl.ds(r, 1), :]
    sc2 = mod_ref[4, pl.ds(r, 1), :]
    y = jnp.concatenate([ya_ref[0], yb_ref[...], yc_ref[0]], axis=1)
    x1 = x_ref[...] + g1 * jnp.dot(y, w_ref[...], preferred_element_type=F32)
    xo_ref[...] = x1
    h2 = _rms(x1, g_ref[...]) * (1.0 + sc2) + sh2
    h2_ref[...] = h2.reshape(h2_ref.shape)
    h_hi, h_mid, _ = _split3(h2)
    logits = (jnp.dot(h_hi, wr_ref[0], preferred_element_type=F32)
              + (jnp.dot(h_mid, wr_ref[0], preferred_element_type=F32)
                 + jnp.dot(h_hi, wr_ref[1], preferred_element_type=F32))) + br_ref[...]
    n, ne = logits.shape
    lane_e = lax.broadcasted_iota(I32, (n, ne), 1).astype(F32)
    lane_o = lax.broadcasted_iota(I32, (n, LANES), 1)
    vals, idxs = [], []
    for _ in range(TOP_K):
        m = jnp.max(logits, axis=1, keepdims=True)
        idx = jnp.min(jnp.where(logits == m, lane_e, float(ne)), axis=1, keepdims=True)
        logits = jnp.where(lane_e == idx, -jnp.inf, logits)
        vals.append(m)
        idxs.append(idx)
    exps = [jnp.exp(vv - vals[0]) for vv in vals]
    tot = exps[0] + exps[1] + exps[2] + exps[3]
    hits = [(lane_e == idx).astype(F32) for idx in idxs]
    chosen = hits[0] + hits[1] + hits[2] + hits[3]
    ti = lax.broadcasted_iota(I32, (n, n), 0)
    si = lax.broadcasted_iota(I32, (n, n), 1)
    before = jnp.dot((si < ti).astype(BF16), chosen.astype(BF16), preferred_element_type=F32) + cnt_ref[0:1, :]
    cnt_ref[...] = cnt_ref[...] + jnp.sum(chosen, axis=0, keepdims=True)
    te = jnp.zeros((n, LANES), F32)
    tg = jnp.zeros((n, LANES), F32)
    rk = jnp.zeros((n, LANES), F32)
    for kk in range(TOP_K):
        te = jnp.where(lane_o == kk, idxs[kk], te)
        tg = jnp.where(lane_o == kk, exps[kk] / tot, tg)
        rk = jnp.where(lane_o == kk, jnp.sum(hits[kk] * before, axis=1, keepdims=True), rk)
    te_ref[...] = te.astype(I32)
    tg_ref[...] = tg
    rk_ref[...] = rk.astype(I32)


def _out_proj(ya, yb, yc, x_all, mod_l, w_out_p, g, w_router, b_router, geo, n_tiles):
    d = x_all.shape[1]
    rows = n_tiles * TILE
    kern = functools.partial(_out_kernel, n_lat=geo["n_lat"], lat_per_batch=geo["lpb"], ctx_row=geo["ctx_row"])
    row_blk = lambda i: (i, 0)
    const = lambda i: (0, 0)
    n_lat, lpb = geo["n_lat"], geo["lpb"]
    scan_blk = lambda i: (jnp.where(i < n_lat, i // lpb, i - n_lat), jnp.where(i < n_lat, i % lpb, lpb), 0)
    w_router = jnp.pad(w_router, ((0, 0), (0, LANES - N_EXPERTS)))
    w_router = jnp.stack(_split3(w_router)[:2])
    b_router = jnp.pad(b_router, (0, LANES - N_EXPERTS), constant_values=-jnp.inf)
    return pl.pallas_call(
        kern,
        grid=(n_tiles,),
        in_specs=[
            pl.BlockSpec((1, TILE, ya.shape[2]), scan_blk),
            pl.BlockSpec((TILE, yb.shape[1]), row_blk),
            pl.BlockSpec((1, TILE, yc.shape[2]), scan_blk),
            pl.BlockSpec((TILE, d), row_blk),
            pl.BlockSpec(mod_l.shape, lambda i: (0, 0, 0)),
            pl.BlockSpec(w_out_p.shape, const),
            pl.BlockSpec((1, d), const),
            pl.BlockSpec(w_router.shape, lambda i: (0, 0, 0)),
            pl.BlockSpec((1, LANES), const),
        ],
        out_specs=[pl.BlockSpec((TILE, d), row_blk), pl.BlockSpec((TILE, SUBLANES, d // SUBLANES), lambda i: (i, 0, 0)),
                   pl.BlockSpec((TILE, LANES), row_blk), pl.BlockSpec((TILE, LANES), row_blk),
                   pl.BlockSpec((TILE, LANES), row_blk), pl.BlockSpec((SUBLANES, LANES), const)],
        out_shape=[jax.ShapeDtypeStruct((rows, d), F32), jax.ShapeDtypeStruct((rows, SUBLANES, d // SUBLANES), F32),
                   jax.ShapeDtypeStruct((rows, LANES), I32), jax.ShapeDtypeStruct((rows, LANES), F32),
                   jax.ShapeDtypeStruct((rows, LANES), I32), jax.ShapeDtypeStruct((SUBLANES, LANES), F32)],
        compiler_params=_params("arbitrary"),
        name="out_proj_router",
    )(ya, yb, yc, x_all, mod_l, w_out_p, g.reshape(1, d), w_router, b_router.reshape(1, LANES))


def _invert_kernel(cnt_ref, pstart_ref, padded_ref, nu_ref, dest_ref, rows_ref):
    i = pl.program_id(0)
    n_slots, n_pairs = rows_ref.shape[0], dest_ref.shape[2]

    @pl.when(i == 0)
    def _():
        def zero(r, carry):
            rows_ref[r] = 0
            return carry

        for e in range(N_EXPERTS):
            lax.fori_loop(pstart_ref[e] + cnt_ref[e], pstart_ref[e] + padded_ref[e], zero, 0)
        lax.fori_loop(nu_ref[0] * MOE_BLOCK, n_slots, zero, 0)

    def place(j, carry):
        f = i * n_pairs + j
        rows_ref[dest_ref[0, 0, j]] = lax.shift_right_logical(f, TOP_K.bit_length() - 1)
        return carry

    lax.fori_loop(0, n_pairs, place, 0, unroll=8)


def _moe_invert(dest, counts, padded_start, padded, n_used, n_slots):
    assert TOP_K & (TOP_K - 1) == 0
    n_tiles = dest.shape[0] // TILE
    grid_spec = pltpu.PrefetchScalarGridSpec(
        num_scalar_prefetch=4,
        grid=(n_tiles,),
        in_specs=[pl.BlockSpec((1, 1, TILE * TOP_K), lambda i, *_: (i, 0, 0), memory_space=pltpu.SMEM)],
        out_specs=pl.BlockSpec(memory_space=pltpu.SMEM),
    )
    return pl.pallas_call(
        _invert_kernel,
        grid_spec=grid_spec,
        out_shape=jax.ShapeDtypeStruct((n_slots,), I32),
        compiler_params=_params("arbitrary"),
        name="moe_slot_tokens",
    )(counts, padded_start, padded, n_used, dest.reshape(n_tiles, 1, TILE * TOP_K))


def _moe_kernel(be_ref, nu_ref, slot_ref, nxt_ref, tok_ref, tok_next_ref, h_hbm, wgu_hbm, bgu_ref, wd_hbm, bd_ref,
                y_ref, xbuf, gsem, wgu_f32, wd_f32, wgu_bf, wd_bf, sem, *, layer):
    i = pl.program_id(0)
    n_used = nu_ref[0]
    par = i % 2
    rows = xbuf.shape[1]

    def weight_copies(e, s):
        return (pltpu.make_async_copy(wgu_hbm.at[layer, e], wgu_f32.at[s], sem.at[0, s]),
                pltpu.make_async_copy(wd_hbm.at[layer, e], wd_f32.at[s], sem.at[1, s]))

    def rows_wait(s):
        pltpu.make_async_copy(h_hbm.at[pl.ds(0, rows)], xbuf.at[s], gsem.at[s]).wait()

    @pl.when(i == 0)
    def _():
        def first(r, carry):
            pltpu.make_async_copy(h_hbm.at[tok_ref[0, 0, r]], xbuf.at[0, r], gsem.at[0]).start()
            return carry
        lax.fori_loop(0, rows, first, 0, unroll=16)

    @pl.when(i < n_used)
    def _():
        e, s = be_ref[i], slot_ref[i]

        @pl.when(i == 0)
        def _():
            for cp in weight_copies(e, s):
                cp.start()

        @pl.when(jnp.logical_or(i == 0, e != be_ref[jnp.maximum(i - 1, 0)]))
        def _():
            for cp in weight_copies(e, s):
                cp.wait()

            @pl.when(nxt_ref[i] >= 0)
            def _():
                for cp in weight_copies(nxt_ref[i], 1 - s):
                    cp.start()

            wgu_bf[...] = wgu_f32[s].astype(BF16)
            wd_bf[...] = wd_f32[s].astype(BF16)

        rows_wait(par)
        x = xbuf[par].reshape(rows, -1).astype(BF16)
        for r in range(rows):
            pltpu.make_async_copy(h_hbm.at[tok_next_ref[0, 0, r]], xbuf.at[1 - par, r], gsem.at[1 - par]).start()
        gu =jnp.dot(x, wgu_bf[...], preferred_element_type=F32) + bgu_ref[0, 0]
        glu = jnp.minimum(gu[:, :D_EXPERT], SWIGLU_LIMIT)
        lin = jnp.clip(gu[:, D_EXPERT:], -SWIGLU_LIMIT, SWIGLU_LIMIT)
        act = glu * _sigmoid(SWIGLU_ALPHA * glu) * (lin + 1.0)
        y = jnp.dot(act.astype(BF16), wd_bf[...], preferred_element_type=F32) + bd_ref[0, 0]
        y_ref[...] = y.reshape(y_ref.shape)

    @pl.when(i == n_used)
    def _():
        rows_wait(par)

    @pl.when(i >= n_used)
    def _():
        y_ref[...] = jnp.zeros_like(y_ref)


def _moe_experts(h2, rows_tok, block_e, n_used, w_slot, next_e, w_gu, b_gu, w_down, b_down, layer):
    n_slots, tile_shape = rows_tok.shape[0], h2.shape[1:]
    n_blocks = n_slots // MOE_BLOCK
    depth, d = w_gu.shape[0], w_gu.shape[2]
    tok3 = rows_tok.reshape(n_blocks, 1, MOE_BLOCK)

    def expert(i, be, nu, sl, nx):
        return (layer, be[jnp.minimum(i, nu[0] - 1)], 0, 0)

    grid_spec = pltpu.PrefetchScalarGridSpec(
        num_scalar_prefetch=4,
        grid=(n_blocks,),
        in_specs=[
            pl.BlockSpec((1, 1, MOE_BLOCK), lambda i, be, nu, sl, nx: (i, 0, 0), memory_space=pltpu.SMEM),
            pl.BlockSpec((1, 1, MOE_BLOCK), lambda i, be, nu, sl, nx: (jnp.minimum(i + 1, n_blocks - 1), 0, 0),
                         memory_space=pltpu.SMEM),
            pl.BlockSpec(memory_space=pl.ANY),
            pl.BlockSpec(memory_space=pl.ANY),
            pl.BlockSpec((1, 1, 1, 2 * D_EXPERT), expert),
            pl.BlockSpec(memory_space=pl.ANY),
            pl.BlockSpec((1, 1, 1, d), expert),
        ],
        out_specs=pl.BlockSpec((MOE_BLOCK,) + tile_shape, lambda i, be, nu, sl, nx: (i, 0, 0)),
        scratch_shapes=[pltpu.VMEM((2, MOE_BLOCK) + tile_shape, F32), pltpu.SemaphoreType.DMA((2,)),
                        pltpu.VMEM((2, d, 2 * D_EXPERT), F32), pltpu.VMEM((2, D_EXPERT, d), F32),
                        pltpu.VMEM((d, 2 * D_EXPERT), BF16), pltpu.VMEM((D_EXPERT, d), BF16),
                        pltpu.SemaphoreType.DMA((2, 2))],
    )
    return pl.pallas_call(
        functools.partial(_moe_kernel, layer=layer),
        grid_spec=grid_spec,
        out_shape=jax.ShapeDtypeStruct((n_slots,) + tile_shape, F32),
        compiler_params=_params("arbitrary"),
        name="moe_experts",
    )(block_e, n_used, w_slot, next_e, tok3, tok3, h2, w_gu, b_gu.reshape(depth, N_EXPERTS, 1, -1), w_down,
      b_down.reshape(depth, N_EXPERTS, 1, -1))


def _comb_kernel(inv_ref, inv_next_ref, y_hbm, tg_ref, x_ref, mod_ref, fg_ref, o_ref, cbuf, sem,
                 *, n_lat, lat_per_batch, ctx_row, final):
    i = pl.program_id(0)
    n_steps = pl.num_programs(0)
    slot = i % 2
    rows = cbuf.shape[1]
    per_tile = TILE // COMB_TILE

    def issue(idx_ref, dst_slot):
        def body(r, carry):
            pltpu.make_async_copy(y_hbm.at[idx_ref[0, 0, r]], cbuf.at[dst_slot, r], sem.at[dst_slot]).start()
            return carry
        lax.fori_loop(0, rows, body, 0, unroll=16)

    @pl.when(i == 0)
    def _():
        issue(inv_ref, 0)

    @pl.when(i + 1 < n_steps)
    def _():
        issue(inv_next_ref, 1 - slot)

    pltpu.make_async_copy(y_hbm.at[pl.ds(0, rows)], cbuf.at[slot], sem.at[slot]).wait()

    r = _mod_row(i // per_tile, n_lat, lat_per_batch, ctx_row)
    g2 = mod_ref[5, pl.ds(r, 1), :]
    tg = tg_ref[...]
    f = None
    for kk in range(TOP_K):
        rows_k = cbuf[slot, kk * COMB_TILE:(kk + 1) * COMB_TILE].reshape(COMB_TILE, -1)
        f = tg[:, kk:kk + 1] * rows_k if f is None else f + tg[:, kk:kk + 1] * rows_k
    x2 = x_ref[...] + g2 * f
    o_ref[...] = _rms(x2, fg_ref[...]) if final else x2


def _moe_combine(y_sorted, dest, tg, x_all, mod_l, final_g, geo, final):
    t_moe = dest.shape[0]
    d = x_all.shape[1]
    n_steps = t_moe // COMB_TILE
    inv3 = dest.reshape(n_steps, COMB_TILE, TOP_K).transpose(0, 2, 1).reshape(n_steps, 1, TOP_K * COMB_TILE)
    kern = functools.partial(_comb_kernel, n_lat=geo["n_lat"], lat_per_batch=geo["lpb"], ctx_row=geo["ctx_row"],
                             final=final)
    return pl.pallas_call(
        kern,
        grid=(n_steps,),
        in_specs=[
            pl.BlockSpec((1, 1, TOP_K * COMB_TILE), lambda i: (i, 0, 0), memory_space=pltpu.SMEM),
            pl.BlockSpec((1, 1, TOP_K * COMB_TILE), lambda i: (jnp.minimum(i + 1, n_steps - 1), 0, 0),
                         memory_space=pltpu.SMEM),
            pl.BlockSpec(memory_space=pl.ANY),
            pl.BlockSpec((COMB_TILE, LANES), lambda i: (i, 0)),
            pl.BlockSpec((COMB_TILE, d), lambda i: (i, 0)),
            pl.BlockSpec(mod_l.shape, lambda i: (0, 0, 0)),
            pl.BlockSpec((1, d), lambda i: (0, 0)),
        ],
        out_specs=pl.BlockSpec((COMB_TILE, d), lambda i: (i, 0)),
        out_shape=jax.ShapeDtypeStruct((t_moe, d), F32),
        scratch_shapes=[pltpu.VMEM((2, TOP_K * COMB_TILE) + y_sorted.shape[1:], F32), pltpu.SemaphoreType.DMA((2,))],
        compiler_params=_params("arbitrary"),
        name="moe_combine",
    )(inv3, inv3, y_sorted, tg, x_all, mod_l, final_g.reshape(1, d))


def _route(te, rk, cnt, t_moe):
    counts = cnt[0, :N_EXPERTS].astype(I32)
    padded = (counts + MOE_BLOCK - 1) // MOE_BLOCK * MOE_BLOCK
    padded_end = jnp.cumsum(padded)
    padded_start = padded_end - padded
    experts = jnp.arange(N_EXPERTS, dtype=I32)
    e_tok = te[:t_moe, :TOP_K]
    dest = rk[:t_moe, :TOP_K] + jnp.sum(jnp.where(e_tok[..., None] == experts, padded_start, 0), axis=-1)
    n_blocks = -(-(t_moe * TOP_K) // MOE_BLOCK) + N_EXPERTS
    blk_start = jnp.arange(n_blocks, dtype=I32) * MOE_BLOCK
    block_e = jnp.minimum(jnp.sum((padded_end[None, :] <= blk_start[:, None]).astype(I32), axis=1), N_EXPERTS - 1)
    n_used = (padded_end[-1:] // MOE_BLOCK).astype(I32)
    nonempty = counts > 0
    ordinal = jnp.cumsum(nonempty.astype(I32)) - 1
    later = jnp.where(jnp.logical_and(nonempty[None, :], experts[None, :] > experts[:, None]), experts[None, :], N_EXPERTS)
    nxt = jnp.min(later, axis=1)
    nxt = jnp.where(nxt == N_EXPERTS, -1, nxt)
    w_slot = (jnp.take(ordinal, block_e) % 2).astype(I32)
    next_e = jnp.take(nxt, block_e).astype(I32)
    segments = (counts, padded_start.astype(I32), padded.astype(I32))
    return dest.astype(I32), block_e, n_used, segments, w_slot, next_e, n_blocks * MOE_BLOCK


def _block_diag_dense(w):
    g, i, j = w.shape
    out = jnp.zeros((g * i, g * j), w.dtype)
    for n in range(g):
        out = out.at[n * i:(n + 1) * i, n * j:(n + 1) * j].set(w[n])
    return out


def _pad_heads(w, heads, axis):
    shape = list(w.shape)
    shape[axis:axis + 1] = [heads, shape[axis] // heads]
    w = w.reshape(shape)
    pad = [(0, 0)] * w.ndim
    pad[axis + 1] = (0, LANES - shape[axis + 1])
    w = jnp.pad(w, pad)
    shape[axis:axis + 2] = [heads * LANES]
    return w.reshape(shape)


_ROPE_SWAP = np.concatenate([np.arange(8, 16), np.arange(0, 8), np.arange(24, 32), np.arange(16, 24)])


def _prep_in_weight(w_in):
    d = w_in.shape[0]
    o = np.cumsum([0, LRU_WIDTH, LRU_WIDTH, MLA_Q_RANK, MLA_KV_RANK, MLA_ROPE, ML_WIDTH, ML_WIDTH])
    a_xg = w_in[:, o[0]:o[2]]
    b_qkv = w_in[:, o[2]:o[4]]
    kr = w_in[:, o[4]:o[5]]
    z_nope = jnp.zeros((d, MLA_NOPE), w_in.dtype)
    z_tail = jnp.zeros((d, LANES - MLA_NOPE - MLA_ROPE), w_in.dtype)
    m_x = _pad_heads(w_in[:, o[5]:o[6]], ML_HEADS, 1)
    m_z = _pad_heads(w_in[:, o[6]:o[7]], ML_HEADS, 1)
    return jnp.concatenate([a_xg, b_qkv, z_nope, kr, z_tail, z_nope, kr[:, _ROPE_SWAP], z_tail, m_x, m_z],
                           axis=1).astype(BF16)


def _prep_out_weight(w_out):
    a = w_out[:LRU_WIDTH]
    b = w_out[LRU_WIDTH:LRU_WIDTH + MLA_HEADS * MLA_V]
    c = _pad_heads(w_out[LRU_WIDTH + MLA_HEADS * MLA_V:], ML_HEADS, 0)
    return jnp.concatenate([a, b, c], axis=0).astype(BF16)


def _prep_mla(q_norm_g, w_qb, kv_norm_g, w_kvb):
    scale = (MLA_NOPE + MLA_ROPE) ** -0.5 * math.log2(math.e)
    rq = w_qb.shape[0]
    wq = w_qb.reshape(rq, MLA_HEADS, MLA_NOPE + MLA_ROPE) * scale
    nope, rope = wq[..., :MLA_NOPE], wq[..., MLA_NOPE:]
    z32 = jnp.zeros((rq, MLA_HEADS, LANES - MLA_NOPE - MLA_ROPE), w_qb.dtype)
    w1 = jnp.concatenate([nope, rope, z32], axis=-1).reshape(rq, MLA_HEADS * LANES)
    w2 = jnp.concatenate([jnp.zeros_like(nope), rope[..., _ROPE_SWAP], z32], axis=-1).reshape(rq, MLA_HEADS * LANES)
    rk = w_kvb.shape[0]
    wkv = w_kvb.reshape(rk, MLA_HEADS, MLA_NOPE + MLA_V)
    z64 = jnp.zeros((rk, MLA_HEADS, LANES - MLA_NOPE), w_kvb.dtype)
    wk = jnp.concatenate([wkv[..., :MLA_NOPE], z64], axis=-1).reshape(rk, MLA_HEADS * LANES)
    wv = jnp.concatenate([wkv[..., MLA_NOPE:], z64], axis=-1).reshape(rk, MLA_HEADS * LANES)
    return {"gq": q_norm_g.reshape(1, -1), "w1": w1.astype(BF16), "w2": w2.astype(BF16),
            "gkv": kv_norm_g.reshape(1, -1), "wk": wk.astype(BF16), "wv": wv.astype(BF16)}


def _prep_mlstm(conv_w, conv_b, wq, wk, wv, w_gate_d, b_gate_d, norm_g, skip):
    def proj(w):
        return _pad_heads(_pad_heads(_block_diag_dense(w), ML_HEADS, 0), ML_HEADS, 1)

    wk_p = proj(wk) * (ML_HEAD_DIM ** -0.5)
    wg = jnp.concatenate([_pad_heads(w_gate_d[i * ML_WIDTH:(i + 1) * ML_WIDTH], ML_HEADS, 0) for i in range(3)], axis=0)
    ng = w_gate_d.shape[1]
    wg = jnp.pad(wg, ((0, 0), (0, LANES - ng)))
    b_gate_d = jnp.pad(b_gate_d, (0, LANES - ng))
    gate_rows = 2 * SUBLANES
    return {
        "cw": _pad_heads(conv_w, ML_HEADS, 1), "cb": _pad_heads(conv_b.reshape(1, -1), ML_HEADS, 1),
        "wq": proj(wq).astype(BF16), "wk": wk_p.astype(BF16), "wkt": wk_p.T.astype(BF16), "wv": proj(wv).astype(BF16),
        "wg": wg.astype(BF16), "wgt": wg.T[:gate_rows].astype(BF16),
        "bg": b_gate_d.reshape(1, -1), "bgt": b_gate_d[:gate_rows].reshape(-1, 1),
        "ng": _pad_heads(norm_g.reshape(1, -1), ML_HEADS, 1), "sk": _pad_heads(skip.reshape(1, -1), ML_HEADS, 1),
    }


def _rope_tables(seq, ctx_len):
    rows = seq // GRID_W
    row = jnp.repeat(jnp.arange(rows, dtype=I32), GRID_W)
    col = jnp.tile(jnp.arange(GRID_W, dtype=I32), rows)
    freqs = ROPE_BASE ** (-jnp.arange(ROPE_AXIS_FREQ, dtype=F32) / ROPE_AXIS_FREQ)
    ang_r, ang_c = row[:, None] * freqs, col[:, None] * freqs
    cos32 = jnp.concatenate([jnp.cos(ang_r), jnp.cos(ang_r), jnp.cos(ang_c), jnp.cos(ang_c)], axis=1)
    sin32 = jnp.concatenate([-jnp.sin(ang_r), jnp.sin(ang_r), -jnp.sin(ang_c), jnp.sin(ang_c)], axis=1)
    cos32 = jnp.concatenate([cos32, jnp.ones((ctx_len, MLA_ROPE), F32)], axis=0)
    sin32 = jnp.concatenate([sin32, jnp.zeros((ctx_len, MLA_ROPE), F32)], axis=0)
    n = seq + ctx_len
    tail = jnp.zeros((n, LANES - MLA_NOPE - MLA_ROPE), F32)
    cos_t = jnp.concatenate([jnp.ones((n, MLA_NOPE), F32), cos32, tail], axis=1)
    sin_t = jnp.concatenate([jnp.zeros((n, MLA_NOPE), F32), sin32, tail], axis=1)
    return cos_t, sin_t


def kernel(x, c, ctx, c_ctx, norm1_g, norm2_g, w_mod, b_mod, w_in, w_out, lru_conv_w, lru_conv_b, lru_wa, lru_ba,
           lru_wx, lru_bx, lru_lambda, mla_q_norm_g, mla_w_qb, mla_kv_norm_g, mla_w_kvb, ml_conv_w, ml_conv_b,
           ml_wq, ml_wk, ml_wv, ml_w_gate, ml_b_gate, ml_norm_g, ml_skip, w_router, b_router, w_gu, b_gu, w_down,
           b_down, final_g):
    bsz, seq, d = x.shape
    ctx_len = ctx.shape[1]
    depth = w_mod.shape[0]
    assert ctx_len == TILE and seq % ATTN_Q_TILE == 0 and bsz + 1 <= SUBLANES
    t_lat = bsz * seq
    t_all = t_lat + bsz * ctx_len
    geo = {"batch": bsz, "lpb": seq // TILE, "n_lat": t_lat // TILE, "ctx_row": bsz, "t_all": t_all}

    cv = jnp.zeros((SUBLANES, d), F32).at[:bsz].set(c).at[bsz].set(c_ctx)
    mod = _modulation(cv, w_mod, b_mod)
    cos_t, sin_t = _rope_tables(seq, ctx_len)
    x_all = jnp.concatenate([x.reshape(t_lat, d), ctx.reshape(bsz * ctx_len, d)], axis=0)

    out = None
    for l in range(depth):
        last = l == depth - 1
        u_a, u_b, u_m = _in_proj(x_all, mod[l], norm1_g[l], _prep_in_weight(w_in[l]), geo)

        hf = None
        for dd in range(2):
            wg = jnp.concatenate([_block_diag_dense(lru_wa[l, dd]), _block_diag_dense(lru_wx[l, dd])], axis=1)
            bg = jnp.concatenate([lru_ba[l, dd], lru_bx[l, dd]])
            hf = _lru_dir(u_a, hf, lru_conv_w[l], lru_conv_b[l], wg.astype(BF16), bg, lru_lambda[l, dd], geo, dd == 1)
        ya = hf

        q, k, v = _mla_proj(u_b, cos_t, sin_t, _prep_mla(mla_q_norm_g[l], mla_w_qb[l], mla_kv_norm_g[l], mla_w_kvb[l]), geo)
        tq = ATTN_Q_TILE
        yb = _flash(q, k, v, tq, 0, seq // tq, 0, seq + ctx_len, 0, t_lat)
        if not last:
            yb_c = _flash(q, k, v, TILE, seq // TILE, 1, seq, ctx_len, 0, bsz * ctx_len)
            yb = jnp.concatenate([yb, yb_c], axis=0)

        hf = None
        for dd in range(2):
            wts = _prep_mlstm(ml_conv_w[l], ml_conv_b[l], ml_wq[l], ml_wk[l], ml_wv[l], ml_w_gate[l, dd],
                              ml_b_gate[l, dd], ml_norm_g[l], ml_skip[l])
            hf = _mlstm_dir(u_m, hf, wts, geo, dd == 1)
        yc = hf

        n_tiles = (t_lat if last else t_all) // TILE
        x_mid, h2, te, tg, rk, cnt = _out_proj(ya, yb, yc, x_all, mod[l], _prep_out_weight(w_out[l]), norm2_g[l],
                                               w_router[l], b_router[l], geo, n_tiles)
        t_moe = n_tiles * TILE
        dest, block_e, n_used, segments, w_slot, next_e, n_slots = _route(te, rk, cnt, t_moe)
        rows_tok = _moe_invert(dest, *segments, n_used, n_slots)
        y_sorted = _moe_experts(h2, rows_tok, block_e, n_used, w_slot, next_e, w_gu, b_gu, w_down, b_down, l)
        x_all = _moe_combine(y_sorted, dest, tg, x_mid, mod[l], final_g, geo, last)
        if last:
            out = x_all.reshape(bsz, seq, d)
    return out
```

```python
import functools
import math

import jax
import jax.numpy as jnp
import numpy as np
from jax import lax
from jax.experimental import pallas as pl
from jax.experimental.pallas import tpu as pltpu

F32 = jnp.float32
BF16 = jnp.bfloat16
I32 = jnp.int32
HIGHEST = lax.Precision.HIGHEST

LANES = 128
SUBLANES = 8
VMEM_LIMIT_BYTES = 56 * 1024 * 1024

GRID_W = 64
EPS = 1e-6
LRU_WIDTH = 256
LRU_C = 8.0
CONV_W = 4
MLA_HEADS = 8
MLA_NOPE = 64
MLA_ROPE = 32
MLA_V = 64
MLA_Q_RANK = 256
MLA_KV_RANK = 128
ROPE_AXIS_FREQ = MLA_ROPE // 4
ROPE_BASE = 10000.0
ML_HEADS = 4
ML_HEAD_DIM = 64
ML_WIDTH = ML_HEADS * ML_HEAD_DIM
ML_PAD = ML_HEADS * LANES
N_EXPERTS = 32
TOP_K = 4
D_EXPERT = 1024
SWIGLU_LIMIT = 7.0
SWIGLU_ALPHA = 1.702
MOE_BLOCK = 256

TILE = 256
HALO = SUBLANES
UB_W = MLA_Q_RANK + MLA_KV_RANK + 2 * LANES
COMB_TILE = 128
ATTN_Q_TILE = 1024

NT_DIMS = (((1,), (1,)), ((), ()))


def _params(*sem):
    return pltpu.CompilerParams(dimension_semantics=sem, vmem_limit_bytes=VMEM_LIMIT_BYTES)


def _sigmoid(x):
    return 1.0 / (1.0 + jnp.exp(-x))


def _log_sigmoid(x):
    return jnp.minimum(x, 0.0) - jnp.log1p(jnp.exp(-jnp.abs(x)))


def _split3(x):
    hi = x.astype(BF16)
    rest = x - hi.astype(F32)
    mid = rest.astype(BF16)
    return hi, mid, (rest - mid.astype(F32)).astype(BF16)


def _rms(x, g):
    return x * lax.rsqrt(jnp.mean(x * x, axis=-1, keepdims=True) + EPS) * g


def _mod_kernel(cv_ref, w_ref, b_ref, o_ref):
    cv = cv_ref[...]
    a = cv * _sigmoid(cv)
    o_ref[0, 0] = jnp.dot(a, w_ref[0], precision=HIGHEST, preferred_element_type=F32) + b_ref[0, 0]


def _modulation(cv, w_mod, b_mod):
    depth, d, _ = w_mod.shape
    rows = cv.shape[0]
    return pl.pallas_call(
        _mod_kernel,
        grid=(depth, 6),
        in_specs=[
            pl.BlockSpec((rows, d), lambda l, j: (0, 0)),
            pl.BlockSpec((1, d, d), lambda l, j: (l, 0, j)),
            pl.BlockSpec((1, 1, 1, d), lambda l, j: (l, j, 0, 0)),
        ],
        out_specs=pl.BlockSpec((1, 1, rows, d), lambda l, j: (l, j, 0, 0)),
        out_shape=jax.ShapeDtypeStruct((depth, 6, rows, d), F32),
        compiler_params=_params("arbitrary", "arbitrary"),
        name="modulation",
    )(cv, w_mod, b_mod.reshape(depth, 6, 1, d))


def _mod_row(i, n_lat, lat_per_batch, ctx_row):
    return jnp.where(i < n_lat, i // lat_per_batch, ctx_row)


def _in_kernel(x_ref, mod_ref, g_ref, w_ref, ua_ref, ub_ref, um_ref, *, n_lat, lat_per_batch, ctx_row):
    r = _mod_row(pl.program_id(0), n_lat, lat_per_batch, ctx_row)
    sh = mod_ref[0, pl.ds(r, 1), :]
    sc = mod_ref[1, pl.ds(r, 1), :]
    h = _rms(x_ref[...], g_ref[...]) * (1.0 + sc) + sh
    u = jnp.dot(h.astype(BF16), w_ref[...], preferred_element_type=F32)
    wa = ua_ref.shape[1]
    wb = ub_ref.shape[1]
    ua_ref[...] = u[:, :wa]
    ub_ref[...] = u[:, wa:wa + wb]
    um_ref[...] = u[:, wa + wb:]


def _in_proj(x_all, mod_l, g, w_in_p, geo):
    t_all, d = x_all.shape
    n_tiles = t_all // TILE
    wa, wb, wm = 2 * LRU_WIDTH, UB_W, 2 * ML_PAD
    kern = functools.partial(_in_kernel, n_lat=geo["n_lat"], lat_per_batch=geo["lpb"], ctx_row=geo["ctx_row"])
    return pl.pallas_call(
        kern,
        grid=(n_tiles,),
        in_specs=[
            pl.BlockSpec((TILE, d), lambda i: (i, 0)),
            pl.BlockSpec(mod_l.shape, lambda i: (0, 0, 0)),
            pl.BlockSpec((1, d), lambda i: (0, 0)),
            pl.BlockSpec(w_in_p.shape, lambda i: (0, 0)),
        ],
        out_specs=[
            pl.BlockSpec((TILE, wa), lambda i: (i, 0)),
            pl.BlockSpec((TILE, wb), lambda i: (i, 0)),
            pl.BlockSpec((TILE, wm), lambda i: (i, 0)),
        ],
        out_shape=[
            jax.ShapeDtypeStruct((t_all, wa), F32),
            jax.ShapeDtypeStruct((t_all, wb), F32),
            jax.ShapeDtypeStruct((t_all, wm), F32),
        ],
        compiler_params=_params("arbitrary"),
        name="in_proj",
    )(x_all, mod_l, g.reshape(1, d), w_in_p)


def _chunk_block(b, j, geo, rev):
    lat = (geo["lpb"] - j) if rev else (j - 1)
    return jnp.where(j == 0, geo["n_lat"] + b, b * geo["lpb"] + lat)


def _local_block(j, lpb, rev):
    return jnp.where(j == 0, lpb, (lpb - j) if rev else (j - 1))


def _chunk_specs(width, col, geo, rev, b):
    per = TILE // HALO
    last = geo["t_all"] // HALO - 1

    def cur(j):
        return (_chunk_block(b, j, geo, rev), col)

    def prev(j):
        return (jnp.maximum(_chunk_block(b, j, geo, rev) * per - 1, 0), col)

    def nxt(j):
        return (jnp.minimum((_chunk_block(b, j, geo, rev) + 1) * per, last), col)

    return [pl.BlockSpec((TILE, width), cur), pl.BlockSpec((HALO, width), prev), pl.BlockSpec((HALO, width), nxt)]


def _scan_call(kern, name, src, width, hf, weights, scratch, geo, rev):
    bsz, lpb = geo["batch"], geo["lpb"]
    chunk_of_all = pl.BlockSpec((bsz, TILE, width), lambda j: (0, _local_block(j, lpb, rev), 0))
    specs, args = [], []
    for b in range(bsz):
        specs += _chunk_specs(width, 0, geo, rev, b)
        args += [src, src, src]
        if rev:
            specs.append(pl.BlockSpec((TILE, width), lambda j, b=b: (_chunk_block(b, j, geo, rev), 1)))
            args.append(src)
    if rev:
        specs.append(chunk_of_all)
        args.append(hf)
    for wgt in weights:
        specs.append(pl.BlockSpec(wgt.shape, lambda j: (0, 0)))
        args.append(wgt)
    return pl.pallas_call(
        functools.partial(kern, rev=rev, lpb=lpb, bsz=bsz),
        grid=(lpb + 1,),
        in_specs=specs,
        out_specs=chunk_of_all,
        out_shape=jax.ShapeDtypeStruct((bsz, (lpb + 1) * TILE, width), BF16 if rev else F32),
        scratch_shapes=scratch,
        compiler_params=_params("arbitrary"),
        name=name,
    )(*args)


def _split_scan_refs(refs, rev, bsz, n_weights, n_scratch):
    n_in = 4 if rev else 3
    batch_refs = [refs[b * n_in:(b + 1) * n_in] for b in range(bsz)]
    w0 = bsz * n_in
    hf_ref = refs[w0] if rev else None
    w0 += 1 if rev else 0
    weights = refs[w0:w0 + n_weights]
    return batch_refs, hf_ref, weights, refs[w0 + n_weights], refs[len(refs) - n_scratch:]


def _short_conv(x, xp_ref, xn_ref, w_ref, b_ref, j, lpb, rev):
    n = x.shape[0]
    lat = (lpb - j) if rev else (j - 1)
    is_lat = j > 0
    prev_ok = jnp.logical_and(is_lat, lat > 0)
    next_ok = jnp.logical_and(is_lat, lat < lpb - 1)
    xp = xp_ref[...] * prev_ok.astype(F32)
    xn = xn_ref[...] * next_ok.astype(F32)
    row = lax.broadcasted_iota(I32, x.shape, 0)
    x_m1 = jnp.where(row == 0, xp[HALO - 1:HALO, :], pltpu.roll(x, 1, 0))
    x_m2 = jnp.where(row == 0, xp[HALO - 2:HALO - 1, :], jnp.where(row == 1, xp[HALO - 1:HALO, :], pltpu.roll(x, 2, 0)))
    x_p1 = jnp.where(row == n - 1, xn[0:1, :], pltpu.roll(x, n - 1, 0))
    return x_m2 * w_ref[0:1, :] + x_m1 * w_ref[1:2, :] + x * w_ref[2:3, :] + x_p1 * w_ref[3:4, :] + b_ref[...]


def _lin_scan(a, b, rev):
    n = a.shape[0]
    row = lax.broadcasted_iota(I32, a.shape, 0)
    d = 1
    while d < n:
        if rev:
            a_s, b_s, valid = pltpu.roll(a, n - d, 0), pltpu.roll(b, n - d, 0), row < n - d
        else:
            a_s, b_s, valid = pltpu.roll(a, d, 0), pltpu.roll(b, d, 0), row >= d
        a_s = jnp.where(valid, a_s, 1.0)
        b_s = jnp.where(valid, b_s, 0.0)
        b = a * b_s + b
        a = a * a_s
        d *= 2
        yield
    return a, b


def _lru_kernel(*refs, rev, lpb, bsz):
    batch_refs, hf_ref, (cw_ref, cb_ref, wg_ref, bg_ref, lam_ref), o_ref, (h_scr,) = _split_scan_refs(
        refs, rev, bsz, 5, 1)
    j = pl.program_id(0)

    @pl.when(j == 0)
    def _():
        h_scr[...] = jnp.zeros_like(h_scr)

    def chunk(b):
        x_ref, xp_ref, xn_ref = batch_refs[b][:3]
        x = x_ref[...]
        n, w = x.shape
        xc = _short_conv(x, xp_ref, xn_ref, cw_ref, cb_ref, j, lpb, rev)
        yield
        gates = jnp.dot(xc.astype(BF16), wg_ref[...], preferred_element_type=F32) + bg_ref[...]
        yield
        r = _sigmoid(gates[:, :w])
        ig = _sigmoid(gates[:, w:])
        log_a = LRU_C * r * _log_sigmoid(lam_ref[...])
        a = jnp.exp(log_a)
        bb = jnp.sqrt(-jnp.tanh(log_a) * (1.0 + a * a)) * ig * xc
        yield
        a_cum, h_loc = yield from _lin_scan(a, bb, rev)
        h = a_cum * h_scr[b] + h_loc
        h_scr[b] = h[0:1, :] if rev else h[n - 1:n, :]
        if rev:
            g_ref = batch_refs[b][3]
            o_ref[b] = ((hf_ref[b] + h) * jax.nn.gelu(g_ref[...], approximate=True)).astype(o_ref.dtype)
        else:
            o_ref[b] = h

    _round_robin([chunk(b) for b in range(bsz)])


def _lru_dir(u_a, hf, cw, cb, wg, bg, lam, geo, rev):
    w = LRU_WIDTH
    weights = [cw, cb.reshape(1, w), wg, bg.reshape(1, 2 * w), lam.reshape(1, w)]
    scratch = [pltpu.VMEM((geo["batch"], 1, w), F32)]
    return _scan_call(_lru_kernel, "rglru_bwd" if rev else "rglru_fwd", u_a, w, hf, weights, scratch, geo, rev)


def _mlstm_kernel(*refs, rev, lpb, bsz):
    n_weights = 12 if rev else 10
    batch_refs, hf_ref, weights, o_ref, (c_scr, m_scr) = _split_scan_refs(refs, rev, bsz, n_weights, 2)
    j = pl.program_id(0)

    @pl.when(j == 0)
    def _():
        c_scr[...] = jnp.zeros_like(c_scr)
        m_scr[...] = jnp.zeros_like(m_scr)

    _round_robin([_mlstm_chunk(batch_refs[b], hf_ref, weights, o_ref, c_scr, m_scr, b, j, rev, lpb)
                  for b in range(bsz)])


def _round_robin(stage_generators):
    live = list(stage_generators)
    while live:
        live = [g for g in live if next(g, StopIteration) is not StopIteration]


def _mlstm_chunk(in_refs, hf_ref, weights, o_ref, c_scr, m_scr, b, j, rev, lpb):
    if rev:
        x_ref, xp_ref, xn_ref, z_ref = in_refs
        cw_ref, cb_ref, wq_ref, wk_ref, wkt_ref, wv_ref, wg_ref, wgt_ref, bg_ref, bgt_ref, ng_ref, sk_ref = weights
    else:
        x_ref, xp_ref, xn_ref = in_refs
        cw_ref, cb_ref, wq_ref, wk_ref, wkt_ref, wv_ref, wg_ref, wgt_ref, bg_ref, bgt_ref = weights
    x = x_ref[...]
    n = x.shape[0]
    xc = _short_conv(x, xp_ref, xn_ref, cw_ref, cb_ref, j, lpb, rev)
    xc = xc * _sigmoid(xc)
    xcb = xc.astype(BF16)
    yield
    q = jnp.dot(xcb, wq_ref[...], preferred_element_type=F32)
    k = jnp.dot(xcb, wk_ref[...], preferred_element_type=F32)
    yield
    kt =lax.dot_general(wkt_ref[...], xcb, NT_DIMS, preferred_element_type=F32)
    lane_w = lax.broadcasted_iota(I32, (1, ML_PAD), 1)
    ones_lane = (lane_w % LANES == ML_HEAD_DIM).astype(F32)
    v = jnp.dot(x.astype(BF16), wv_ref[...], preferred_element_type=F32) + ones_lane
    yield
    qkv =jnp.concatenate([q, k, v], axis=1).astype(BF16)
    g_col = jnp.dot(qkv, wg_ref[...], preferred_element_type=F32) + bg_ref[...]
    g_row = lax.dot_general(wgt_ref[...], qkv, NT_DIMS, preferred_element_type=F32) + bgt_ref[...]
    yield
    ti =lax.broadcasted_iota(I32, (n, n), 0)
    si = lax.broadcasted_iota(I32, (n, n), 1)
    mask = (si >= ti) if rev else (si <= ti)
    tri = mask.astype(BF16)
    tri_t = ((ti >= si) if rev else (ti <= si)).astype(BF16)
    b_col = sum(jnp.dot(tri, part, preferred_element_type=F32) for part in _split3(_log_sigmoid(g_col)))
    b_row = sum(jnp.dot(part, tri_t, preferred_element_type=F32) for part in _split3(_log_sigmoid(g_row)))
    yield
    lane =lax.broadcasted_iota(I32, (1, LANES), 1)
    num_mask = (lane < ML_HEAD_DIM).astype(F32)
    den_mask = (lane == ML_HEAD_DIM).astype(F32)
    last = 0 if rev else n - 1
    outs = []
    for h in range(ML_HEADS):
        hs = slice(h * LANES, (h + 1) * LANES)
        bc = b_col[:, ML_HEADS + h:ML_HEADS + h + 1]
        br = b_row[ML_HEADS + h:ML_HEADS + h + 1, :]
        ic = g_col[:, h:h + 1]
        ir = g_row[h:h + 1, :]
        m_prev = m_scr[b, h:h + 1, 0:1]
        dmat = jnp.where(mask, bc - br + ir, -jnp.inf)
        inter = bc + m_prev
        m_t = jnp.maximum(inter, jnp.max(dmat, axis=1, keepdims=True))
        yield
        s =jnp.dot(q[:, hs].astype(BF16), kt[hs, :].astype(BF16), preferred_element_type=F32)
        p = s * jnp.exp(dmat - m_t)
        yield
        w_inter =jnp.exp(inter - m_t)
        vh = v[:, hs]
        c_old = c_scr[b * ML_HEADS + h]
        numden = (jnp.dot(p.astype(BF16), vh.astype(BF16), preferred_element_type=F32)
                  + w_inter * jnp.dot(q[:, hs].astype(BF16), c_old.astype(BF16), preferred_element_type=F32))
        den = jnp.sum(numden * den_mask, axis=1, keepdims=True)
        hh = numden * num_mask / jnp.maximum(jnp.abs(den), jnp.exp(-m_t))
        yield
        b_last =bc[last:last + 1, :]
        ws_col = b_last - bc + ic
        m_new = jnp.maximum(b_last + m_prev, jnp.max(ws_col, axis=0, keepdims=True))
        decay = jnp.exp(b_last + m_prev - m_new)
        wv = (jnp.exp(ws_col - m_new) * vh).astype(BF16)
        c_scr[b * ML_HEADS + h] = decay * c_old + jnp.dot(kt[hs, :].astype(BF16), wv, preferred_element_type=F32)
        m_scr[b, h:h + 1, :] = jnp.broadcast_to(m_new, (1, LANES))
        yield
        if rev:
            hsum = hf_ref[b, :, hs] + hh
            mu = jnp.sum(hsum, axis=1, keepdims=True) * (1.0 / ML_HEAD_DIM)
            cen = (hsum - mu) * num_mask
            var = jnp.sum(cen * cen, axis=1, keepdims=True) * (1.0 / ML_HEAD_DIM)
            hh = cen * lax.rsqrt(var + EPS)
        outs.append(hh)
    hcat = jnp.concatenate(outs, axis=1)
    if rev:
        z = z_ref[...]
        o_ref[b] = ((hcat * ng_ref[...] + sk_ref[...] * xc) * (z * _sigmoid(z))).astype(o_ref.dtype)
    else:
        o_ref[b] = hcat


def _mlstm_dir(u_m, hf, wts, geo, rev):
    names = ["cw", "cb", "wq", "wk", "wkt", "wv", "wg", "wgt", "bg", "bgt"] + (["ng", "sk"] if rev else [])
    bsz = geo["batch"]
    scratch = [pltpu.VMEM((bsz * ML_HEADS, LANES, LANES), F32), pltpu.VMEM((bsz, SUBLANES, LANES), F32)]
    return _scan_call(_mlstm_kernel, "mlstm_bwd" if rev else "mlstm_fwd", u_m, ML_PAD, hf, [wts[nm] for nm in names],
                      scratch, geo, rev)


def _mla_proj_kernel(ub_ref, cos_ref, sin_ref, gq_ref, w1_ref, w2_ref, gkv_ref, wk_ref, wv_ref, q_ref, k_ref, v_ref):
    ub = ub_ref[...]
    cos = cos_ref[...]
    sin = sin_ref[...]
    qn = _rms(ub[:, :MLA_Q_RANK], gq_ref[...]).astype(BF16)
    qa = jnp.dot(qn, w1_ref[...], preferred_element_type=F32)
    qb = jnp.dot(qn, w2_ref[...], preferred_element_type=F32)
    kvn = _rms(ub[:, MLA_Q_RANK:MLA_Q_RANK + MLA_KV_RANK], gkv_ref[...]).astype(BF16)
    kn = jnp.dot(kvn, wk_ref[...], preferred_element_type=F32)
    lane = lax.broadcasted_iota(I32, (1, LANES), 1)
    ones_lane = (lane == MLA_V).astype(F32)
    vn = jnp.dot(kvn, wv_ref[...], preferred_element_type=F32)
    off = MLA_Q_RANK + MLA_KV_RANK
    kr = ub[:, off:off + LANES] * cos + ub[:, off + LANES:off + 2 * LANES] * sin
    for h in range(MLA_HEADS):
        hs = slice(h * LANES, (h + 1) * LANES)
        q_ref[0, h] = (qa[:, hs] * cos + qb[:, hs] * sin).astype(BF16)
        k_ref[0, h] = (kn[:, hs] + kr).astype(BF16)
        v_ref[0, h] = (vn[:, hs] + ones_lane).astype(BF16)


def _mla_proj(u_b, cos_t, sin_t, wts, geo):
    n_tiles = geo["t_all"] // TILE
    n_lat, lpb, bsz = geo["n_lat"], geo["lpb"], geo["batch"]
    hw = MLA_HEADS * LANES

    def batch_of(i):
        return jnp.where(i < n_lat, i // lpb, i - n_lat)

    def blk_of(i):
        return jnp.where(i < n_lat, i % lpb, lpb)

    const = lambda i: (0, 0)
    head_spec = pl.BlockSpec((1, MLA_HEADS, TILE, LANES), lambda i: (batch_of(i), 0, blk_of(i), 0))
    head_shape = jax.ShapeDtypeStruct((bsz, MLA_HEADS, (lpb + 1) * TILE, LANES), BF16)
    return pl.pallas_call(
        _mla_proj_kernel,
        grid=(n_tiles,),
        in_specs=[
            pl.BlockSpec((TILE, UB_W), lambda i: (i, 0)),
            pl.BlockSpec((TILE, LANES), lambda i: (blk_of(i), 0)),
            pl.BlockSpec((TILE, LANES), lambda i: (blk_of(i), 0)),
            pl.BlockSpec((1, MLA_Q_RANK), const),
            pl.BlockSpec((MLA_Q_RANK, hw), const),
            pl.BlockSpec((MLA_Q_RANK, hw), const),
            pl.BlockSpec((1, MLA_KV_RANK), const),
            pl.BlockSpec((MLA_KV_RANK, hw), const),
            pl.BlockSpec((MLA_KV_RANK, hw), const),
        ],
        out_specs=[head_spec, head_spec, head_spec],
        out_shape=[head_shape, head_shape, head_shape],
        compiler_params=_params("arbitrary"),
        name="mla_proj",
    )(u_b, cos_t, sin_t, wts["gq"], wts["w1"], wts["w2"], wts["gkv"], wts["wk"], wts["wv"])


def _flash_kernel(q_ref, k_ref, v_ref, o_ref, m_scr, acc_scr, *, k_start, nk, tk):
    lane = lax.broadcasted_iota(I32, (1, LANES), 1)
    den_mask = (lane == MLA_V).astype(F32)
    outs = []
    for j in range(2):
        q = q_ref[0, j]
        m_scr[...] = jnp.full_like(m_scr, -jnp.inf)
        acc_scr[...] = jnp.zeros_like(acc_scr)

        def body(i, carry, j=j, q=q):
            start = k_start + i * tk
            kk = k_ref[0, j, start:start + tk, :]
            vv = v_ref[0, j, start:start + tk, :]
            s = lax.dot_general(q, kk, NT_DIMS, preferred_element_type=F32)
            cols = [s[:, c * LANES:(c + 1) * LANES] for c in range(tk // LANES)]
            mp = cols[0]
            for sc in cols[1:]:
                mp = jnp.maximum(mp, sc)
            m_old = m_scr[...]
            m_new = jnp.maximum(m_old, jnp.broadcast_to(jnp.max(mp, axis=1, keepdims=True), mp.shape))
            p = jnp.concatenate([jnp.exp2(sc - m_new).astype(BF16) for sc in cols], axis=1)
            acc_scr[...] = jnp.exp2(m_old - m_new) * acc_scr[...] + jnp.dot(p, vv, preferred_element_type=F32)
            m_scr[...] = m_new
            return carry

        for i in range(nk):
            body(i, 0)
        acc = acc_scr[...]
        den = jnp.sum(acc * den_mask, axis=1, keepdims=True)
        outs.append(acc / den)
    o_ref[...] = jnp.where(lane < MLA_V, outs[0], pltpu.roll(outs[1], MLA_V, 1)).astype(o_ref.dtype)


def _key_tile(n):
    for cand in range(min(n, 1024) // LANES * LANES, 0, -LANES):
        if n % cand == 0:
            return cand
    raise ValueError(n)


def _flash(q, k, v, tq, q_blk0, nq, k_start, k_len, out_blk0, t_all):
    bsz, heads, rows, _ = q.shape
    tk = _key_tile(k_len)
    kern = functools.partial(_flash_kernel, k_start=k_start, nk=k_len // tk, tk=tk)
    return pl.pallas_call(
        kern,
        grid=(bsz, heads // 2, nq),
        in_specs=[
            pl.BlockSpec((1, 2, tq, LANES), lambda b, h, i: (b, h, q_blk0 + i, 0)),
            pl.BlockSpec((1, 2, rows, LANES), lambda b, h, i: (b, h, 0, 0)),
            pl.BlockSpec((1, 2, rows, LANES), lambda b, h, i: (b, h, 0, 0)),
        ],
        out_specs=pl.BlockSpec((tq, LANES), lambda b, h, i: (out_blk0 + b * nq + i, h)),
        out_shape=jax.ShapeDtypeStruct((t_all, heads * MLA_V), BF16),
        scratch_shapes=[pltpu.VMEM((tq, LANES), F32), pltpu.VMEM((tq, LANES), F32)],
        compiler_params=_params("arbitrary", "arbitrary", "arbitrary"),
        name="mla_attention",
    )(q, k, v)


def _out_kernel(ya_ref, yb_ref, yc_ref, x_ref, mod_ref, w_ref, g_ref, wr_ref, br_ref,
                xo_ref, h2_ref, te_ref, tg_ref, rk_ref, cnt_ref, *, n_lat, lat_per_batch, ctx_row):
    i = pl.program_id(0)

    @pl.when(i == 0)
    def _():
        cnt_ref[...] = jnp.zeros_like(cnt_ref)

    r = _mod_row(i, n_lat, lat_per_batch, ctx_row)
    g1 = mod_ref[2, pl.ds(r, 1), :]
    sh2 = mod_ref[3, pl.ds(r, 1), :]
    sc2 = mod_ref[4, pl.ds(r, 1), :]
    y = jnp.concatenate([ya_ref[0], yb_ref[...], yc_ref[0]], axis=1)
    x1 = x_ref[...] + g1 * jnp.dot(y, w_ref[...], preferred_element_type=F32)
    xo_ref[...] = x1
    h2 = _rms(x1, g_ref[...]) * (1.0 + sc2) + sh2
    h2_ref[...] = h2.reshape(h2_ref.shape)
    h_hi, h_mid, _ = _split3(h2)
    logits = (jnp.dot(h_hi, wr_ref[0], preferred_element_type=F32)
              + (jnp.dot(h_mid, wr_ref[0], preferred_element_type=F32)
                 + jnp.dot(h_hi, wr_ref[1], preferred_element_type=F32))) + br_ref[...]
    n, ne = logits.shape
    lane_e = lax.broadcasted_iota(I32, (n, ne), 1).astype(F32)
    lane_o = lax.broadcasted_iota(I32, (n, LANES), 1)
    vals, idxs = [], []
    for _ in range(TOP_K):
        m = jnp.max(logits, axis=1, keepdims=True)
        idx = jnp.min(jnp.where(logits == m, lane_e, float(ne)), axis=1, keepdims=True)
        logits = jnp.where(lane_e == idx, -jnp.inf, logits)
        vals.append(m)
        idxs.append(idx)
    exps = [jnp.exp(vv - vals[0]) for vv in vals]
    tot = exps[0] + exps[1] + exps[2] + exps[3]
    hits = [(lane_e == idx).astype(F32) for idx in idxs]
    chosen = hits[0] + hits[1] + hits[2] + hits[3]
    ti = lax.broadcasted_iota(I32, (n, n), 0)
    si = lax.broadcasted_iota(I32, (n, n), 1)
    before = jnp.dot((si < ti).astype(BF16), chosen.astype(BF16), preferred_element_type=F32) + cnt_ref[0:1, :]
    cnt_ref[...] = cnt_ref[...] + jnp.sum(chosen, axis=0, keepdims=True)
    te = jnp.zeros((n, LANES), F32)
    tg = jnp.zeros((n, LANES), F32)
    rk = jnp.zeros((n, LANES), F32)
    for kk in range(TOP_K):
        te = jnp.where(lane_o == kk, idxs[kk], te)
        tg = jnp.where(lane_o == kk, exps[kk] / tot, tg)
        rk = jnp.where(lane_o == kk, jnp.sum(hits[kk] * before, axis=1, keepdims=True), rk)
    te_ref[...] = te.astype(I32)
    tg_ref[...] = tg
    rk_ref[...] = rk.astype(I32)


def _out_proj(ya, yb, yc, x_all, mod_l, w_out_p, g, w_router, b_router, geo, n_tiles):
    d = x_all.shape[1]
    rows = n_tiles * TILE
    kern = functools.partial(_out_kernel, n_lat=geo["n_lat"], lat_per_batch=geo["lpb"], ctx_row=geo["ctx_row"])
    row_blk = lambda i: (i, 0)
    const = lambda i: (0, 0)
    n_lat, lpb = geo["n_lat"], geo["lpb"]
    scan_blk = lambda i: (jnp.where(i < n_lat, i // lpb, i - n_lat), jnp.where(i < n_lat, i % lpb, lpb), 0)
    w_router = jnp.pad(w_router, ((0, 0), (0, LANES - N_EXPERTS)))
    w_router = jnp.stack(_split3(w_router)[:2])
    b_router = jnp.pad(b_router, (0, LANES - N_EXPERTS), constant_values=-jnp.inf)
    return pl.pallas_call(
        kern,
        grid=(n_tiles,),
        in_specs=[
            pl.BlockSpec((1, TILE, ya.shape[2]), scan_blk),
            pl.BlockSpec((TILE, yb.shape[1]), row_blk),
            pl.BlockSpec((1, TILE, yc.shape[2]), scan_blk),
            pl.BlockSpec((TILE, d), row_blk),
            pl.BlockSpec(mod_l.shape, lambda i: (0, 0, 0)),
            pl.BlockSpec(w_out_p.shape, const),
            pl.BlockSpec((1, d), const),
            pl.BlockSpec(w_router.shape, lambda i: (0, 0, 0)),
            pl.BlockSpec((1, LANES), const),
        ],
        out_specs=[pl.BlockSpec((TILE, d), row_blk), pl.BlockSpec((TILE, SUBLANES, d // SUBLANES), lambda i: (i, 0, 0)),
                   pl.BlockSpec((TILE, LANES), row_blk), pl.BlockSpec((TILE, LANES), row_blk),
                   pl.BlockSpec((TILE, LANES), row_blk), pl.BlockSpec((SUBLANES, LANES), const)],
        out_shape=[jax.ShapeDtypeStruct((rows, d), F32), jax.ShapeDtypeStruct((rows, SUBLANES, d // SUBLANES), F32),
                   jax.ShapeDtypeStruct((rows, LANES), I32), jax.ShapeDtypeStruct((rows, LANES), F32),
                   jax.ShapeDtypeStruct((rows, LANES), I32), jax.ShapeDtypeStruct((SUBLANES, LANES), F32)],
        compiler_params=_params("arbitrary"),
        name="out_proj_router",
    )(ya, yb, yc, x_all, mod_l, w_out_p, g.reshape(1, d), w_router, b_router.reshape(1, LANES))


def _dispatch_kernel(zrow_ref, nu_ref, dest_ref, h_ref, xs_hbm, zbuf, zsem, sem):
    i = pl.program_id(0)
    n_tok = h_ref.shape[0]
    n_blocks = xs_hbm.shape[0] // MOE_BLOCK

    def zero_copy(blk_row):
        row = pl.multiple_of(blk_row, MOE_BLOCK)
        return pltpu.make_async_copy(zbuf, xs_hbm.at[pl.ds(row, MOE_BLOCK)], zsem)

    @pl.when(i == 0)
    def _():
        zbuf[...] = jnp.zeros_like(zbuf)
        for start in (True, False):
            for e in range(N_EXPERTS):
                @pl.when(zrow_ref[e] >= 0)
                def _(e=e, start=start):
                    zero_copy(zrow_ref[e]).start() if start else zero_copy(zrow_ref[e]).wait()

            def tail(b, carry, start=start):
                zero_copy(b * MOE_BLOCK).start() if start else zero_copy(b * MOE_BLOCK).wait()
                return carry
            lax.fori_loop(nu_ref[0], n_blocks, tail, 0)

    for kk in range(TOP_K):
        def issue(t, carry, kk=kk):
            pltpu.make_async_copy(h_ref.at[t], xs_hbm.at[dest_ref[0, 0, kk * n_tok + t]], sem).start()
            return carry
        lax.fori_loop(0, n_tok, issue, 0, unroll=16)
    for _ in range(TOP_K):
        pltpu.make_async_copy(h_ref, xs_hbm.at[pl.ds(0, n_tok)], sem).wait()


def _moe_dispatch(h2, dest, zrow, n_used, n_slots):
    t_moe, tile_shape = dest.shape[0], h2.shape[1:]
    n_tiles = t_moe // TILE
    dest3 = dest.reshape(n_tiles, TILE, TOP_K).transpose(0, 2, 1).reshape(n_tiles, 1, TOP_K * TILE)
    grid_spec = pltpu.PrefetchScalarGridSpec(
        num_scalar_prefetch=2,
        grid=(n_tiles,),
        in_specs=[
            pl.BlockSpec((1, 1, TOP_K * TILE), lambda i, zr, nu: (i, 0, 0), memory_space=pltpu.SMEM),
            pl.BlockSpec((TILE,) + tile_shape, lambda i, zr, nu: (i, 0, 0)),
        ],
        out_specs=pl.BlockSpec(memory_space=pl.ANY),
        scratch_shapes=[pltpu.VMEM((MOE_BLOCK,) + tile_shape, F32), pltpu.SemaphoreType.DMA, pltpu.SemaphoreType.DMA],
    )
    return pl.pallas_call(
        _dispatch_kernel,
        grid_spec=grid_spec,
        out_shape=jax.ShapeDtypeStruct((n_slots,) + tile_shape, F32),
        compiler_params=_params("arbitrary"),
        name="moe_dispatch",
    )(zrow, n_used, dest3, h2)


def _moe_kernel(be_ref, nu_ref, slot_ref, nxt_ref, x_ref, wgu_hbm, bgu_ref, wd_hbm, bd_ref, y_ref,
                wgu_f32, wd_f32, wgu_bf, wd_bf, sem, *, layer):
    i = pl.program_id(0)

    def weight_copies(e, s):
        return (pltpu.make_async_copy(wgu_hbm.at[layer, e], wgu_f32.at[s], sem.at[0, s]),
                pltpu.make_async_copy(wd_hbm.at[layer, e], wd_f32.at[s], sem.at[1, s]))

    @pl.when(i < nu_ref[0])
    def _():
        e, s = be_ref[i], slot_ref[i]

        @pl.when(i == 0)
        def _():
            for cp in weight_copies(e, s):
                cp.start()

        @pl.when(jnp.logical_or(i == 0, e != be_ref[jnp.maximum(i - 1, 0)]))
        def _():
            for cp in weight_copies(e, s):
                cp.wait()

            @pl.when(nxt_ref[i] >= 0)
            def _():
                for cp in weight_copies(nxt_ref[i], 1 - s):
                    cp.start()

            wgu_bf[...] = wgu_f32[s].astype(BF16)
            wd_bf[...] = wd_f32[s].astype(BF16)

        x = x_ref[...].reshape(x_ref.shape[0], -1).astype(BF16)
        gu = jnp.dot(x, wgu_bf[...], preferred_element_type=F32) + bgu_ref[0, 0]
        glu = jnp.minimum(gu[:, :D_EXPERT], SWIGLU_LIMIT)
        lin = jnp.clip(gu[:, D_EXPERT:], -SWIGLU_LIMIT, SWIGLU_LIMIT)
        act = glu * _sigmoid(SWIGLU_ALPHA * glu) * (lin + 1.0)
        y = jnp.dot(act.astype(BF16), wd_bf[...], preferred_element_type=F32) + bd_ref[0, 0]
        y_ref[...] = y.reshape(y_ref.shape)

    @pl.when(i >= nu_ref[0])
    def _():
        y_ref[...] = jnp.zeros_like(y_ref)


def _moe_experts(x_sorted, block_e, n_used, w_slot, next_e, w_gu, b_gu, w_down, b_down, layer):
    n_slots, tile_shape = x_sorted.shape[0], x_sorted.shape[1:]
    n_blocks = n_slots // MOE_BLOCK
    depth, d = w_gu.shape[0], w_gu.shape[2]
    slot_blk = pl.BlockSpec((MOE_BLOCK,) + tile_shape, lambda i, be, nu, sl, nx: (i, 0, 0))

    def expert(i, be, nu, sl, nx):
        return (layer, be[jnp.minimum(i, nu[0] - 1)], 0, 0)

    grid_spec = pltpu.PrefetchScalarGridSpec(
        num_scalar_prefetch=4,
        grid=(n_blocks,),
        in_specs=[
            slot_blk,
            pl.BlockSpec(memory_space=pl.ANY),
            pl.BlockSpec((1, 1, 1, 2 * D_EXPERT), expert),
            pl.BlockSpec(memory_space=pl.ANY),
            pl.BlockSpec((1, 1, 1, d), expert),
        ],
        out_specs=slot_blk,
        scratch_shapes=[pltpu.VMEM((2, d, 2 * D_EXPERT), F32), pltpu.VMEM((2, D_EXPERT, d), F32),
                        pltpu.VMEM((d, 2 * D_EXPERT), BF16), pltpu.VMEM((D_EXPERT, d), BF16),
                        pltpu.SemaphoreType.DMA((2, 2))],
    )
    return pl.pallas_call(
        functools.partial(_moe_kernel, layer=layer),
        grid_spec=grid_spec,
        out_shape=jax.ShapeDtypeStruct((n_slots,) + tile_shape, F32),
        compiler_params=_params("arbitrary"),
        name="moe_experts",
    )(block_e, n_used, w_slot, next_e, x_sorted, w_gu, b_gu.reshape(depth, N_EXPERTS, 1, -1), w_down,
      b_down.reshape(depth, N_EXPERTS, 1, -1))


def _comb_kernel(inv_ref, inv_next_ref, y_hbm, tg_ref, x_ref, mod_ref, fg_ref, o_ref, cbuf, sem,
                 *, n_lat, lat_per_batch, ctx_row, final):
    i = pl.program_id(0)
    n_steps = pl.num_programs(0)
    slot = i % 2
    rows = cbuf.shape[1]
    per_tile = TILE // COMB_TILE

    def issue(idx_ref, dst_slot):
        def body(r, carry):
            pltpu.make_async_copy(y_hbm.at[idx_ref[0, 0, r]], cbuf.at[dst_slot, r], sem.at[dst_slot]).start()
            return carry
        lax.fori_loop(0, rows, body, 0, unroll=16)

    @pl.when(i == 0)
    def _():
        issue(inv_ref, 0)

    @pl.when(i + 1 < n_steps)
    def _():
        issue(inv_next_ref, 1 - slot)

    pltpu.make_async_copy(y_hbm.at[pl.ds(0, rows)], cbuf.at[slot], sem.at[slot]).wait()

    r = _mod_row(i // per_tile, n_lat, lat_per_batch, ctx_row)
    g2 = mod_ref[5, pl.ds(r, 1), :]
    tg = tg_ref[...]
    f = None
    for kk in range(TOP_K):
        rows_k = cbuf[slot, kk * COMB_TILE:(kk + 1) * COMB_TILE].reshape(COMB_TILE, -1)
        f = tg[:, kk:kk + 1] * rows_k if f is None else f + tg[:, kk:kk + 1] * rows_k
    x2 = x_ref[...] + g2 * f
    o_ref[...] = _rms(x2, fg_ref[...]) if final else x2


def _moe_combine(y_sorted, dest, tg, x_all, mod_l, final_g, geo, final):
    t_moe = dest.shape[0]
    d = x_all.shape[1]
    n_steps = t_moe // COMB_TILE
    inv3 = dest.reshape(n_steps, COMB_TILE, TOP_K).transpose(0, 2, 1).reshape(n_steps, 1, TOP_K * COMB_TILE)
    kern = functools.partial(_comb_kernel, n_lat=geo["n_lat"], lat_per_batch=geo["lpb"], ctx_row=geo["ctx_row"],
                             final=final)
    return pl.pallas_call(
        kern,
        grid=(n_steps,),
        in_specs=[
            pl.BlockSpec((1, 1, TOP_K * COMB_TILE), lambda i: (i, 0, 0), memory_space=pltpu.SMEM),
            pl.BlockSpec((1, 1, TOP_K * COMB_TILE), lambda i: (jnp.minimum(i + 1, n_steps - 1), 0, 0),
                         memory_space=pltpu.SMEM),
            pl.BlockSpec(memory_space=pl.ANY),
            pl.BlockSpec((COMB_TILE, LANES), lambda i: (i, 0)),
            pl.BlockSpec((COMB_TILE, d), lambda i: (i, 0)),
            pl.BlockSpec(mod_l.shape, lambda i: (0, 0, 0)),
            pl.BlockSpec((1, d), lambda i: (0, 0)),
        ],
        out_specs=pl.BlockSpec((COMB_TILE, d), lambda i: (i, 0)),
        out_shape=jax.ShapeDtypeStruct((t_moe, d), F32),
        scratch_shapes=[pltpu.VMEM((2, TOP_K * COMB_TILE) + y_sorted.shape[1:], F32), pltpu.SemaphoreType.DMA((2,))],
        compiler_params=_params("arbitrary"),
        name="moe_combine",
    )(inv3, inv3, y_sorted, tg, x_all, mod_l, final_g.reshape(1, d))


def _route(te, rk, cnt, t_moe):
    counts = cnt[0, :N_EXPERTS].astype(I32)
    padded = (counts + MOE_BLOCK - 1) // MOE_BLOCK * MOE_BLOCK
    padded_end = jnp.cumsum(padded)
    padded_start = padded_end - padded
    experts = jnp.arange(N_EXPERTS, dtype=I32)
    seg_start = jnp.zeros_like(te)
    for e in range(N_EXPERTS):
        seg_start = jnp.where(te == e, padded_start[e], seg_start)
    dest = (rk + seg_start)[:t_moe, :TOP_K]
    n_blocks = -(-(t_moe * TOP_K) // MOE_BLOCK) + N_EXPERTS
    blk_start = jnp.arange(n_blocks, dtype=I32) * MOE_BLOCK
    block_e = jnp.minimum(jnp.sum((padded_end[None, :] <= blk_start[:, None]).astype(I32), axis=1), N_EXPERTS - 1)
    n_used = (padded_end[-1:] // MOE_BLOCK).astype(I32)
    zrow = jnp.where(counts > 0, padded_end - MOE_BLOCK, -1).astype(I32)
    nonempty = counts > 0
    ordinal = jnp.cumsum(nonempty.astype(I32)) - 1
    later = jnp.where(jnp.logical_and(nonempty[None, :], experts[None, :] > experts[:, None]), experts[None, :], N_EXPERTS)
    nxt = jnp.min(later, axis=1)
    nxt = jnp.where(nxt == N_EXPERTS, -1, nxt)
    w_slot = (jnp.take(ordinal, block_e) % 2).astype(I32)
    next_e = jnp.take(nxt, block_e).astype(I32)
    return dest.astype(I32), block_e, n_used, zrow, w_slot, next_e, n_blocks * MOE_BLOCK


def _block_diag_dense(w):
    g, i, j = w.shape
    out = jnp.zeros((g * i, g * j), w.dtype)
    for n in range(g):
        out = out.at[n * i:(n + 1) * i, n * j:(n + 1) * j].set(w[n])
    return out


def _pad_heads(w, heads, axis):
    shape = list(w.shape)
    shape[axis:axis + 1] = [heads, shape[axis] // heads]
    w = w.reshape(shape)
    pad = [(0, 0)] * w.ndim
    pad[axis + 1] = (0, LANES - shape[axis + 1])
    w = jnp.pad(w, pad)
    shape[axis:axis + 2] = [heads * LANES]
    return w.reshape(shape)


_ROPE_SWAP = np.concatenate([np.arange(8, 16), np.arange(0, 8), np.arange(24, 32), np.arange(16, 24)])


def _prep_in_weight(w_in):
    d = w_in.shape[0]
    o = np.cumsum([0, LRU_WIDTH, LRU_WIDTH, MLA_Q_RANK, MLA_KV_RANK, MLA_ROPE, ML_WIDTH, ML_WIDTH])
    a_xg = w_in[:, o[0]:o[2]]
    b_qkv = w_in[:, o[2]:o[4]]
    kr = w_in[:, o[4]:o[5]]
    z_nope = jnp.zeros((d, MLA_NOPE), w_in.dtype)
    z_tail = jnp.zeros((d, LANES - MLA_NOPE - MLA_ROPE), w_in.dtype)
    m_x = _pad_heads(w_in[:, o[5]:o[6]], ML_HEADS, 1)
    m_z = _pad_heads(w_in[:, o[6]:o[7]], ML_HEADS, 1)
    return jnp.concatenate([a_xg, b_qkv, z_nope, kr, z_tail, z_nope, kr[:, _ROPE_SWAP], z_tail, m_x, m_z],
                           axis=1).astype(BF16)


def _prep_out_weight(w_out):
    a = w_out[:LRU_WIDTH]
    b = w_out[LRU_WIDTH:LRU_WIDTH + MLA_HEADS * MLA_V]
    c = _pad_heads(w_out[LRU_WIDTH + MLA_HEADS * MLA_V:], ML_HEADS, 0)
    return jnp.concatenate([a, b, c], axis=0).astype(BF16)


def _prep_mla(q_norm_g, w_qb, kv_norm_g, w_kvb):
    scale = (MLA_NOPE + MLA_ROPE) ** -0.5 * math.log2(math.e)
    rq = w_qb.shape[0]
    wq = w_qb.reshape(rq, MLA_HEADS, MLA_NOPE + MLA_ROPE) * scale
    nope, rope = wq[..., :MLA_NOPE], wq[..., MLA_NOPE:]
    z32 = jnp.zeros((rq, MLA_HEADS, LANES - MLA_NOPE - MLA_ROPE), w_qb.dtype)
    w1 = jnp.concatenate([nope, rope, z32], axis=-1).reshape(rq, MLA_HEADS * LANES)
    w2 = jnp.concatenate([jnp.zeros_like(nope), rope[..., _ROPE_SWAP], z32], axis=-1).reshape(rq, MLA_HEADS * LANES)
    rk = w_kvb.shape[0]
    wkv = w_kvb.reshape(rk, MLA_HEADS, MLA_NOPE + MLA_V)
    z64 = jnp.zeros((rk, MLA_HEADS, LANES - MLA_NOPE), w_kvb.dtype)
    wk = jnp.concatenate([wkv[..., :MLA_NOPE], z64], axis=-1).reshape(rk, MLA_HEADS * LANES)
    wv = jnp.concatenate([wkv[..., MLA_NOPE:], z64], axis=-1).reshape(rk, MLA_HEADS * LANES)
    return {"gq": q_norm_g.reshape(1, -1), "w1": w1.astype(BF16), "w2": w2.astype(BF16),
            "gkv": kv_norm_g.reshape(1, -1), "wk": wk.astype(BF16), "wv": wv.astype(BF16)}


def _prep_mlstm(conv_w, conv_b, wq, wk, wv, w_gate_d, b_gate_d, norm_g, skip):
    def proj(w):
        return _pad_heads(_pad_heads(_block_diag_dense(w), ML_HEADS, 0), ML_HEADS, 1)

    wk_p = proj(wk) * (ML_HEAD_DIM ** -0.5)
    wg = jnp.concatenate([_pad_heads(w_gate_d[i * ML_WIDTH:(i + 1) * ML_WIDTH], ML_HEADS, 0) for i in range(3)], axis=0)
    ng = w_gate_d.shape[1]
    wg = jnp.pad(wg, ((0, 0), (0, LANES - ng)))
    b_gate_d = jnp.pad(b_gate_d, (0, LANES - ng))
    gate_rows = 2 * SUBLANES
    return {
        "cw": _pad_heads(conv_w, ML_HEADS, 1), "cb": _pad_heads(conv_b.reshape(1, -1), ML_HEADS, 1),
        "wq": proj(wq).astype(BF16), "wk": wk_p.astype(BF16), "wkt": wk_p.T.astype(BF16), "wv": proj(wv).astype(BF16),
        "wg": wg.astype(BF16), "wgt": wg.T[:gate_rows].astype(BF16),
        "bg": b_gate_d.reshape(1, -1), "bgt": b_gate_d[:gate_rows].reshape(-1, 1),
        "ng": _pad_heads(norm_g.reshape(1, -1), ML_HEADS, 1), "sk": _pad_heads(skip.reshape(1, -1), ML_HEADS, 1),
    }


def _rope_tables(seq, ctx_len):
    rows = seq // GRID_W
    row = jnp.repeat(jnp.arange(rows, dtype=I32), GRID_W)
    col = jnp.tile(jnp.arange(GRID_W, dtype=I32), rows)
    freqs = ROPE_BASE ** (-jnp.arange(ROPE_AXIS_FREQ, dtype=F32) / ROPE_AXIS_FREQ)
    ang_r, ang_c = row[:, None] * freqs, col[:, None] * freqs
    cos32 = jnp.concatenate([jnp.cos(ang_r), jnp.cos(ang_r), jnp.cos(ang_c), jnp.cos(ang_c)], axis=1)
    sin32 = jnp.concatenate([-jnp.sin(ang_r), jnp.sin(ang_r), -jnp.sin(ang_c), jnp.sin(ang_c)], axis=1)
    cos32 = jnp.concatenate([cos32, jnp.ones((ctx_len, MLA_ROPE), F32)], axis=0)
    sin32 = jnp.concatenate([sin32, jnp.zeros((ctx_len, MLA_ROPE), F32)], axis=0)
    n = seq + ctx_len
    tail = jnp.zeros((n, LANES - MLA_NOPE - MLA_ROPE), F32)
    cos_t = jnp.concatenate([jnp.ones((n, MLA_NOPE), F32), cos32, tail], axis=1)
    sin_t = jnp.concatenate([jnp.zeros((n, MLA_NOPE), F32), sin32, tail], axis=1)
    return cos_t, sin_t


def kernel(x, c, ctx, c_ctx, norm1_g, norm2_g, w_mod, b_mod, w_in, w_out, lru_conv_w, lru_conv_b, lru_wa, lru_ba,
           lru_wx, lru_bx, lru_lambda, mla_q_norm_g, mla_w_qb, mla_kv_norm_g, mla_w_kvb, ml_conv_w, ml_conv_b,
           ml_wq, ml_wk, ml_wv, ml_w_gate, ml_b_gate, ml_norm_g, ml_skip, w_router, b_router, w_gu, b_gu, w_down,
           b_down, final_g):
    bsz, seq, d = x.shape
    ctx_len = ctx.shape[1]
    depth = w_mod.shape[0]
    assert ctx_len == TILE and seq % ATTN_Q_TILE == 0 and bsz + 1 <= SUBLANES
    t_lat = bsz * seq
    t_all = t_lat + bsz * ctx_len
    geo = {"batch": bsz, "lpb": seq // TILE, "n_lat": t_lat // TILE, "ctx_row": bsz, "t_all": t_all}

    cv = jnp.zeros((SUBLANES, d), F32).at[:bsz].set(c).at[bsz].set(c_ctx)
    mod = _modulation(cv, w_mod, b_mod)
    cos_t, sin_t = _rope_tables(seq, ctx_len)
    x_all = jnp.concatenate([x.reshape(t_lat, d), ctx.reshape(bsz * ctx_len, d)], axis=0)

    out = None
    for l in range(depth):
        last = l == depth - 1
        u_a, u_b, u_m = _in_proj(x_all, mod[l], norm1_g[l], _prep_in_weight(w_in[l]), geo)

        hf = None
        for dd in range(2):
            wg = jnp.concatenate([_block_diag_dense(lru_wa[l, dd]), _block_diag_dense(lru_wx[l, dd])], axis=1)
            bg = jnp.concatenate([lru_ba[l, dd], lru_bx[l, dd]])
            hf = _lru_dir(u_a, hf, lru_conv_w[l], lru_conv_b[l], wg.astype(BF16), bg, lru_lambda[l, dd], geo, dd == 1)
        ya = hf

        q, k, v = _mla_proj(u_b, cos_t, sin_t, _prep_mla(mla_q_norm_g[l], mla_w_qb[l], mla_kv_norm_g[l], mla_w_kvb[l]), geo)
        tq = ATTN_Q_TILE
        yb = _flash(q, k, v, tq, 0, seq // tq, 0, seq + ctx_len, 0, t_lat)
        if not last:
            yb_c = _flash(q, k, v, TILE, seq // TILE, 1, seq, ctx_len, 0, bsz * ctx_len)
            yb = jnp.concatenate([yb, yb_c], axis=0)

        hf = None
        for dd in range(2):
            wts = _prep_mlstm(ml_conv_w[l], ml_conv_b[l], ml_wq[l], ml_wk[l], ml_wv[l], ml_w_gate[l, dd],
                              ml_b_gate[l, dd], ml_norm_g[l], ml_skip[l])
            hf = _mlstm_dir(u_m, hf, wts, geo, dd == 1)
        yc = hf

        n_tiles = (t_lat if last else t_all) // TILE
        x_mid, h2, te, tg, rk, cnt = _out_proj(ya, yb, yc, x_all, mod[l], _prep_out_weight(w_out[l]), norm2_g[l],
                                               w_router[l], b_router[l], geo, n_tiles)
        t_moe = n_tiles * TILE
        dest, block_e, n_used, zrow, w_slot, next_e, n_slots = _route(te, rk, cnt, t_moe)
        x_sorted = _moe_dispatch(h2, dest, zrow, n_used, n_slots)
        y_sorted = _moe_experts(x_sorted, block_e, n_used, w_slot, next_e, w_gu, b_gu, w_down, b_down, l)
        x_all = _moe_combine(y_sorted, dest, tg, x_mid, mod[l], final_g, geo, last)
        if last:
            out = x_all.reshape(bsz, seq, d)
    return out
```

```python
import functools
import math

import jax
import jax.numpy as jnp
import numpy as np
from jax import lax
from jax.experimental import pallas as pl
from jax.experimental.pallas import tpu as pltpu

F32 = jnp.float32
BF16 = jnp.bfloat16
I32 = jnp.int32
HIGHEST = lax.Precision.HIGHEST

LANES = 128
SUBLANES = 8
VMEM_LIMIT_BYTES = 56 * 1024 * 1024

GRID_W = 64
EPS = 1e-6
LRU_WIDTH = 256
LRU_C = 8.0
CONV_W = 4
MLA_HEADS = 8
MLA_NOPE = 64
MLA_ROPE = 32
MLA_V = 64
MLA_Q_RANK = 256
MLA_KV_RANK = 128
ROPE_AXIS_FREQ = MLA_ROPE // 4
ROPE_BASE = 10000.0
ML_HEADS = 4
ML_HEAD_DIM = 64
ML_WIDTH = ML_HEADS * ML_HEAD_DIM
ML_PAD = ML_HEADS * LANES
N_EXPERTS = 32
TOP_K = 4
D_EXPERT = 1024
SWIGLU_LIMIT = 7.0
SWIGLU_ALPHA = 1.702
MOE_BLOCK = 256

TILE = 256
HALO = SUBLANES
UB_W = MLA_Q_RANK + MLA_KV_RANK + 2 * LANES
COMB_TILE = 128
ATTN_Q_TILE = 1024

NT_DIMS = (((1,), (1,)), ((), ()))


def _params(*sem):
    return pltpu.CompilerParams(dimension_semantics=sem, vmem_limit_bytes=VMEM_LIMIT_BYTES)


def _sigmoid(x):
    return 1.0 / (1.0 + jnp.exp(-x))


def _log_sigmoid(x):
    return jnp.minimum(x, 0.0) - jnp.log1p(jnp.exp(-jnp.abs(x)))


def _split3(x):
    hi = x.astype(BF16)
    rest = x - hi.astype(F32)
    mid = rest.astype(BF16)
    return hi, mid, (rest - mid.astype(F32)).astype(BF16)


def _rms(x, g):
    return x * lax.rsqrt(jnp.mean(x * x, axis=-1, keepdims=True) + EPS) * g


def _mod_kernel(cv_ref, w_ref, b_ref, o_ref):
    cv = cv_ref[...]
    a = cv * _sigmoid(cv)
    o_ref[0, 0] = jnp.dot(a, w_ref[0], precision=HIGHEST, preferred_element_type=F32) + b_ref[0, 0]


def _modulation(cv, w_mod, b_mod):
    depth, d, _ = w_mod.shape
    rows = cv.shape[0]
    return pl.pallas_call(
        _mod_kernel,
        grid=(depth, 6),
        in_specs=[
            pl.BlockSpec((rows, d), lambda l, j: (0, 0)),
            pl.BlockSpec((1, d, d), lambda l, j: (l, 0, j)),
            pl.BlockSpec((1, 1, 1, d), lambda l, j: (l, j, 0, 0)),
        ],
        out_specs=pl.BlockSpec((1, 1, rows, d), lambda l, j: (l, j, 0, 0)),
        out_shape=jax.ShapeDtypeStruct((depth, 6, rows, d), F32),
        compiler_params=_params("arbitrary", "arbitrary"),
        name="modulation",
    )(cv, w_mod, b_mod.reshape(depth, 6, 1, d))


def _mod_row(i, n_lat, lat_per_batch, ctx_row):
    return jnp.where(i < n_lat, i // lat_per_batch, ctx_row)


def _in_kernel(x_ref, mod_ref, g_ref, w_ref, ua_ref, ub_ref, um_ref, *, n_lat, lat_per_batch, ctx_row):
    r = _mod_row(pl.program_id(0), n_lat, lat_per_batch, ctx_row)
    sh = mod_ref[0, pl.ds(r, 1), :]
    sc = mod_ref[1, pl.ds(r, 1), :]
    h = _rms(x_ref[...], g_ref[...]) * (1.0 + sc) + sh
    u = jnp.dot(h.astype(BF16), w_ref[...], preferred_element_type=F32)
    wa = ua_ref.shape[1]
    wb = ub_ref.shape[1]
    ua_ref[...] = u[:, :wa]
    ub_ref[...] = u[:, wa:wa + wb]
    um_ref[...] = u[:, wa + wb:]


def _in_proj(x_all, mod_l, g, w_in_p, geo):
    t_all, d = x_all.shape
    n_tiles = t_all // TILE
    wa, wb, wm = 2 * LRU_WIDTH, UB_W, 2 * ML_PAD
    kern = functools.partial(_in_kernel, n_lat=geo["n_lat"], lat_per_batch=geo["lpb"], ctx_row=geo["ctx_row"])
    return pl.pallas_call(
        kern,
        grid=(n_tiles,),
        in_specs=[
            pl.BlockSpec((TILE, d), lambda i: (i, 0)),
            pl.BlockSpec(mod_l.shape, lambda i: (0, 0, 0)),
            pl.BlockSpec((1, d), lambda i: (0, 0)),
            pl.BlockSpec(w_in_p.shape, lambda i: (0, 0)),
        ],
        out_specs=[
            pl.BlockSpec((TILE, wa), lambda i: (i, 0)),
            pl.BlockSpec((TILE, wb), lambda i: (i, 0)),
            pl.BlockSpec((TILE, wm), lambda i: (i, 0)),
        ],
        out_shape=[
            jax.ShapeDtypeStruct((t_all, wa), F32),
            jax.ShapeDtypeStruct((t_all, wb), F32),
            jax.ShapeDtypeStruct((t_all, wm), F32),
        ],
        compiler_params=_params("arbitrary"),
        name="in_proj",
    )(x_all, mod_l, g.reshape(1, d), w_in_p)


def _chunk_block(b, j, geo, rev):
    lat = (geo["lpb"] - j) if rev else (j - 1)
    return jnp.where(j == 0, geo["n_lat"] + b, b * geo["lpb"] + lat)


def _local_block(j, lpb, rev):
    return jnp.where(j == 0, lpb, (lpb - j) if rev else (j - 1))


def _chunk_specs(width, col, geo, rev, b):
    per = TILE // HALO
    last = geo["t_all"] // HALO - 1

    def cur(j):
        return (_chunk_block(b, j, geo, rev), col)

    def prev(j):
        return (jnp.maximum(_chunk_block(b, j, geo, rev) * per - 1, 0), col)

    def nxt(j):
        return (jnp.minimum((_chunk_block(b, j, geo, rev) + 1) * per, last), col)

    return [pl.BlockSpec((TILE, width), cur), pl.BlockSpec((HALO, width), prev), pl.BlockSpec((HALO, width), nxt)]


def _scan_call(kern, name, src, width, hf, weights, scratch, geo, rev):
    bsz, lpb = geo["batch"], geo["lpb"]
    chunk_of_all = pl.BlockSpec((bsz, TILE, width), lambda j: (0, _local_block(j, lpb, rev), 0))
    specs, args = [], []
    for b in range(bsz):
        specs += _chunk_specs(width, 0, geo, rev, b)
        args += [src, src, src]
        if rev:
            specs.append(pl.BlockSpec((TILE, width), lambda j, b=b: (_chunk_block(b, j, geo, rev), 1)))
            args.append(src)
    if rev:
        specs.append(chunk_of_all)
        args.append(hf)
    for wgt in weights:
        specs.append(pl.BlockSpec(wgt.shape, lambda j: (0, 0)))
        args.append(wgt)
    return pl.pallas_call(
        functools.partial(kern, rev=rev, lpb=lpb, bsz=bsz),
        grid=(lpb + 1,),
        in_specs=specs,
        out_specs=chunk_of_all,
        out_shape=jax.ShapeDtypeStruct((bsz, (lpb + 1) * TILE, width), BF16 if rev else F32),
        scratch_shapes=scratch,
        compiler_params=_params("arbitrary"),
        name=name,
    )(*args)


def _split_scan_refs(refs, rev, bsz, n_weights, n_scratch):
    n_in = 4 if rev else 3
    batch_refs = [refs[b * n_in:(b + 1) * n_in] for b in range(bsz)]
    w0 = bsz * n_in
    hf_ref = refs[w0] if rev else None
    w0 += 1 if rev else 0
    weights = refs[w0:w0 + n_weights]
    return batch_refs, hf_ref, weights, refs[w0 + n_weights], refs[len(refs) - n_scratch:]


def _short_conv(x, xp_ref, xn_ref, w_ref, b_ref, j, lpb, rev):
    n = x.shape[0]
    lat = (lpb - j) if rev else (j - 1)
    is_lat = j > 0
    prev_ok = jnp.logical_and(is_lat, lat > 0)
    next_ok = jnp.logical_and(is_lat, lat < lpb - 1)
    xp = xp_ref[...] * prev_ok.astype(F32)
    xn = xn_ref[...] * next_ok.astype(F32)
    row = lax.broadcasted_iota(I32, x.shape, 0)
    x_m1 = jnp.where(row == 0, xp[HALO - 1:HALO, :], pltpu.roll(x, 1, 0))
    x_m2 = jnp.where(row == 0, xp[HALO - 2:HALO - 1, :], jnp.where(row == 1, xp[HALO - 1:HALO, :], pltpu.roll(x, 2, 0)))
    x_p1 = jnp.where(row == n - 1, xn[0:1, :], pltpu.roll(x, n - 1, 0))
    return x_m2 * w_ref[0:1, :] + x_m1 * w_ref[1:2, :] + x * w_ref[2:3, :] + x_p1 * w_ref[3:4, :] + b_ref[...]


def _lin_scan(a, b, rev):
    n = a.shape[0]
    row = lax.broadcasted_iota(I32, a.shape, 0)
    d = 1
    while d < n:
        if rev:
            a_s, b_s, valid = pltpu.roll(a, n - d, 0), pltpu.roll(b, n - d, 0), row < n - d
        else:
            a_s, b_s, valid = pltpu.roll(a, d, 0), pltpu.roll(b, d, 0), row >= d
        a_s = jnp.where(valid, a_s, 1.0)
        b_s = jnp.where(valid, b_s, 0.0)
        b = a * b_s + b
        a = a * a_s
        d *= 2
        yield
    return a, b


def _lru_kernel(*refs, rev, lpb, bsz):
    batch_refs, hf_ref, (cw_ref, cb_ref, wg_ref, bg_ref, lam_ref), o_ref, (h_scr,) = _split_scan_refs(
        refs, rev, bsz, 5, 1)
    j = pl.program_id(0)

    @pl.when(j == 0)
    def _():
        h_scr[...] = jnp.zeros_like(h_scr)

    def chunk(b):
        x_ref, xp_ref, xn_ref = batch_refs[b][:3]
        x = x_ref[...]
        n, w = x.shape
        xc = _short_conv(x, xp_ref, xn_ref, cw_ref, cb_ref, j, lpb, rev)
        yield
        gates = jnp.dot(xc.astype(BF16), wg_ref[...], preferred_element_type=F32) + bg_ref[...]
        yield
        r = _sigmoid(gates[:, :w])
        ig = _sigmoid(gates[:, w:])
        log_a = LRU_C * r * _log_sigmoid(lam_ref[...])
        a = jnp.exp(log_a)
        bb = jnp.sqrt(-jnp.tanh(log_a) * (1.0 + a * a)) * ig * xc
        yield
        a_cum, h_loc = yield from _lin_scan(a, bb, rev)
        h = a_cum * h_scr[b] + h_loc
        h_scr[b] = h[0:1, :] if rev else h[n - 1:n, :]
        if rev:
            g_ref = batch_refs[b][3]
            o_ref[b] = ((hf_ref[b] + h) * jax.nn.gelu(g_ref[...], approximate=True)).astype(o_ref.dtype)
        else:
            o_ref[b] = h

    _round_robin([chunk(b) for b in range(bsz)])


def _lru_dir(u_a, hf, cw, cb, wg, bg, lam, geo, rev):
    w = LRU_WIDTH
    weights = [cw, cb.reshape(1, w), wg, bg.reshape(1, 2 * w), lam.reshape(1, w)]
    scratch = [pltpu.VMEM((geo["batch"], 1, w), F32)]
    return _scan_call(_lru_kernel, "rglru_bwd" if rev else "rglru_fwd", u_a, w, hf, weights, scratch, geo, rev)


def _mlstm_kernel(*refs, rev, lpb, bsz):
    n_weights = 12 if rev else 10
    batch_refs, hf_ref, weights, o_ref, (c_scr, m_scr) = _split_scan_refs(refs, rev, bsz, n_weights, 2)
    j = pl.program_id(0)

    @pl.when(j == 0)
    def _():
        c_scr[...] = jnp.zeros_like(c_scr)
        m_scr[...] = jnp.zeros_like(m_scr)

    _round_robin([_mlstm_chunk(batch_refs[b], hf_ref, weights, o_ref, c_scr, m_scr, b, j, rev, lpb)
                  for b in range(bsz)])


def _round_robin(stage_generators):
    live = list(stage_generators)
    while live:
        live = [g for g in live if next(g, StopIteration) is not StopIteration]


def _mlstm_chunk(in_refs, hf_ref, weights, o_ref, c_scr, m_scr, b, j, rev, lpb):
    if rev:
        x_ref, xp_ref, xn_ref, z_ref = in_refs
        cw_ref, cb_ref, wq_ref, wk_ref, wkt_ref, wv_ref, wg_ref, wgt_ref, bg_ref, bgt_ref, ng_ref, sk_ref = weights
    else:
        x_ref, xp_ref, xn_ref = in_refs
        cw_ref, cb_ref, wq_ref, wk_ref, wkt_ref, wv_ref, wg_ref, wgt_ref, bg_ref, bgt_ref = weights
    x = x_ref[...]
    n = x.shape[0]
    xc = _short_conv(x, xp_ref, xn_ref, cw_ref, cb_ref, j, lpb, rev)
    xc = xc * _sigmoid(xc)
    xcb = xc.astype(BF16)
    yield
    q = jnp.dot(xcb, wq_ref[...], preferred_element_type=F32)
    k = jnp.dot(xcb, wk_ref[...], preferred_element_type=F32)
    yield
    kt =lax.dot_general(wkt_ref[...], xcb, NT_DIMS, preferred_element_type=F32)
    lane_w = lax.broadcasted_iota(I32, (1, ML_PAD), 1)
    ones_lane = (lane_w % LANES == ML_HEAD_DIM).astype(F32)
    v = jnp.dot(x.astype(BF16), wv_ref[...], preferred_element_type=F32) + ones_lane
    yield
    qkv =jnp.concatenate([q, k, v], axis=1).astype(BF16)
    g_col = jnp.dot(qkv, wg_ref[...], preferred_element_type=F32) + bg_ref[...]
    g_row = lax.dot_general(wgt_ref[...], qkv, NT_DIMS, preferred_element_type=F32) + bgt_ref[...]
    yield
    ti =lax.broadcasted_iota(I32, (n, n), 0)
    si = lax.broadcasted_iota(I32, (n, n), 1)
    mask = (si >= ti) if rev else (si <= ti)
    tri = mask.astype(BF16)
    tri_t = ((ti >= si) if rev else (ti <= si)).astype(BF16)
    b_col = sum(jnp.dot(tri, part, preferred_element_type=F32) for part in _split3(_log_sigmoid(g_col)))
    b_row = sum(jnp.dot(part, tri_t, preferred_element_type=F32) for part in _split3(_log_sigmoid(g_row)))
    yield
    lane =lax.broadcasted_iota(I32, (1, LANES), 1)
    num_mask = (lane < ML_HEAD_DIM).astype(F32)
    den_mask = (lane == ML_HEAD_DIM).astype(F32)
    last = 0 if rev else n - 1
    outs = []
    for h in range(ML_HEADS):
        hs = slice(h * LANES, (h + 1) * LANES)
        bc = b_col[:, ML_HEADS + h:ML_HEADS + h + 1]
        br = b_row[ML_HEADS + h:ML_HEADS + h + 1, :]
        ic = g_col[:, h:h + 1]
        ir = g_row[h:h + 1, :]
        m_prev = m_scr[b, h:h + 1, 0:1]
        dmat = jnp.where(mask, bc - br + ir, -jnp.inf)
        inter = bc + m_prev
        m_t = jnp.maximum(inter, jnp.max(dmat, axis=1, keepdims=True))
        yield
        s =jnp.dot(q[:, hs].astype(BF16), kt[hs, :].astype(BF16), preferred_element_type=F32)
        p = s * jnp.exp(dmat - m_t)
        yield
        w_inter =jnp.exp(inter - m_t)
        vh = v[:, hs]
        c_old = c_scr[b * ML_HEADS + h]
        numden = (jnp.dot(p.astype(BF16), vh.astype(BF16), preferred_element_type=F32)
                  + w_inter * jnp.dot(q[:, hs].astype(BF16), c_old.astype(BF16), preferred_element_type=F32))
        den = jnp.sum(numden * den_mask, axis=1, keepdims=True)
        hh = numden * num_mask / jnp.maximum(jnp.abs(den), jnp.exp(-m_t))
        yield
        b_last =bc[last:last + 1, :]
        ws_col = b_last - bc + ic
        m_new = jnp.maximum(b_last + m_prev, jnp.max(ws_col, axis=0, keepdims=True))
        decay = jnp.exp(b_last + m_prev - m_new)
        wv = (jnp.exp(ws_col - m_new) * vh).astype(BF16)
        c_scr[b * ML_HEADS + h] = decay * c_old + jnp.dot(kt[hs, :].astype(BF16), wv, preferred_element_type=F32)
        m_scr[b, h:h + 1, :] = jnp.broadcast_to(m_new, (1, LANES))
        yield
        if rev:
            hsum = hf_ref[b, :, hs] + hh
            mu = jnp.sum(hsum, axis=1, keepdims=True) * (1.0 / ML_HEAD_DIM)
            cen = (hsum - mu) * num_mask
            var = jnp.sum(cen * cen, axis=1, keepdims=True) * (1.0 / ML_HEAD_DIM)
            hh = cen * lax.rsqrt(var + EPS)
        outs.append(hh)
    hcat = jnp.concatenate(outs, axis=1)
    if rev:
        z = z_ref[...]
        o_ref[b] = ((hcat * ng_ref[...] + sk_ref[...] * xc) * (z * _sigmoid(z))).astype(o_ref.dtype)
    else:
        o_ref[b] = hcat


def _mlstm_dir(u_m, hf, wts, geo, rev):
    names = ["cw", "cb", "wq", "wk", "wkt", "wv", "wg", "wgt", "bg", "bgt"] + (["ng", "sk"] if rev else [])
    bsz = geo["batch"]
    scratch = [pltpu.VMEM((bsz * ML_HEADS, LANES, LANES), F32), pltpu.VMEM((bsz, SUBLANES, LANES), F32)]
    return _scan_call(_mlstm_kernel, "mlstm_bwd" if rev else "mlstm_fwd", u_m, ML_PAD, hf, [wts[nm] for nm in names],
                      scratch, geo, rev)


def _mla_proj_kernel(ub_ref, cos_ref, sin_ref, gq_ref, w1_ref, w2_ref, gkv_ref, wk_ref, wv_ref, q_ref, k_ref, v_ref):
    ub = ub_ref[...]
    cos = cos_ref[...]
    sin = sin_ref[...]
    qn = _rms(ub[:, :MLA_Q_RANK], gq_ref[...]).astype(BF16)
    qa = jnp.dot(qn, w1_ref[...], preferred_element_type=F32)
    qb = jnp.dot(qn, w2_ref[...], preferred_element_type=F32)
    kvn = _rms(ub[:, MLA_Q_RANK:MLA_Q_RANK + MLA_KV_RANK], gkv_ref[...]).astype(BF16)
    kn = jnp.dot(kvn, wk_ref[...], preferred_element_type=F32)
    lane = lax.broadcasted_iota(I32, (1, LANES), 1)
    ones_lane = (lane == MLA_V).astype(F32)
    vn = jnp.dot(kvn, wv_ref[...], preferred_element_type=F32)
    off = MLA_Q_RANK + MLA_KV_RANK
    kr = ub[:, off:off + LANES] * cos + ub[:, off + LANES:off + 2 * LANES] * sin
    for h in range(MLA_HEADS):
        hs = slice(h * LANES, (h + 1) * LANES)
        q_ref[0, h] = (qa[:, hs] * cos + qb[:, hs] * sin).astype(BF16)
        k_ref[0, h] = (kn[:, hs] + kr).astype(BF16)
        v_ref[0, h] = (vn[:, hs] + ones_lane).astype(BF16)


def _mla_proj(u_b, cos_t, sin_t, wts, geo):
    n_tiles = geo["t_all"] // TILE
    n_lat, lpb, bsz = geo["n_lat"], geo["lpb"], geo["batch"]
    hw = MLA_HEADS * LANES

    def batch_of(i):
        return jnp.where(i < n_lat, i // lpb, i - n_lat)

    def blk_of(i):
        return jnp.where(i < n_lat, i % lpb, lpb)

    const = lambda i: (0, 0)
    head_spec = pl.BlockSpec((1, MLA_HEADS, TILE, LANES), lambda i: (batch_of(i), 0, blk_of(i), 0))
    head_shape = jax.ShapeDtypeStruct((bsz, MLA_HEADS, (lpb + 1) * TILE, LANES), BF16)
    return pl.pallas_call(
        _mla_proj_kernel,
        grid=(n_tiles,),
        in_specs=[
            pl.BlockSpec((TILE, UB_W), lambda i: (i, 0)),
            pl.BlockSpec((TILE, LANES), lambda i: (blk_of(i), 0)),
            pl.BlockSpec((TILE, LANES), lambda i: (blk_of(i), 0)),
            pl.BlockSpec((1, MLA_Q_RANK), const),
            pl.BlockSpec((MLA_Q_RANK, hw), const),
            pl.BlockSpec((MLA_Q_RANK, hw), const),
            pl.BlockSpec((1, MLA_KV_RANK), const),
            pl.BlockSpec((MLA_KV_RANK, hw), const),
            pl.BlockSpec((MLA_KV_RANK, hw), const),
        ],
        out_specs=[head_spec, head_spec, head_spec],
        out_shape=[head_shape, head_shape, head_shape],
        compiler_params=_params("arbitrary"),
        name="mla_proj",
    )(u_b, cos_t, sin_t, wts["gq"], wts["w1"], wts["w2"], wts["gkv"], wts["wk"], wts["wv"])


def _flash_kernel(q_ref, k_ref, v_ref, o_ref, m_scr, acc_scr, *, k_start, nk, tk):
    lane = lax.broadcasted_iota(I32, (1, LANES), 1)
    den_mask = (lane == MLA_V).astype(F32)
    outs = []
    for j in range(2):
        q = q_ref[0, j]
        m_scr[...] = jnp.full_like(m_scr, -jnp.inf)
        acc_scr[...] = jnp.zeros_like(acc_scr)

        def body(i, carry, j=j, q=q):
            start = k_start + i * tk
            kk = k_ref[0, j, start:start + tk, :]
            vv = v_ref[0, j, start:start + tk, :]
            s = lax.dot_general(q, kk, NT_DIMS, preferred_element_type=F32)
            cols = [s[:, c * LANES:(c + 1) * LANES] for c in range(tk // LANES)]
            mp = cols[0]
            for sc in cols[1:]:
                mp = jnp.maximum(mp, sc)
            m_old = m_scr[...]
            m_new = jnp.maximum(m_old, jnp.broadcast_to(jnp.max(mp, axis=1, keepdims=True), mp.shape))
            p = jnp.concatenate([jnp.exp2(sc - m_new).astype(BF16) for sc in cols], axis=1)
            acc_scr[...] = jnp.exp2(m_old - m_new) * acc_scr[...] + jnp.dot(p, vv, preferred_element_type=F32)
            m_scr[...] = m_new
            return carry

        for i in range(nk):
            body(i, 0)
        acc = acc_scr[...]
        den = jnp.sum(acc * den_mask, axis=1, keepdims=True)
        outs.append(acc / den)
    o_ref[...] = jnp.where(lane < MLA_V, outs[0], pltpu.roll(outs[1], MLA_V, 1)).astype(o_ref.dtype)


def _key_tile(n):
    for cand in range(min(n, 1024) // LANES * LANES, 0, -LANES):
        if n % cand == 0:
            return cand
    raise ValueError(n)


def _flash(q, k, v, tq, q_blk0, nq, k_start, k_len, out_blk0, t_all):
    bsz, heads, rows, _ = q.shape
    tk = _key_tile(k_len)
    kern = functools.partial(_flash_kernel, k_start=k_start, nk=k_len // tk, tk=tk)
    return pl.pallas_call(
        kern,
        grid=(bsz, heads // 2, nq),
        in_specs=[
            pl.BlockSpec((1, 2, tq, LANES), lambda b, h, i: (b, h, q_blk0 + i, 0)),
            pl.BlockSpec((1, 2, rows, LANES), lambda b, h, i: (b, h, 0, 0)),
            pl.BlockSpec((1, 2, rows, LANES), lambda b, h, i: (b, h, 0, 0)),
        ],
        out_specs=pl.BlockSpec((tq, LANES), lambda b, h, i: (out_blk0 + b * nq + i, h)),
        out_shape=jax.ShapeDtypeStruct((t_all, heads * MLA_V), BF16),
        scratch_shapes=[pltpu.VMEM((tq, LANES), F32), pltpu.VMEM((tq, LANES), F32)],
        compiler_params=_params("arbitrary", "arbitrary", "arbitrary"),
        name="mla_attention",
    )(q, k, v)


def _out_kernel(ya_ref, yb_ref, yc_ref, x_ref, mod_ref, w_ref, g_ref, wr_ref, br_ref,
                xo_ref, h2_ref, te_ref, tg_ref, rk_ref, cnt_ref, *, n_lat, lat_per_batch, ctx_row):
    i = pl.program_id(0)

    @pl.when(i == 0)
    def _():
        cnt_ref[...] = jnp.zeros_like(cnt_ref)

    r = _mod_row(i, n_lat, lat_per_batch, ctx_row)
    g1 = mod_ref[2, pl.ds(r, 1), :]
    sh2 = mod_ref[3, pl.ds(r, 1), :]
    sc2 = mod_ref[4, pl.ds(r, 1), :]
    y = jnp.concatenate([ya_ref[0], yb_ref[...], yc_ref[0]], axis=1)
    x1 = x_ref[...] + g1 * jnp.dot(y, w_ref[...], preferred_element_type=F32)
    xo_ref[...] = x1
    h2 = _rms(x1, g_ref[...]) * (1.0 + sc2) + sh2
    h2_ref[...] = h2.reshape(h2_ref.shape)
    h_hi, h_mid, _ = _split3(h2)
    logits = (jnp.dot(h_hi, wr_ref[0], preferred_element_type=F32)
              + (jnp.dot(h_mid, wr_ref[0], preferred_element_type=F32)
                 + jnp.dot(h_hi, wr_ref[1], preferred_element_type=F32))) + br_ref[...]
    n, ne = logits.shape
    lane_e = lax.broadcasted_iota(I32, (n, ne), 1).astype(F32)
    lane_o = lax.broadcasted_iota(I32, (n, LANES), 1)
    vals, idxs = [], []
    for _ in range(TOP_K):
        m = jnp.max(logits, axis=1, keepdims=True)
        idx = jnp.min(jnp.where(logits == m, lane_e, float(ne)), axis=1, keepdims=True)
        logits = jnp.where(lane_e == idx, -jnp.inf, logits)
        vals.append(m)
        idxs.append(idx)
    exps = [jnp.exp(vv - vals[0]) for vv in vals]
    tot = exps[0] + exps[1] + exps[2] + exps[3]
    hits = [(lane_e == idx).astype(F32) for idx in idxs]
    chosen = hits[0] + hits[1] + hits[2] + hits[3]
    ti = lax.broadcasted_iota(I32, (n, n), 0)
    si = lax.broadcasted_iota(I32, (n, n), 1)
    before = jnp.dot((si < ti).astype(BF16), chosen.astype(BF16), preferred_element_type=F32) + cnt_ref[0:1, :]
    cnt_ref[...] = cnt_ref[...] + jnp.sum(chosen, axis=0, keepdims=True)
    te = jnp.zeros((n, LANES), F32)
    tg = jnp.zeros((n, LANES), F32)
    rk = jnp.zeros((n, LANES), F32)
    for kk in range(TOP_K):
        te = jnp.where(lane_o == kk, idxs[kk], te)
        tg = jnp.where(lane_o == kk, exps[kk] / tot, tg)
        rk = jnp.where(lane_o == kk, jnp.sum(hits[kk] * before, axis=1, keepdims=True), rk)
    te_ref[...] = te.astype(I32)
    tg_ref[...] = tg
    rk_ref[...] = rk.astype(I32)


def _out_proj(ya, yb, yc, x_all, mod_l, w_out_p, g, w_router, b_router, geo, n_tiles):
    d = x_all.shape[1]
    rows = n_tiles * TILE
    kern = functools.partial(_out_kernel, n_lat=geo["n_lat"], lat_per_batch=geo["lpb"], ctx_row=geo["ctx_row"])
    row_blk = lambda i: (i, 0)
    const = lambda i: (0, 0)
    n_lat, lpb = geo["n_lat"], geo["lpb"]
    scan_blk = lambda i: (jnp.where(i < n_lat, i // lpb, i - n_lat), jnp.where(i < n_lat, i % lpb, lpb), 0)
    w_router = jnp.pad(w_router, ((0, 0), (0, LANES - N_EXPERTS)))
    w_router = jnp.stack(_split3(w_router)[:2])
    b_router = jnp.pad(b_router, (0, LANES - N_EXPERTS), constant_values=-jnp.inf)
    return pl.pallas_call(
        kern,
        grid=(n_tiles,),
        in_specs=[
            pl.BlockSpec((1, TILE, ya.shape[2]), scan_blk),
            pl.BlockSpec((TILE, yb.shape[1]), row_blk),
            pl.BlockSpec((1, TILE, yc.shape[2]), scan_blk),
            pl.BlockSpec((TILE, d), row_blk),
            pl.BlockSpec(mod_l.shape, lambda i: (0, 0, 0)),
            pl.BlockSpec(w_out_p.shape, const),
            pl.BlockSpec((1, d), const),
            pl.BlockSpec(w_router.shape, lambda i: (0, 0, 0)),
            pl.BlockSpec((1, LANES), const),
        ],
        out_specs=[pl.BlockSpec((TILE, d), row_blk), pl.BlockSpec((TILE, SUBLANES, d // SUBLANES), lambda i: (i, 0, 0)),
                   pl.BlockSpec((TILE, LANES), row_blk), pl.BlockSpec((TILE, LANES), row_blk),
                   pl.BlockSpec((TILE, LANES), row_blk), pl.BlockSpec((SUBLANES, LANES), const)],
        out_shape=[jax.ShapeDtypeStruct((rows, d), F32), jax.ShapeDtypeStruct((rows, SUBLANES, d // SUBLANES), F32),
                   jax.ShapeDtypeStruct((rows, LANES), I32), jax.ShapeDtypeStruct((rows, LANES), F32),
                   jax.ShapeDtypeStruct((rows, LANES), I32), jax.ShapeDtypeStruct((SUBLANES, LANES), F32)],
        compiler_params=_params("arbitrary"),
        name="out_proj_router",
    )(ya, yb, yc, x_all, mod_l, w_out_p, g.reshape(1, d), w_router, b_router.reshape(1, LANES))


def _dispatch_kernel(zrow_ref, nu_ref, dest_ref, h_ref, xs_hbm, zbuf, zsem, sem):
    i = pl.program_id(0)
    n_tok = h_ref.shape[0]
    n_blocks = xs_hbm.shape[0] // MOE_BLOCK

    def zero_copy(blk_row):
        row = pl.multiple_of(blk_row, MOE_BLOCK)
        return pltpu.make_async_copy(zbuf, xs_hbm.at[pl.ds(row, MOE_BLOCK)], zsem)

    @pl.when(i == 0)
    def _():
        zbuf[...] = jnp.zeros_like(zbuf)
        for start in (True, False):
            for e in range(N_EXPERTS):
                @pl.when(zrow_ref[e] >= 0)
                def _(e=e, start=start):
                    zero_copy(zrow_ref[e]).start() if start else zero_copy(zrow_ref[e]).wait()

            def tail(b, carry, start=start):
                zero_copy(b * MOE_BLOCK).start() if start else zero_copy(b * MOE_BLOCK).wait()
                return carry
            lax.fori_loop(nu_ref[0], n_blocks, tail, 0)

    for kk in range(TOP_K):
        def issue(t2, carry, kk=kk):
            for prio in range(2):
                t = 2 * t2 + prio
                pltpu.make_async_copy(h_ref.at[t], xs_hbm.at[dest_ref[0, 0, kk * n_tok + t]], sem).start(priority=prio)
            return carry
        lax.fori_loop(0, n_tok // 2, issue, 0, unroll=8)
    for _ in range(TOP_K):
        pltpu.make_async_copy(h_ref, xs_hbm.at[pl.ds(0, n_tok)], sem).wait()


def _moe_dispatch(h2, dest, zrow, n_used, n_slots):
    t_moe, tile_shape = dest.shape[0], h2.shape[1:]
    n_tiles = t_moe // TILE
    dest3 = dest.reshape(n_tiles, TILE, TOP_K).transpose(0, 2, 1).reshape(n_tiles, 1, TOP_K * TILE)
    grid_spec = pltpu.PrefetchScalarGridSpec(
        num_scalar_prefetch=2,
        grid=(n_tiles,),
        in_specs=[
            pl.BlockSpec((1, 1, TOP_K * TILE), lambda i, zr, nu: (i, 0, 0), memory_space=pltpu.SMEM),
            pl.BlockSpec((TILE,) + tile_shape, lambda i, zr, nu: (i, 0, 0)),
        ],
        out_specs=pl.BlockSpec(memory_space=pl.ANY),
        scratch_shapes=[pltpu.VMEM((MOE_BLOCK,) + tile_shape, F32), pltpu.SemaphoreType.DMA, pltpu.SemaphoreType.DMA],
    )
    return pl.pallas_call(
        _dispatch_kernel,
        grid_spec=grid_spec,
        out_shape=jax.ShapeDtypeStruct((n_slots,) + tile_shape, F32),
        compiler_params=_params("arbitrary"),
        name="moe_dispatch",
    )(zrow, n_used, dest3, h2)


def _moe_kernel(be_ref, nu_ref, slot_ref, nxt_ref, x_ref, wgu_hbm, bgu_ref, wd_hbm, bd_ref, y_ref,
                wgu_f32, wd_f32, wgu_bf, wd_bf, sem, *, layer):
    i = pl.program_id(0)

    def weight_copies(e, s):
        return (pltpu.make_async_copy(wgu_hbm.at[layer, e], wgu_f32.at[s], sem.at[0, s]),
                pltpu.make_async_copy(wd_hbm.at[layer, e], wd_f32.at[s], sem.at[1, s]))

    @pl.when(i < nu_ref[0])
    def _():
        e, s = be_ref[i], slot_ref[i]

        @pl.when(i == 0)
        def _():
            for cp in weight_copies(e, s):
                cp.start()

        @pl.when(jnp.logical_or(i == 0, e != be_ref[jnp.maximum(i - 1, 0)]))
        def _():
            for cp in weight_copies(e, s):
                cp.wait()

            @pl.when(nxt_ref[i] >= 0)
            def _():
                for cp in weight_copies(nxt_ref[i], 1 - s):
                    cp.start()

            wgu_bf[...] = wgu_f32[s].astype(BF16)
            wd_bf[...] = wd_f32[s].astype(BF16)

        x = x_ref[...].reshape(x_ref.shape[0], -1).astype(BF16)
        gu = jnp.dot(x, wgu_bf[...], preferred_element_type=F32) + bgu_ref[0, 0]
        glu = jnp.minimum(gu[:, :D_EXPERT], SWIGLU_LIMIT)
        lin = jnp.clip(gu[:, D_EXPERT:], -SWIGLU_LIMIT, SWIGLU_LIMIT)
        act = glu * _sigmoid(SWIGLU_ALPHA * glu) * (lin + 1.0)
        y = jnp.dot(act.astype(BF16), wd_bf[...], preferred_element_type=F32) + bd_ref[0, 0]
        y_ref[...] = y.reshape(y_ref.shape)

    @pl.when(i >= nu_ref[0])
    def _():
        y_ref[...] = jnp.zeros_like(y_ref)


def _moe_experts(x_sorted, block_e, n_used, w_slot, next_e, w_gu, b_gu, w_down, b_down, layer):
    n_slots, tile_shape = x_sorted.shape[0], x_sorted.shape[1:]
    n_blocks = n_slots // MOE_BLOCK
    depth, d = w_gu.shape[0], w_gu.shape[2]
    slot_blk = pl.BlockSpec((MOE_BLOCK,) + tile_shape, lambda i, be, nu, sl, nx: (i, 0, 0))

    def expert(i, be, nu, sl, nx):
        return (layer, be[jnp.minimum(i, nu[0] - 1)], 0, 0)

    grid_spec = pltpu.PrefetchScalarGridSpec(
        num_scalar_prefetch=4,
        grid=(n_blocks,),
        in_specs=[
            slot_blk,
            pl.BlockSpec(memory_space=pl.ANY),
            pl.BlockSpec((1, 1, 1, 2 * D_EXPERT), expert),
            pl.BlockSpec(memory_space=pl.ANY),
            pl.BlockSpec((1, 1, 1, d), expert),
        ],
        out_specs=slot_blk,
        scratch_shapes=[pltpu.VMEM((2, d, 2 * D_EXPERT), F32), pltpu.VMEM((2, D_EXPERT, d), F32),
                        pltpu.VMEM((d, 2 * D_EXPERT), BF16), pltpu.VMEM((D_EXPERT, d), BF16),
                        pltpu.SemaphoreType.DMA((2, 2))],
    )
    return pl.pallas_call(
        functools.partial(_moe_kernel, layer=layer),
        grid_spec=grid_spec,
        out_shape=jax.ShapeDtypeStruct((n_slots,) + tile_shape, F32),
        compiler_params=_params("arbitrary"),
        name="moe_experts",
    )(block_e, n_used, w_slot, next_e, x_sorted, w_gu, b_gu.reshape(depth, N_EXPERTS, 1, -1), w_down,
      b_down.reshape(depth, N_EXPERTS, 1, -1))


def _comb_kernel(inv_ref, inv_next_ref, y_hbm, tg_ref, x_ref, mod_ref, fg_ref, o_ref, cbuf, sem,
                 *, n_lat, lat_per_batch, ctx_row, final):
    i = pl.program_id(0)
    n_steps = pl.num_programs(0)
    slot = i % 2
    rows = cbuf.shape[1]
    per_tile = TILE // COMB_TILE

    def issue(idx_ref, dst_slot):
        def body(r2, carry):
            for prio in range(2):
                r = 2 * r2 + prio
                pltpu.make_async_copy(y_hbm.at[idx_ref[0, 0, r]], cbuf.at[dst_slot, r], sem.at[dst_slot]).start(
                    priority=prio)
            return carry
        lax.fori_loop(0, rows // 2, body, 0, unroll=8)

    @pl.when(i == 0)
    def _():
        issue(inv_ref, 0)

    @pl.when(i + 1 < n_steps)
    def _():
        issue(inv_next_ref, 1 - slot)

    pltpu.make_async_copy(y_hbm.at[pl.ds(0, rows)], cbuf.at[slot], sem.at[slot]).wait()

    r = _mod_row(i // per_tile, n_lat, lat_per_batch, ctx_row)
    g2 = mod_ref[5, pl.ds(r, 1), :]
    tg = tg_ref[...]
    f = None
    for kk in range(TOP_K):
        rows_k = cbuf[slot, kk * COMB_TILE:(kk + 1) * COMB_TILE].reshape(COMB_TILE, -1)
        f = tg[:, kk:kk + 1] * rows_k if f is None else f + tg[:, kk:kk + 1] * rows_k
    x2 = x_ref[...] + g2 * f
    o_ref[...] = _rms(x2, fg_ref[...]) if final else x2


def _moe_combine(y_sorted, dest, tg, x_all, mod_l, final_g, geo, final):
    t_moe = dest.shape[0]
    d = x_all.shape[1]
    n_steps = t_moe // COMB_TILE
    inv3 = dest.reshape(n_steps, COMB_TILE, TOP_K).transpose(0, 2, 1).reshape(n_steps, 1, TOP_K * COMB_TILE)
    kern = functools.partial(_comb_kernel, n_lat=geo["n_lat"], lat_per_batch=geo["lpb"], ctx_row=geo["ctx_row"],
                             final=final)
    return pl.pallas_call(
        kern,
        grid=(n_steps,),
        in_specs=[
            pl.BlockSpec((1, 1, TOP_K * COMB_TILE), lambda i: (i, 0, 0), memory_space=pltpu.SMEM),
            pl.BlockSpec((1, 1, TOP_K * COMB_TILE), lambda i: (jnp.minimum(i + 1, n_steps - 1), 0, 0),
                         memory_space=pltpu.SMEM),
            pl.BlockSpec(memory_space=pl.ANY),
            pl.BlockSpec((COMB_TILE, LANES), lambda i: (i, 0)),
            pl.BlockSpec((COMB_TILE, d), lambda i: (i, 0)),
            pl.BlockSpec(mod_l.shape, lambda i: (0, 0, 0)),
            pl.BlockSpec((1, d), lambda i: (0, 0)),
        ],
        out_specs=pl.BlockSpec((COMB_TILE, d), lambda i: (i, 0)),
        out_shape=jax.ShapeDtypeStruct((t_moe, d), F32),
        scratch_shapes=[pltpu.VMEM((2, TOP_K * COMB_TILE) + y_sorted.shape[1:], F32), pltpu.SemaphoreType.DMA((2,))],
        compiler_params=_params("arbitrary"),
        name="moe_combine",
    )(inv3, inv3, y_sorted, tg, x_all, mod_l, final_g.reshape(1, d))


def _route(te, rk, cnt, t_moe):
    counts = cnt[0, :N_EXPERTS].astype(I32)
    padded = (counts + MOE_BLOCK - 1) // MOE_BLOCK * MOE_BLOCK
    padded_end = jnp.cumsum(padded)
    padded_start = padded_end - padded
    experts = jnp.arange(N_EXPERTS, dtype=I32)
    e_tok = te[:t_moe, :TOP_K]
    dest = rk[:t_moe, :TOP_K] + jnp.sum(jnp.where(e_tok[..., None] == experts, padded_start, 0), axis=-1)
    n_blocks = -(-(t_moe * TOP_K) // MOE_BLOCK) + N_EXPERTS
    blk_start = jnp.arange(n_blocks, dtype=I32) * MOE_BLOCK
    block_e = jnp.minimum(jnp.sum((padded_end[None, :] <= blk_start[:, None]).astype(I32), axis=1), N_EXPERTS - 1)
    n_used = (padded_end[-1:] // MOE_BLOCK).astype(I32)
    zrow = jnp.where(counts > 0, padded_end - MOE_BLOCK, -1).astype(I32)
    nonempty = counts > 0
    ordinal = jnp.cumsum(nonempty.astype(I32)) - 1
    later = jnp.where(jnp.logical_and(nonempty[None, :], experts[None, :] > experts[:, None]), experts[None, :], N_EXPERTS)
    nxt = jnp.min(later, axis=1)
    nxt = jnp.where(nxt == N_EXPERTS, -1, nxt)
    w_slot = (jnp.take(ordinal, block_e) % 2).astype(I32)
    next_e = jnp.take(nxt, block_e).astype(I32)
    return dest.astype(I32), block_e, n_used, zrow, w_slot, next_e, n_blocks * MOE_BLOCK


def _block_diag_dense(w):
    g, i, j = w.shape
    out = jnp.zeros((g * i, g * j), w.dtype)
    for n in range(g):
        out = out.at[n * i:(n + 1) * i, n * j:(n + 1) * j].set(w[n])
    return out


def _pad_heads(w, heads, axis):
    shape = list(w.shape)
    shape[axis:axis + 1] = [heads, shape[axis] // heads]
    w = w.reshape(shape)
    pad = [(0, 0)] * w.ndim
    pad[axis + 1] = (0, LANES - shape[axis + 1])
    w = jnp.pad(w, pad)
    shape[axis:axis + 2] = [heads * LANES]
    return w.reshape(shape)


_ROPE_SWAP = np.concatenate([np.arange(8, 16), np.arange(0, 8), np.arange(24, 32), np.arange(16, 24)])


def _prep_in_weight(w_in):
    d = w_in.shape[0]
    o = np.cumsum([0, LRU_WIDTH, LRU_WIDTH, MLA_Q_RANK, MLA_KV_RANK, MLA_ROPE, ML_WIDTH, ML_WIDTH])
    a_xg = w_in[:, o[0]:o[2]]
    b_qkv = w_in[:, o[2]:o[4]]
    kr = w_in[:, o[4]:o[5]]
    z_nope = jnp.zeros((d, MLA_NOPE), w_in.dtype)
    z_tail = jnp.zeros((d, LANES - MLA_NOPE - MLA_ROPE), w_in.dtype)
    m_x = _pad_heads(w_in[:, o[5]:o[6]], ML_HEADS, 1)
    m_z = _pad_heads(w_in[:, o[6]:o[7]], ML_HEADS, 1)
    return jnp.concatenate([a_xg, b_qkv, z_nope, kr, z_tail, z_nope, kr[:, _ROPE_SWAP], z_tail, m_x, m_z],
                           axis=1).astype(BF16)


def _prep_out_weight(w_out):
    a = w_out[:LRU_WIDTH]
    b = w_out[LRU_WIDTH:LRU_WIDTH + MLA_HEADS * MLA_V]
    c = _pad_heads(w_out[LRU_WIDTH + MLA_HEADS * MLA_V:], ML_HEADS, 0)
    return jnp.concatenate([a, b, c], axis=0).astype(BF16)


def _prep_mla(q_norm_g, w_qb, kv_norm_g, w_kvb):
    scale = (MLA_NOPE + MLA_ROPE) ** -0.5 * math.log2(math.e)
    rq = w_qb.shape[0]
    wq = w_qb.reshape(rq, MLA_HEADS, MLA_NOPE + MLA_ROPE) * scale
    nope, rope = wq[..., :MLA_NOPE], wq[..., MLA_NOPE:]
    z32 = jnp.zeros((rq, MLA_HEADS, LANES - MLA_NOPE - MLA_ROPE), w_qb.dtype)
    w1 = jnp.concatenate([nope, rope, z32], axis=-1).reshape(rq, MLA_HEADS * LANES)
    w2 = jnp.concatenate([jnp.zeros_like(nope), rope[..., _ROPE_SWAP], z32], axis=-1).reshape(rq, MLA_HEADS * LANES)
    rk = w_kvb.shape[0]
    wkv = w_kvb.reshape(rk, MLA_HEADS, MLA_NOPE + MLA_V)
    z64 = jnp.zeros((rk, MLA_HEADS, LANES - MLA_NOPE), w_kvb.dtype)
    wk = jnp.concatenate([wkv[..., :MLA_NOPE], z64], axis=-1).reshape(rk, MLA_HEADS * LANES)
    wv = jnp.concatenate([wkv[..., MLA_NOPE:], z64], axis=-1).reshape(rk, MLA_HEADS * LANES)
    return {"gq": q_norm_g.reshape(1, -1), "w1": w1.astype(BF16), "w2": w2.astype(BF16),
            "gkv": kv_norm_g.reshape(1, -1), "wk": wk.astype(BF16), "wv": wv.astype(BF16)}


def _prep_mlstm(conv_w, conv_b, wq, wk, wv, w_gate_d, b_gate_d, norm_g, skip):
    def proj(w):
        return _pad_heads(_pad_heads(_block_diag_dense(w), ML_HEADS, 0), ML_HEADS, 1)

    wk_p = proj(wk) * (ML_HEAD_DIM ** -0.5)
    wg = jnp.concatenate([_pad_heads(w_gate_d[i * ML_WIDTH:(i + 1) * ML_WIDTH], ML_HEADS, 0) for i in range(3)], axis=0)
    ng = w_gate_d.shape[1]
    wg = jnp.pad(wg, ((0, 0), (0, LANES - ng)))
    b_gate_d = jnp.pad(b_gate_d, (0, LANES - ng))
    gate_rows = 2 * SUBLANES
    return {
        "cw": _pad_heads(conv_w, ML_HEADS, 1), "cb": _pad_heads(conv_b.reshape(1, -1), ML_HEADS, 1),
        "wq": proj(wq).astype(BF16), "wk": wk_p.astype(BF16), "wkt": wk_p.T.astype(BF16), "wv": proj(wv).astype(BF16),
        "wg": wg.astype(BF16), "wgt": wg.T[:gate_rows].astype(BF16),
        "bg": b_gate_d.reshape(1, -1), "bgt": b_gate_d[:gate_rows].reshape(-1, 1),
        "ng": _pad_heads(norm_g.reshape(1, -1), ML_HEADS, 1), "sk": _pad_heads(skip.reshape(1, -1), ML_HEADS, 1),
    }


def _rope_tables(seq, ctx_len):
    rows = seq // GRID_W
    row = jnp.repeat(jnp.arange(rows, dtype=I32), GRID_W)
    col = jnp.tile(jnp.arange(GRID_W, dtype=I32), rows)
    freqs = ROPE_BASE ** (-jnp.arange(ROPE_AXIS_FREQ, dtype=F32) / ROPE_AXIS_FREQ)
    ang_r, ang_c = row[:, None] * freqs, col[:, None] * freqs
    cos32 = jnp.concatenate([jnp.cos(ang_r), jnp.cos(ang_r), jnp.cos(ang_c), jnp.cos(ang_c)], axis=1)
    sin32 = jnp.concatenate([-jnp.sin(ang_r), jnp.sin(ang_r), -jnp.sin(ang_c), jnp.sin(ang_c)], axis=1)
    cos32 = jnp.concatenate([cos32, jnp.ones((ctx_len, MLA_ROPE), F32)], axis=0)
    sin32 = jnp.concatenate([sin32, jnp.zeros((ctx_len, MLA_ROPE), F32)], axis=0)
    n = seq + ctx_len
    tail = jnp.zeros((n, LANES - MLA_NOPE - MLA_ROPE), F32)
    cos_t = jnp.concatenate([jnp.ones((n, MLA_NOPE), F32), cos32, tail], axis=1)
    sin_t = jnp.concatenate([jnp.zeros((n, MLA_NOPE), F32), sin32, tail], axis=1)
    return cos_t, sin_t


def kernel(x, c, ctx, c_ctx, norm1_g, norm2_g, w_mod, b_mod, w_in, w_out, lru_conv_w, lru_conv_b, lru_wa, lru_ba,
           lru_wx, lru_bx, lru_lambda, mla_q_norm_g, mla_w_qb, mla_kv_norm_g, mla_w_kvb, ml_conv_w, ml_conv_b,
           ml_wq, ml_wk, ml_wv, ml_w_gate, ml_b_gate, ml_norm_g, ml_skip, w_router, b_router, w_gu, b_gu, w_down,
           b_down, final_g):
    bsz, seq, d = x.shape
    ctx_len = ctx.shape[1]
    depth = w_mod.shape[0]
    assert ctx_len == TILE and seq % ATTN_Q_TILE == 0 and bsz + 1 <= SUBLANES
    t_lat = bsz * seq
    t_all = t_lat + bsz * ctx_len
    geo = {"batch": bsz, "lpb": seq // TILE, "n_lat": t_lat // TILE, "ctx_row": bsz, "t_all": t_all}

    cv = jnp.zeros((SUBLANES, d), F32).at[:bsz].set(c).at[bsz].set(c_ctx)
    mod = _modulation(cv, w_mod, b_mod)
    cos_t, sin_t = _rope_tables(seq, ctx_len)
    x_all = jnp.concatenate([x.reshape(t_lat, d), ctx.reshape(bsz * ctx_len, d)], axis=0)

    out = None
    for l in range(depth):
        last = l == depth - 1
        u_a, u_b, u_m = _in_proj(x_all, mod[l], norm1_g[l], _prep_in_weight(w_in[l]), geo)

        hf = None
        for dd in range(2):
            wg = jnp.concatenate([_block_diag_dense(lru_wa[l, dd]), _block_diag_dense(lru_wx[l, dd])], axis=1)
            bg = jnp.concatenate([lru_ba[l, dd], lru_bx[l, dd]])
            hf = _lru_dir(u_a, hf, lru_conv_w[l], lru_conv_b[l], wg.astype(BF16), bg, lru_lambda[l, dd], geo, dd == 1)
        ya = hf

        q, k, v = _mla_proj(u_b, cos_t, sin_t, _prep_mla(mla_q_norm_g[l], mla_w_qb[l], mla_kv_norm_g[l], mla_w_kvb[l]), geo)
        tq = ATTN_Q_TILE
        yb = _flash(q, k, v, tq, 0, seq // tq, 0, seq + ctx_len, 0, t_lat)
        if not last:
            yb_c = _flash(q, k, v, TILE, seq // TILE, 1, seq, ctx_len, 0, bsz * ctx_len)
            yb = jnp.concatenate([yb, yb_c], axis=0)

        hf = None
        for dd in range(2):
            wts = _prep_mlstm(ml_conv_w[l], ml_conv_b[l], ml_wq[l], ml_wk[l], ml_wv[l], ml_w_gate[l, dd],
                              ml_b_gate[l, dd], ml_norm_g[l], ml_skip[l])
            hf = _mlstm_dir(u_m, hf, wts, geo, dd == 1)
        yc = hf

        n_tiles = (t_lat if last else t_all) // TILE
        x_mid, h2, te, tg, rk, cnt = _out_proj(ya, yb, yc, x_all, mod[l], _prep_out_weight(w_out[l]), norm2_g[l],
                                               w_router[l], b_router[l], geo, n_tiles)
        t_moe = n_tiles * TILE
        dest, block_e, n_used, zrow, w_slot, next_e, n_slots = _route(te, rk, cnt, t_moe)
        x_sorted = _moe_dispatch(h2, dest, zrow, n_used, n_slots)
        y_sorted = _moe_experts(x_sorted, block_e, n_used, w_slot, next_e, w_gu, b_gu, w_down, b_down, l)
        x_all = _moe_combine(y_sorted, dest, tg, x_mid, mod[l], final_g, geo, last)
        if last:
            out = x_all.reshape(bsz, seq, d)
    return out
```
